```python
import math
import jax
import jax.numpy as jnp
from jax import lax
import numpy as np

D_MODEL = 1024
BATCH = 4
SEQ = 4096
DEPTH = 2
DEC_BATCH = 16
DEC_SEQ = 64
PAST_LEN = 1024

CHUNK = 64
CONV_WIDTH = 4
SSD_WIDTH = D_MODEL
SSD_HEAD_DIM = 64
SSD_HEADS = SSD_WIDTH // SSD_HEAD_DIM
SSD_GROUPS = 4
SSD_STATE = 128
SSD_CONV_DIM = SSD_WIDTH + 2 * SSD_GROUPS * SSD_STATE
ATT_HEADS = 16
ATT_HEAD_DIM = D_MODEL // ATT_HEADS
ATT_WIDTH = ATT_HEADS * ATT_HEAD_DIM
ATT_LEFT_CHUNKS = 8
ATT_LEFT = ATT_LEFT_CHUNKS * CHUNK
REL_CLIP = 128
ATT_SCALE = ATT_HEAD_DIM ** -0.5
LRU_WIDTH = D_MODEL
LRU_BLOCKS = 16
LRU_BLOCK = LRU_WIDTH // LRU_BLOCKS
LRU_C = 8.0
N_BRANCH = 3
N_EXPERTS = 64
N_GROUPS = 8
TOPK_GROUPS = 4
TOP_K = 8
EXPERT_HIDDEN = D_MODEL // 4
SHARED_HIDDEN = EXPERT_HIDDEN
ROUTED_SCALE = 2.5
DN_ALPHA = (2.0 * DEPTH) ** 0.25
DN_BETA = (8.0 * DEPTH) ** -0.25
LN_EPS = 1e-5
RMS_EPS = 1e-5
IN_SPLITS = (SSD_WIDTH, SSD_CONV_DIM, SSD_HEADS, ATT_WIDTH, ATT_WIDTH, ATT_WIDTH, LRU_WIDTH, LRU_WIDTH, N_BRANCH * D_MODEL)
IN_COLS = sum(IN_SPLITS)

kernel_name = 'hybrid_streaming_encoder_step'


def split_cols(proj):
    parts = []
    start = 0
    for width in IN_SPLITS:
        parts.append(proj[..., start:start + width])
        start += width
    return parts


def layer_norm(x, g, b):
    xf = x.astype(jnp.float32)
    mu = jnp.mean(xf, axis=-1, keepdims=True)
    var = jnp.mean(jnp.square(xf - mu), axis=-1, keepdims=True)
    return ((xf - mu) * lax.rsqrt(var + LN_EPS) * g + b).astype(x.dtype)


def gated_group_rmsnorm(y, z, w):
    b, t, c = y.shape
    v = (y * jax.nn.silu(z)).astype(jnp.float32).reshape(b, t, SSD_GROUPS, c // SSD_GROUPS)
    v = v * lax.rsqrt(jnp.mean(jnp.square(v), axis=-1, keepdims=True) + RMS_EPS)
    return (v.reshape(b, t, c) * w).astype(y.dtype)


def causal_conv(x, buf, w, bias):
    t = x.shape[1]
    xp = jnp.concatenate([buf.astype(x.dtype), x], axis=1)
    y = bias + xp[:, 0:t] * w[0]
    for k in range(1, CONV_WIDTH):
        y = y + xp[:, k:k + t] * w[k]
    return y, xp[:, t:]


def ssd_scan(xh, dt, a, bm, cm, h0, chunk):
    b, t, h, p = xh.shape
    g, n = bm.shape[2], bm.shape[3]
    r = h // g
    nc = t // chunk
    x = (xh * dt[..., None]).reshape(b, nc, chunk, g, r, p)
    da = (dt * a).reshape(b, nc, chunk, g, r).astype(jnp.float32)
    bc = bm.reshape(b, nc, chunk, g, n)
    cc = cm.reshape(b, nc, chunk, g, n)
    cs = jnp.cumsum(da, axis=2)
    causal = jnp.tril(jnp.ones((chunk, chunk), bool))[None, None, :, :, None, None]
    seg = cs[:, :, :, None] - cs[:, :, None, :]
    decay = jnp.where(causal, jnp.exp(jnp.where(causal, seg, 0.0)), 0.0).astype(x.dtype)
    cb = jnp.einsum('bclgn,bcsgn->bclsg', cc, bc)
    y_diag = jnp.einsum('bclsgr,bcsgrp->bclgrp', cb[..., None] * decay, x)
    to_end = jnp.exp(cs[:, :, -1:] - cs).astype(x.dtype)
    chunk_states = jnp.einsum('bclgn,bclgr,bclgrp->bcgrpn', bc, to_end, x)
    chunk_decay = jnp.exp(cs[:, :, -1]).astype(x.dtype)

    def step(hc, inp):
        dec, st = inp
        return dec[..., None, None] * hc + st, hc

    h_last, h_prev = lax.scan(step, h0.astype(x.dtype).reshape(b, g, r, p, n),
                              (jnp.moveaxis(chunk_decay, 1, 0), jnp.moveaxis(chunk_states, 1, 0)))
    h_prev = jnp.moveaxis(h_prev, 0, 1)
    y_off = jnp.einsum('bclgn,bcgrpn,bclgr->bclgrp', cc, h_prev, jnp.exp(cs).astype(x.dtype))
    return (y_diag + y_off).reshape(b, t, h, p), h_last.reshape(b, h, p, n)


def ssd_mixer(z, xbc, dt_raw, conv_buf, h0, conv_w, conv_b, dt_bias, a_log, d_skip, norm_w):
    b, t, _ = z.shape
    chunk = CHUNK if t % CHUNK == 0 else t
    xbc, new_buf = causal_conv(xbc, conv_buf, conv_w, conv_b)
    xbc = jax.nn.silu(xbc)
    xs, bm, cm = jnp.split(xbc, [SSD_WIDTH, SSD_WIDTH + SSD_GROUPS * SSD_STATE], axis=-1)
    xh = xs.reshape(b, t, SSD_HEADS, SSD_HEAD_DIM)
    dt = jax.nn.softplus(dt_raw + dt_bias)
    a = -jnp.exp(a_log)
    y, h_last = ssd_scan(xh, dt, a, bm.reshape(b, t, SSD_GROUPS, SSD_STATE),
                         cm.reshape(b, t, SSD_GROUPS, SSD_STATE), h0, chunk)
    y = y + xh * d_skip[:, None]
    y = gated_group_rmsnorm(y.reshape(b, t, SSD_WIDTH), z, norm_w)
    return y, new_buf, h_last


def rel_bias_matrix(table, n_q, n_k, offset):
    rel = jnp.arange(n_q)[:, None] + offset - jnp.arange(n_k)[None, :]
    idx = jnp.clip(rel, -REL_CLIP, REL_CLIP) + REL_CLIP
    return table[:, idx]


def band_attention_prompt(q, k, v, table):
    b, t, h, d = q.shape
    nc = t // CHUNK
    band = ATT_LEFT + CHUNK
    pad = ((0, 0), (ATT_LEFT, 0), (0, 0), (0, 0))
    kp = jnp.pad(k, pad)
    vp = jnp.pad(v, pad)
    bias = rel_bias_matrix(table, CHUNK, band, ATT_LEFT)
    key_idx = jnp.arange(band)
    qc = jnp.moveaxis(q.reshape(b, nc, CHUNK, h, d), 1, 0)

    def one_chunk(args):
        c, qb = args
        kb = lax.dynamic_slice_in_dim(kp, c * CHUNK, band, axis=1)
        vb = lax.dynamic_slice_in_dim(vp, c * CHUNK, band, axis=1)
        s = jnp.einsum('bihd,bjhd->bhij', qb, kb).astype(jnp.float32) * ATT_SCALE + bias
        valid = key_idx >= ATT_LEFT - c * CHUNK
        s = jnp.where(valid, s, -jnp.inf)
        pr = jax.nn.softmax(s, axis=-1).astype(v.dtype)
        return jnp.einsum('bhij,bjhd->bihd', pr, vb)

    out = lax.map(one_chunk, (jnp.arange(nc), qc))
    return jnp.moveaxis(out, 0, 1).reshape(b, t, h * d)


def band_attention_sample(q, k, v, k_cache, v_cache, table):
    b, t, h, d = q.shape
    w = k_cache.shape[1]
    kk = jnp.concatenate([k_cache.astype(k.dtype), k], axis=1)
    vv = jnp.concatenate([v_cache.astype(v.dtype), v], axis=1)
    bias = rel_bias_matrix(table, t, w + t, w)
    s = jnp.einsum('bihd,bjhd->bhij', q, kk).astype(jnp.float32) * ATT_SCALE + bias
    pr = jax.nn.softmax(s, axis=-1).astype(v.dtype)
    return jnp.einsum('bhij,bjhd->bihd', pr, vv).reshape(b, t, h * d)


def rglru_mixer(xl, yl, conv_buf, h0, conv_w, conv_b, wa, ba, wx, bx, lam):
    b, t, _ = xl.shape
    xc, new_buf = causal_conv(xl, conv_buf, conv_w, conv_b)
    xb = xc.reshape(b, t, LRU_BLOCKS, LRU_BLOCK)
    gate_r = jax.nn.sigmoid(jnp.einsum('btkd,kde->btke', xb, wa).reshape(b, t, LRU_WIDTH) + ba)
    gate_i = jax.nn.sigmoid(jnp.einsum('btkd,kde->btke', xb, wx).reshape(b, t, LRU_WIDTH) + bx)
    log_a = LRU_C * gate_r * jax.nn.log_sigmoid(lam)
    a = jnp.exp(log_a)
    u = jnp.sqrt(-jnp.expm1(2.0 * log_a)) * (gate_i * xc)

    def combine(left, right):
        a1, b1 = left
        a2, b2 = right
        return a1 * a2, a2 * b1 + b2

    a_cum, u_cum = lax.associative_scan(combine, (a, u), axis=1)
    h = a_cum * h0.astype(u.dtype)[:, None] + u_cum
    out = h * jax.nn.gelu(yl, approximate=True)
    return out, new_buf, h[:, -1]


def moe(x, router_w, router_bias, w_gate, w_up, w_down, sh_gate, sh_up, sh_down):
    b, t, d = x.shape
    xt = x.reshape(b * t, d)
    scores = jax.nn.sigmoid((xt @ router_w).astype(jnp.float32))
    sel = scores + router_bias.astype(jnp.float32)
    grp = sel.reshape(-1, N_GROUPS, N_EXPERTS // N_GROUPS)
    gscore = jnp.sum(lax.top_k(grp, 2)[0], axis=-1)
    _, gidx = lax.top_k(gscore, TOPK_GROUPS)
    gmask = jnp.sum(jax.nn.one_hot(gidx, N_GROUPS, dtype=jnp.float32), axis=1) > 0
    emask = jnp.repeat(gmask, N_EXPERTS // N_GROUPS, axis=1)
    _, eidx = lax.top_k(jnp.where(emask, sel, -jnp.inf), TOP_K)
    wsel = jnp.take_along_axis(scores, eidx, axis=-1)
    wsel = wsel / jnp.sum(wsel, axis=-1, keepdims=True) * ROUTED_SCALE
    gates = jnp.sum(jax.nn.one_hot(eidx, N_EXPERTS, dtype=jnp.float32) * wsel[..., None], axis=1)
    gates = gates.astype(x.dtype)

    def expert_step(acc, ew):
        wg, wu, wd, g = ew
        hid = jax.nn.silu(xt @ wg) * (xt @ wu)
        return acc + g[:, None] * (hid @ wd), None

    routed, _ = lax.scan(expert_step, jnp.zeros_like(xt), (w_gate, w_up, w_down, gates.T))
    shared = (jax.nn.silu(xt @ sh_gate) * (xt @ sh_up)) @ sh_down
    return (routed + shared).reshape(b, t, d)


def trunk_layer(x, ssd_buf, ssd_h, lru_buf, lru_h, kv_cache, p):
    b, t, _ = x.shape
    proj = x @ p['w_in']
    z, xbc, dt_raw, q, k, v, xl, yl, g = split_cols(proj)
    ssd_out, ssd_buf_new, ssd_h_new = ssd_mixer(z, xbc, dt_raw, ssd_buf, ssd_h, p['ssd_conv_w'], p['ssd_conv_b'],
                                                p['ssd_dt_bias'], p['ssd_a_log'], p['ssd_d'], p['ssd_norm_w'])
    qh = q.reshape(b, t, ATT_HEADS, ATT_HEAD_DIM)
    kh = k.reshape(b, t, ATT_HEADS, ATT_HEAD_DIM)
    vh = v.reshape(b, t, ATT_HEADS, ATT_HEAD_DIM)
    if kv_cache is None:
        att_out = band_attention_prompt(qh, kh, vh, p['att_rel_bias'])
        keep = min(ATT_LEFT, t)
        k_keep, v_keep = kh[:, t - keep:], vh[:, t - keep:]
    else:
        att_out = band_attention_sample(qh, kh, vh, kv_cache[0], kv_cache[1], p['att_rel_bias'])
        k_keep, v_keep = kh, vh
    lru_out, lru_buf_new, lru_h_new = rglru_mixer(xl, yl, lru_buf, lru_h, p['lru_conv_w'], p['lru_conv_b'],
                                                  p['lru_wa'], p['lru_ba'], p['lru_wx'], p['lru_bx'], p['lru_lambda'])
    gates = jax.nn.sigmoid(g + p['gate_bias']).reshape(b, t, N_BRANCH, D_MODEL)
    merged = (gates[:, :, 0] * (ssd_out @ p['w_ssd_proj'])
              + gates[:, :, 1] * (att_out @ p['w_att_proj'])
              + gates[:, :, 2] * (lru_out @ p['w_lru_proj']))
    x1 = layer_norm(DN_ALPHA * x + merged @ p['w_out'], p['ln1_g'], p['ln1_b'])
    ffn = moe(x1, p['router_w'], p['router_bias'], p['exp_w_gate'], p['exp_w_up'], p['exp_w_down'],
              p['sh_w_gate'], p['sh_w_up'], p['sh_w_down'])
    x2 = layer_norm(DN_ALPHA * x1 + ffn, p['ln2_g'], p['ln2_b'])
    return x2, (ssd_buf_new, ssd_h_new, k_keep, v_keep, lru_buf_new, lru_h_new)


def stack_state(states, i):
    return jnp.stack([s[i] for s in states], axis=0)


def setup_inputs(seed: int = 0) -> dict:
    key = jax.random.key(seed)
    ks = iter(jax.random.split(key, 48))

    def nrm(shape, scale):
        return scale * jax.random.normal(next(ks), shape, jnp.float32)

    def unif(shape, lo, hi):
        return jax.random.uniform(next(ks), shape, jnp.float32, lo, hi)

    att_rows = min(ATT_LEFT, PAST_LEN)
    dt0 = jnp.exp(unif((DEPTH, SSD_HEADS), math.log(1e-3), math.log(1e-1)))
    a0 = unif((DEPTH, LRU_WIDTH), 0.9, 0.999) ** (1.0 / LRU_C)
    return {
        'x_prompt': nrm((BATCH, SEQ, D_MODEL), 1.0),
        'x_sample': nrm((DEC_BATCH, DEC_SEQ, D_MODEL), 1.0),
        'cache_ssd_conv': nrm((DEPTH, DEC_BATCH, CONV_WIDTH - 1, SSD_CONV_DIM), 1.0),
        'state_ssd': nrm((DEPTH, DEC_BATCH, SSD_HEADS, SSD_HEAD_DIM, SSD_STATE), 0.3),
        'cache_att_k': nrm((DEPTH, DEC_BATCH, att_rows, ATT_HEADS, ATT_HEAD_DIM), 1.0),
        'cache_att_v': nrm((DEPTH, DEC_BATCH, att_rows, ATT_HEADS, ATT_HEAD_DIM), 1.0),
        'cache_lru_conv': nrm((DEPTH, DEC_BATCH, CONV_WIDTH - 1, LRU_WIDTH), 1.0),
        'state_lru': nrm((DEPTH, DEC_BATCH, LRU_WIDTH), 0.5),
        'w_in': nrm((DEPTH, D_MODEL, IN_COLS), D_MODEL ** -0.5),
        'ssd_conv_w': nrm((DEPTH, CONV_WIDTH, SSD_CONV_DIM), 0.5),
        'ssd_conv_b': nrm((DEPTH, SSD_CONV_DIM), 0.02),
        'ssd_dt_bias': dt0 + jnp.log(-jnp.expm1(-dt0)),
        'ssd_a_log': jnp.log(unif((DEPTH, SSD_HEADS), 1.0, 16.0)),
        'ssd_d': 1.0 + nrm((DEPTH, SSD_HEADS), 0.02),
        'ssd_norm_w': 1.0 + nrm((DEPTH, SSD_WIDTH), 0.02),
        'att_rel_bias': nrm((DEPTH, ATT_HEADS, 2 * REL_CLIP + 1), 0.2),
        'lru_conv_w': nrm((DEPTH, CONV_WIDTH, LRU_WIDTH), 0.5),
        'lru_conv_b': nrm((DEPTH, LRU_WIDTH), 0.02),
        'lru_wa': nrm((DEPTH, LRU_BLOCKS, LRU_BLOCK, LRU_BLOCK), LRU_BLOCK ** -0.5),
        'lru_ba': nrm((DEPTH, LRU_WIDTH), 0.02),
        'lru_wx': nrm((DEPTH, LRU_BLOCKS, LRU_BLOCK, LRU_BLOCK), LRU_BLOCK ** -0.5),
        'lru_bx': nrm((DEPTH, LRU_WIDTH), 0.02),
        'lru_lambda': jnp.log(a0) - jnp.log1p(-a0),
        'gate_bias': nrm((DEPTH, N_BRANCH * D_MODEL), 0.02),
        'w_ssd_proj': nrm((DEPTH, SSD_WIDTH, D_MODEL), DN_BETA * SSD_WIDTH ** -0.5),
        'w_att_proj': nrm((DEPTH, ATT_WIDTH, D_MODEL), DN_BETA * ATT_WIDTH ** -0.5),
        'w_lru_proj': nrm((DEPTH, LRU_WIDTH, D_MODEL), DN_BETA * LRU_WIDTH ** -0.5),
        'w_out': nrm((DEPTH, D_MODEL, D_MODEL), DN_BETA * D_MODEL ** -0.5),
        'ln1_g': 1.0 + nrm((DEPTH, D_MODEL), 0.02),
        'ln1_b': nrm((DEPTH, D_MODEL), 0.02),
        'router_w': nrm((DEPTH, D_MODEL, N_EXPERTS), D_MODEL ** -0.5),
        'router_bias': nrm((DEPTH, N_EXPERTS), 0.01),
        'exp_w_gate': nrm((DEPTH, N_EXPERTS, D_MODEL, EXPERT_HIDDEN), D_MODEL ** -0.5),
        'exp_w_up': nrm((DEPTH, N_EXPERTS, D_MODEL, EXPERT_HIDDEN), D_MODEL ** -0.5),
        'exp_w_down': nrm((DEPTH, N_EXPERTS, EXPERT_HIDDEN, D_MODEL), DN_BETA * EXPERT_HIDDEN ** -0.5),
        'sh_w_gate': nrm((DEPTH, D_MODEL, SHARED_HIDDEN), D_MODEL ** -0.5),
        'sh_w_up': nrm((DEPTH, D_MODEL, SHARED_HIDDEN), D_MODEL ** -0.5),
        'sh_w_down': nrm((DEPTH, SHARED_HIDDEN, D_MODEL), DN_BETA * SHARED_HIDDEN ** -0.5),
        'ln2_g': 1.0 + nrm((DEPTH, D_MODEL), 0.02),
        'ln2_b': nrm((DEPTH, D_MODEL), 0.02),
    }


def reference(x_prompt, x_sample, cache_ssd_conv, state_ssd, cache_att_k, cache_att_v, cache_lru_conv, state_lru,
              w_in, ssd_conv_w, ssd_conv_b, ssd_dt_bias, ssd_a_log, ssd_d, ssd_norm_w, att_rel_bias,
              lru_conv_w, lru_conv_b, lru_wa, lru_ba, lru_wx, lru_bx, lru_lambda, gate_bias,
              w_ssd_proj, w_att_proj, w_lru_proj, w_out, ln1_g, ln1_b, router_w, router_bias,
              exp_w_gate, exp_w_up, exp_w_down, sh_w_gate, sh_w_up, sh_w_down, ln2_g, ln2_b):
    bp = x_prompt.shape[0]
    xp, xs = x_prompt, x_sample
    states_p, states_s = [], []
    for l in range(DEPTH):
        p = dict(w_in=w_in[l], ssd_conv_w=ssd_conv_w[l], ssd_conv_b=ssd_conv_b[l], ssd_dt_bias=ssd_dt_bias[l],
                 ssd_a_log=ssd_a_log[l], ssd_d=ssd_d[l], ssd_norm_w=ssd_norm_w[l], att_rel_bias=att_rel_bias[l],
                 lru_conv_w=lru_conv_w[l], lru_conv_b=lru_conv_b[l], lru_wa=lru_wa[l], lru_ba=lru_ba[l],
                 lru_wx=lru_wx[l], lru_bx=lru_bx[l], lru_lambda=lru_lambda[l], gate_bias=gate_bias[l],
                 w_ssd_proj=w_ssd_proj[l], w_att_proj=w_att_proj[l], w_lru_proj=w_lru_proj[l], w_out=w_out[l],
                 ln1_g=ln1_g[l], ln1_b=ln1_b[l], router_w=router_w[l], router_bias=router_bias[l],
                 exp_w_gate=exp_w_gate[l], exp_w_up=exp_w_up[l], exp_w_down=exp_w_down[l],
                 sh_w_gate=sh_w_gate[l], sh_w_up=sh_w_up[l], sh_w_down=sh_w_down[l],
                 ln2_g=ln2_g[l], ln2_b=ln2_b[l])
        xp, st_p = trunk_layer(
            xp,
            jnp.zeros((bp, CONV_WIDTH - 1, SSD_CONV_DIM), xp.dtype),
            jnp.zeros((bp, SSD_HEADS, SSD_HEAD_DIM, SSD_STATE), xp.dtype),
            jnp.zeros((bp, CONV_WIDTH - 1, LRU_WIDTH), xp.dtype),
            jnp.zeros((bp, LRU_WIDTH), xp.dtype),
            None, p)
        xs, st_s = trunk_layer(xs, cache_ssd_conv[l], state_ssd[l], cache_lru_conv[l], state_lru[l],
                               (cache_att_k[l], cache_att_v[l]), p)
        states_p.append(st_p)
        states_s.append(st_s)
    p_ssd_conv, s_ssd_conv = stack_state(states_p, 0), stack_state(states_s, 0)
    p_ssd_state, s_ssd_state = stack_state(states_p, 1), stack_state(states_s, 1)
    p_att_k, s_att_k = stack_state(states_p, 2), stack_state(states_s, 2)
    p_att_v, s_att_v = stack_state(states_p, 3), stack_state(states_s, 3)
    p_lru_conv, s_lru_conv = stack_state(states_p, 4), stack_state(states_s, 4)
    p_lru_state, s_lru_state = stack_state(states_p, 5), stack_state(states_s, 5)
    return (xp, xs, p_ssd_conv, s_ssd_conv, p_ssd_state, s_ssd_state, p_att_k, s_att_k, p_att_v, s_att_v,
            p_lru_conv, s_lru_conv, p_lru_state, s_lru_state)
```

```python
import functools
import math

import jax
import jax.numpy as jnp
from jax import lax
from jax.experimental import pallas as pl
from jax.experimental.pallas import tpu as pltpu

F32 = jnp.float32
BF16 = jnp.bfloat16
HIGHEST = lax.Precision.HIGHEST

D_MODEL = 1024
DEPTH = 2
CHUNK = 64
CONV_WIDTH = 4
SSD_HEADS = 16
SSD_HEAD_DIM = 64
SSD_GROUPS = 4
SSD_STATE = 128
SSD_CONV_DIM = D_MODEL + 2 * SSD_GROUPS * SSD_STATE
ATT_HEADS = 16
ATT_HEAD_DIM = 64
ATT_LEFT = 8 * CHUNK
ATT_BAND = ATT_LEFT + CHUNK
REL_CLIP = 128
ATT_SCALE = ATT_HEAD_DIM ** -0.5
LRU_BLOCKS = 16
LRU_C = 8.0
N_BRANCH = 3
N_EXPERTS = 64
N_GROUPS = 8
GROUP_SIZE = N_EXPERTS // N_GROUPS
TOPK_GROUPS = 4
TOP_K = 8
EXPERT_HIDDEN = D_MODEL // 4
ROUTED_SCALE = 2.5
DN_ALPHA = (2.0 * DEPTH) ** 0.25
LN_EPS = 1e-5
RMS_EPS = 1e-5

LANES = 128
CONV_PAD = 8
GATE_COLS = 128

COL_G, COL_Z, COL_XBC2, COL_Q, COL_K, COL_V, COL_XL, COL_YL = 0, 3, 2, 6, 7, 8, 9, 10
MAIN_COLS = 11 * D_MODEL


def _params(semantics, vmem_mb):
    return pltpu.CompilerParams(dimension_semantics=semantics, vmem_limit_bytes=vmem_mb * 2 ** 20)


def _tile(n, candidates):
    for c in candidates:
        if n % c == 0:
            return c
    raise ValueError(f"no tile for {n}")


def _nn(a, b, precision=None):
    return jnp.dot(a, b, preferred_element_type=F32, precision=precision)


def _nt(a, b, precision=None):
    return lax.dot_general(a, b, (((1,), (1,)), ((), ())), preferred_element_type=F32, precision=precision)


def _tn(a, b):
    return lax.dot_general(a, b, (((0,), (0,)), ((), ())), preferred_element_type=F32)


def _sigmoid(x):
    return 1.0 / (1.0 + jnp.exp(-x))


def _silu(x):
    return x * _sigmoid(x)


def _softplus(x):
    return jnp.maximum(x, 0.0) + jnp.log1p(jnp.exp(-jnp.abs(x)))


def _layer_norm(y, g, b):
    mu = jnp.mean(y, axis=-1, keepdims=True)
    d = y - mu
    var = jnp.mean(d * d, axis=-1, keepdims=True)
    return d * lax.rsqrt(var + LN_EPS) * g + b


def _const_spec(shape):
    nd = len(shape)
    return pl.BlockSpec(shape, lambda *_: (0,) * nd)


def _mm_body(a_ref, b_ref, o_ref):
    o_ref[...] = _nn(a_ref[...], b_ref[...]).astype(o_ref.dtype)


def _matmul(a, b, tn, name):
    m, k = a.shape
    n = b.shape[1]
    tm = _tile(m, (1024, 512, 256, 128))
    return pl.pallas_call(
        _mm_body,
        grid=(m // tm, n // tn),
        in_specs=[pl.BlockSpec((tm, k), lambda i, j: (i, 0)), pl.BlockSpec((k, tn), lambda i, j: (0, j))],
        out_specs=pl.BlockSpec((tm, tn), lambda i, j: (i, j)),
        out_shape=jax.ShapeDtypeStruct((m, n), F32),
        compiler_params=_params(("parallel", "arbitrary"), 40),
        name=name,
    )(a, b)


def _conv_block(xp_scr, x_ref, cw_ref, cb_ref, rows):
    lo = CONV_PAD - (CONV_WIDTH - 1)
    xp_scr[CONV_PAD:CONV_PAD + rows, :] = x_ref[...]
    y = cb_ref[...] + xp_scr[lo:lo + rows, :] * cw_ref[0:1, :]
    for k in range(1, CONV_WIDTH):
        y = y + xp_scr[lo + k:lo + k + rows, :] * cw_ref[k:k + 1, :]
    tail = xp_scr[lo + rows:CONV_PAD + rows, :]
    xp_scr[lo:CONV_PAD, :] = tail
    return y, tail


def _ssd_body(z_ref, xbc_ref, dt_ref, buf0_ref, h0_ref, cw_ref, cb_ref, dtb_ref, alog_ref, dch_ref, nw_ref,
              expand_ref, y_ref, bufo_ref, ho_ref, xp_scr, h_scr, *, nc):
    c = pl.program_id(1)
    lo = CONV_PAD - (CONV_WIDTH - 1)

    @pl.when(c == 0)
    def _():
        xp_scr[lo:CONV_PAD, :] = buf0_ref[0]
        h_scr[...] = h0_ref[0]

    conv, tail = _conv_block(xp_scr, xbc_ref, cw_ref, cb_ref, CHUNK)
    bufo_ref[0] = tail
    xbc = _silu(conv)
    xs = xbc[:, :D_MODEL]
    bm = xbc[:, D_MODEL:D_MODEL + SSD_GROUPS * SSD_STATE].astype(BF16)
    cm = xbc[:, D_MODEL + SSD_GROUPS * SSD_STATE:].astype(BF16)

    dt = _softplus(dt_ref[...] + dtb_ref[...])
    da = dt * (-jnp.exp(alog_ref[...]))
    row = lax.broadcasted_iota(jnp.int32, (CHUNK, CHUNK), 0)
    col = lax.broadcasted_iota(jnp.int32, (CHUNK, CHUNK), 1)
    causal = row >= col
    cs = _nn(causal.astype(F32), da, HIGHEST)
    cs_last = cs[CHUNK - 1:CHUNK, :]
    to_end = jnp.exp(cs_last - cs)
    ecs = jnp.exp(cs)
    wide = _nn(jnp.concatenate([dt, to_end, ecs], axis=0), expand_ref[...], HIGHEST)
    dt_ch = wide[0:CHUNK]
    to_end_ch = wide[CHUNK:2 * CHUNK]
    ecs_ch = wide[2 * CHUNK:3 * CHUNK]
    dec_ch = ecs_ch[CHUNK - 1:CHUNK, :]
    eye = (lax.broadcasted_iota(jnp.int32, (LANES, LANES), 0)
           == lax.broadcasted_iota(jnp.int32, (LANES, LANES), 1)).astype(F32)
    cs_t = _nt(eye, cs, HIGHEST)

    xdt = xs * dt_ch
    xdt_b = xdt.astype(BF16)
    xw_b = (xdt * to_end_ch).astype(BF16)
    gw = SSD_HEADS // SSD_GROUPS * SSD_HEAD_DIM
    y_parts = []
    for g in range(SSD_GROUPS):
        bg = bm[:, g * SSD_STATE:(g + 1) * SSD_STATE]
        cg = cm[:, g * SSD_STATE:(g + 1) * SSD_STATE]
        gsl = slice(g * gw, (g + 1) * gw)
        cb = _nt(cg, bg)
        h_prev = h_scr[:, gsl]
        y_off = _nn(cg, h_prev.astype(BF16)) * ecs_ch[:, gsl]
        st = _tn(bg, xw_b[:, gsl])
        h_scr[:, gsl] = h_prev * dec_ch[:, gsl] + st
        diag = []
        for r in range(SSD_HEADS // SSD_GROUPS):
            h = g * (SSD_HEADS // SSD_GROUPS) + r
            seg = cs[:, h:h + 1] - cs_t[h:h + 1, :CHUNK]
            decay = jnp.where(causal, jnp.exp(jnp.where(causal, seg, 0.0)), 0.0)
            sc = (cb * decay).astype(BF16)
            diag.append(_nn(sc, xdt_b[:, h * SSD_HEAD_DIM:(h + 1) * SSD_HEAD_DIM]))
        y_parts.append(jnp.concatenate(diag, axis=-1) + y_off)
    y = jnp.concatenate(y_parts, axis=-1) + xs * dch_ref[...]
    v = y * _silu(z_ref[...])
    outs = []
    for g in range(SSD_GROUPS):
        vg = v[:, g * gw:(g + 1) * gw]
        ms = jnp.mean(vg * vg, axis=-1, keepdims=True)
        outs.append(vg * lax.rsqrt(ms + RMS_EPS))
    y_ref[...] = (jnp.concatenate(outs, axis=-1) * nw_ref[...]).astype(y_ref.dtype)

    @pl.when(c == nc - 1)
    def _():
        ho_ref[0] = h_scr[...]


def _ssd_call(proj, proj_dt, buf0, h0_t, p, nb, nc, blk0, name):
    row = lambda b, c: blk0 + b * nc + c
    in_specs = [
        pl.BlockSpec((CHUNK, D_MODEL), lambda b, c: (row(b, c), COL_Z)),
        pl.BlockSpec((CHUNK, SSD_CONV_DIM), lambda b, c: (row(b, c), COL_XBC2)),
        pl.BlockSpec((CHUNK, LANES), lambda b, c: (row(b, c), 0)),
        pl.BlockSpec((1, CONV_WIDTH - 1, SSD_CONV_DIM), lambda b, c: (b, 0, 0)),
        pl.BlockSpec((1, SSD_STATE, D_MODEL), lambda b, c: (b, 0, 0)),
        _const_spec((CONV_WIDTH, SSD_CONV_DIM)),
        _const_spec((1, SSD_CONV_DIM)),
        _const_spec((1, LANES)),
        _const_spec((1, LANES)),
        _const_spec((1, D_MODEL)),
        _const_spec((1, D_MODEL)),
        _const_spec((LANES, D_MODEL)),
    ]
    out_specs = [
        pl.BlockSpec((CHUNK, D_MODEL), lambda b, c: (b * nc + c, 0)),
        pl.BlockSpec((1, CONV_WIDTH - 1, SSD_CONV_DIM), lambda b, c: (b, 0, 0)),
        pl.BlockSpec((1, SSD_STATE, D_MODEL), lambda b, c: (b, 0, 0)),
    ]
    out_shape = [
        jax.ShapeDtypeStruct((nb * nc * CHUNK, D_MODEL), BF16),
        jax.ShapeDtypeStruct((nb, CONV_WIDTH - 1, SSD_CONV_DIM), F32),
        jax.ShapeDtypeStruct((nb, SSD_STATE, D_MODEL), F32),
    ]
    return pl.pallas_call(
        functools.partial(_ssd_body, nc=nc),
        grid=(nb, nc),
        in_specs=in_specs,
        out_specs=out_specs,
        out_shape=out_shape,
        scratch_shapes=[pltpu.VMEM((CONV_PAD + CHUNK, SSD_CONV_DIM), F32), pltpu.VMEM((SSD_STATE, D_MODEL), F32)],
        compiler_params=_params(("parallel", "arbitrary"), 40),
        name=name,
    )(proj, proj, proj_dt, buf0, h0_t, p["ssd_conv_w"], p["ssd_conv_b"], p["ssd_dt_bias"], p["ssd_a_log"],
      p["ssd_d_ch"], p["ssd_norm_w"], p["ssd_expand"])


def _att_body(q_ref, kp_ref, kc_ref, vp_ref, vc_ref, bias_ref, o_ref, k_scr, v_scr, *, rows, mask_first):
    t = pl.program_id(1)
    k_scr[0:ATT_LEFT, :] = kp_ref[...].astype(BF16)
    k_scr[ATT_LEFT:ATT_LEFT + rows, :] = kc_ref[...].astype(BF16)
    v_scr[0:ATT_LEFT, :] = vp_ref[...].astype(BF16)
    v_scr[ATT_LEFT:ATT_LEFT + rows, :] = vc_ref[...].astype(BF16)

    def chunk(i, carry):
        r0 = pl.multiple_of(i * CHUNK, CHUNK)
        if mask_first:
            pos = lax.broadcasted_iota(jnp.int32, (1, ATT_BAND), 1) + i * CHUNK
            valid = jnp.logical_or(pos >= ATT_LEFT, t > 0)
        for hp in range(ATT_HEADS // 2):
            pair = []
            for h in (2 * hp, 2 * hp + 1):
                hs = slice(h * ATT_HEAD_DIM, (h + 1) * ATT_HEAD_DIM)
                qh = (q_ref[pl.ds(r0, CHUNK), hs] * ATT_SCALE).astype(BF16)
                kw = k_scr[pl.ds(r0, ATT_BAND), hs]
                s = _nt(qh, kw) + bias_ref[h]
                if mask_first:
                    s = jnp.where(valid, s, -jnp.inf)
                m = jnp.max(s, axis=-1, keepdims=True)
                e = jnp.exp(s - m)
                den = jnp.sum(e, axis=-1, keepdims=True)
                vw = v_scr[pl.ds(r0, ATT_BAND), hs]
                pair.append(_nn(e.astype(BF16), vw) / den)
            o_ref[pl.ds(r0, CHUNK), hp * LANES:(hp + 1) * LANES] = jnp.concatenate(pair, axis=-1).astype(o_ref.dtype)
        return carry

    lax.fori_loop(0, rows // CHUNK, chunk, 0)


def _att_call(proj, k_prev, v_prev, bias, nb, nt, rows, blk0, prompt, name):
    cur = lambda col: pl.BlockSpec((rows, D_MODEL), lambda b, t: (blk0 + b * nt + t, col))
    if prompt:
        prev = lambda col: pl.BlockSpec((ATT_LEFT, D_MODEL), lambda b, t: (b * nt + jnp.maximum(t - 1, 0), col))
        kp_spec, vp_spec = prev(COL_K), prev(COL_V)
        k_prev = v_prev = proj
    else:
        kp_spec = vp_spec = pl.BlockSpec((ATT_LEFT, D_MODEL), lambda b, t: (b, 0))
    return pl.pallas_call(
        functools.partial(_att_body, rows=rows, mask_first=prompt),
        grid=(nb, nt),
        in_specs=[cur(COL_Q), kp_spec, cur(COL_K), vp_spec, cur(COL_V), _const_spec((ATT_HEADS, CHUNK, ATT_BAND))],
        out_specs=pl.BlockSpec((rows, D_MODEL), lambda b, t: (b * nt + t, 0)),
        out_shape=jax.ShapeDtypeStruct((nb * nt * rows, D_MODEL), BF16),
        scratch_shapes=[pltpu.VMEM((ATT_LEFT + rows, D_MODEL), BF16), pltpu.VMEM((ATT_LEFT + rows, D_MODEL), BF16)],
        compiler_params=_params(("parallel", "arbitrary"), 48),
        name=name,
    )(proj, k_prev, proj, v_prev, proj, bias)


def _scan64(a, u):
    row = lax.broadcasted_iota(jnp.int32, (CHUNK, 1), 0)
    d = 1
    while d < CHUNK:
        a_sh = pltpu.roll(a, d, 0)
        u_sh = pltpu.roll(u, d, 0)
        m = row >= d
        u = jnp.where(m, a * u_sh + u, u)
        a = jnp.where(m, a * a_sh, a)
        d *= 2
    return a, u


def _gelu_tanh(x):
    return 0.5 * x * (1.0 + jnp.tanh(math.sqrt(2.0 / math.pi) * (x + 0.044715 * (x * x * x))))


def _lru_body(xl_ref, yl_ref, buf0_ref, h0_ref, cw_ref, cb_ref, wa_ref, ba_ref, wx_ref, bx_ref, lam_ref,
              o_ref, bufo_ref, ho_ref, xp_scr, carry_scr, *, rows):
    c = pl.program_id(1)
    lo = CONV_PAD - (CONV_WIDTH - 1)

    @pl.when(c == 0)
    def _():
        xp_scr[lo:CONV_PAD, :] = buf0_ref[0]
        carry_scr[...] = h0_ref[0]

    xc, tail = _conv_block(xp_scr, xl_ref, cw_ref, cb_ref, rows)
    bufo_ref[0] = tail
    xcb = xc.astype(BF16)
    gate_r = _sigmoid(_nn(xcb, wa_ref[...]) + ba_ref[...])
    gate_i = _sigmoid(_nn(xcb, wx_ref[...]) + bx_ref[...])
    log_a = LRU_C * gate_r * (-_softplus(-lam_ref[...]))
    a = jnp.exp(log_a)
    u = jnp.sqrt(1.0 - jnp.exp(2.0 * log_a)) * (gate_i * xc)
    carry = carry_scr[...]
    for j in range(rows // CHUNK):
        sl = slice(j * CHUNK, (j + 1) * CHUNK)
        a_cum, u_cum = _scan64(a[sl], u[sl])
        h = a_cum * carry + u_cum
        carry = h[CHUNK - 1:CHUNK, :]
        o_ref[sl, :] = (h * _gelu_tanh(yl_ref[sl, :])).astype(o_ref.dtype)
    carry_scr[...] = carry
    ho_ref[0] = carry


def _lru_call(proj, buf0, h0, p, nb, nc, rows, blk0, name):
    row = lambda b, c: blk0 + b * nc + c
    in_specs = [
        pl.BlockSpec((rows, D_MODEL), lambda b, c: (row(b, c), COL_XL)),
        pl.BlockSpec((rows, D_MODEL), lambda b, c: (row(b, c), COL_YL)),
        pl.BlockSpec((1, CONV_WIDTH - 1, D_MODEL), lambda b, c: (b, 0, 0)),
        pl.BlockSpec((1, 1, D_MODEL), lambda b, c: (b, 0, 0)),
        _const_spec((CONV_WIDTH, D_MODEL)),
        _const_spec((1, D_MODEL)),
        _const_spec((D_MODEL, D_MODEL)),
        _const_spec((1, D_MODEL)),
        _const_spec((D_MODEL, D_MODEL)),
        _const_spec((1, D_MODEL)),
        _const_spec((1, D_MODEL)),
    ]
    out_specs = [
        pl.BlockSpec((rows, D_MODEL), lambda b, c: (b * nc + c, 0)),
        pl.BlockSpec((1, CONV_WIDTH - 1, D_MODEL), lambda b, c: (b, 0, 0)),
        pl.BlockSpec((1, 1, D_MODEL), lambda b, c: (b, 0, 0)),
    ]
    out_shape = [
        jax.ShapeDtypeStruct((nb * nc * rows, D_MODEL), BF16),
        jax.ShapeDtypeStruct((nb, CONV_WIDTH - 1, D_MODEL), F32),
        jax.ShapeDtypeStruct((nb, 1, D_MODEL), F32),
    ]
    return pl.pallas_call(
        functools.partial(_lru_body, rows=rows),
        grid=(nb, nc),
        in_specs=in_specs,
        out_specs=out_specs,
        out_shape=out_shape,
        scratch_shapes=[pltpu.VMEM((CONV_PAD + rows, D_MODEL), F32), pltpu.VMEM((1, D_MODEL), F32)],
        compiler_params=_params(("parallel", "arbitrary"), 48),
        name=name,
    )(proj, proj, buf0, h0, p["lru_conv_w"], p["lru_conv_b"], p["lru_wa_d"], p["lru_ba"], p["lru_wx_d"],
      p["lru_bx"], p["lru_lambda"])


def _merge_body(s_ref, a_ref, l_ref, g_ref, x_ref, ws_ref, wa_ref, wl_ref, wo_ref, gb_ref, lg_ref, lb_ref,
                o_ref, ob_ref):
    g = _sigmoid(g_ref[...] + gb_ref[...])
    m = (g[:, 0:D_MODEL] * _nn(s_ref[...], ws_ref[...])
         + g[:, D_MODEL:2 * D_MODEL] * _nn(a_ref[...], wa_ref[...])
         + g[:, 2 * D_MODEL:3 * D_MODEL] * _nn(l_ref[...], wl_ref[...]))
    y = DN_ALPHA * x_ref[...] + _nn(m.astype(BF16), wo_ref[...])
    x1 = _layer_norm(y, lg_ref[...], lb_ref[...])
    o_ref[...] = x1
    ob_ref[...] = x1.astype(BF16)


def _merge_call(ssd_y, att_y, lru_y, proj, x, p, name):
    n = x.shape[0]
    tm = _tile(n, (256, 128))
    rowblk = lambda w: pl.BlockSpec((tm, w), lambda i: (i, 0))
    wspec = _const_spec((D_MODEL, D_MODEL))
    return pl.pallas_call(
        _merge_body,
        grid=(n // tm,),
        in_specs=[rowblk(D_MODEL), rowblk(D_MODEL), rowblk(D_MODEL),
                  pl.BlockSpec((tm, N_BRANCH * D_MODEL), lambda i: (i, COL_G)), rowblk(D_MODEL),
                  wspec, wspec, wspec, wspec,
                  _const_spec((1, N_BRANCH * D_MODEL)), _const_spec((1, D_MODEL)), _const_spec((1, D_MODEL))],
        out_specs=[rowblk(D_MODEL), rowblk(D_MODEL)],
        out_shape=[jax.ShapeDtypeStruct((n, D_MODEL), F32), jax.ShapeDtypeStruct((n, D_MODEL), BF16)],
        compiler_params=_params(("parallel",), 48),
        name=name,
    )(ssd_y, att_y, lru_y, proj, x, p["w_ssd_proj"], p["w_att_proj"], p["w_lru_proj"], p["w_out"],
      p["gate_bias"], p["ln1_g"], p["ln1_b"])


def _first_max(x, idx, big):
    m = jnp.max(x, axis=0, keepdims=True)
    first = jnp.min(jnp.where(x == m, idx, big), axis=0, keepdims=True)
    return m, idx == first


def _router_body(x_ref, rwt_ref, rb_ref, o_ref):
    tm = x_ref.shape[0]
    logits = _nt(rwt_ref[...], x_ref[...], HIGHEST)
    scores = _sigmoid(logits)
    sel = scores + rb_ref[...]
    sub = lax.broadcasted_iota(jnp.int32, (GROUP_SIZE, tm), 0)
    gscore = jnp.zeros((N_GROUPS, tm), F32)
    for g in range(N_GROUPS):
        sg = sel[g * GROUP_SIZE:(g + 1) * GROUP_SIZE, :]
        m1, hit = _first_max(sg, sub, GROUP_SIZE)
        m2 = jnp.max(jnp.where(hit, -jnp.inf, sg), axis=0, keepdims=True)
        gscore = jnp.where(sub == g, m1 + m2, gscore)
    gsel = jnp.zeros((N_GROUPS, tm), F32)
    for _ in range(TOPK_GROUPS):
        _, hit = _first_max(gscore, sub, N_GROUPS)
        gsel = jnp.where(hit, 1.0, gsel)
        gscore = jnp.where(hit, -jnp.inf, gscore)
    masked = jnp.concatenate(
        [jnp.where(gsel[g:g + 1, :] > 0.0, sel[g * GROUP_SIZE:(g + 1) * GROUP_SIZE, :], -jnp.inf)
         for g in range(N_GROUPS)], axis=0)
    eidx = lax.broadcasted_iota(jnp.int32, (N_EXPERTS, tm), 0)
    chosen = jnp.zeros((N_EXPERTS, tm), F32)
    for _ in range(TOP_K):
        _, hit = _first_max(masked, eidx, N_EXPERTS)
        chosen = jnp.where(hit, 1.0, chosen)
        masked = jnp.where(hit, -jnp.inf, masked)
    w = chosen * scores
    o_ref[...] = w / jnp.sum(w, axis=0, keepdims=True) * ROUTED_SCALE


def _router_call(x1, p, name):
    n = x1.shape[0]
    tm = _tile(n, (512, 256, 128))
    return pl.pallas_call(
        _router_body,
        grid=(n // tm,),
        in_specs=[pl.BlockSpec((tm, D_MODEL), lambda i: (i, 0)), _const_spec((N_EXPERTS, D_MODEL)),
                  _const_spec((N_EXPERTS, 1))],
        out_specs=pl.BlockSpec((N_EXPERTS, tm), lambda i: (0, i)),
        out_shape=jax.ShapeDtypeStruct((N_EXPERTS, n), F32),
        compiler_params=_params(("parallel",), 32),
        name=name,
    )(x1, p["router_w_t"], p["router_bias"])


def _moe_body(xb_ref, x_ref, g_ref, wg_ref, wu_ref, wd_ref, lg_ref, lb_ref, o_ref, ob_ref, acc):
    e = pl.program_id(1)

    @pl.when(e == 0)
    def _():
        acc[...] = jnp.zeros_like(acc)

    xb = xb_ref[...]
    hid = _silu(_nn(xb, wg_ref[0])) * _nn(xb, wu_ref[0])
    g = g_ref[...]
    g_hi = g.astype(BF16)
    g_lo = (g - g_hi.astype(F32)).astype(BF16)
    pick = (lax.broadcasted_iota(jnp.int32, (GATE_COLS, EXPERT_HIDDEN), 0) == e).astype(BF16)
    gb = _nn(g_hi, pick) + _nn(g_lo, pick)
    acc[...] += _nn((hid * gb).astype(BF16), wd_ref[0])

    @pl.when(e == pl.num_programs(1) - 1)
    def _():
        x2 = _layer_norm(DN_ALPHA * x_ref[...] + acc[...], lg_ref[...], lb_ref[...])
        o_ref[...] = x2
        ob_ref[...] = x2.astype(BF16)


def _moe_call(x1b, x1, gates, p, name):
    n = x1.shape[0]
    tm = _tile(n, (1024, 512, 256, 128))
    ne = p["moe_wg"].shape[0]
    rowblk = lambda w: pl.BlockSpec((tm, w), lambda i, e: (i, 0))
    return pl.pallas_call(
        _moe_body,
        grid=(n // tm, ne),
        in_specs=[rowblk(D_MODEL), rowblk(D_MODEL), rowblk(GATE_COLS),
                  pl.BlockSpec((1, D_MODEL, EXPERT_HIDDEN), lambda i, e: (e, 0, 0)),
                  pl.BlockSpec((1, D_MODEL, EXPERT_HIDDEN), lambda i, e: (e, 0, 0)),
                  pl.BlockSpec((1, EXPERT_HIDDEN, D_MODEL), lambda i, e: (e, 0, 0)),
                  _const_spec((1, D_MODEL)), _const_spec((1, D_MODEL))],
        out_specs=[rowblk(D_MODEL), rowblk(D_MODEL)],
        out_shape=[jax.ShapeDtypeStruct((n, D_MODEL), F32), jax.ShapeDtypeStruct((n, D_MODEL), BF16)],
        scratch_shapes=[pltpu.VMEM((tm, D_MODEL), F32)],
        compiler_params=_params(("parallel", "arbitrary"), 48),
        name=name,
    )(x1b, x1, gates, p["moe_wg"], p["moe_wu"], p["moe_wd"], p["ln2_g"], p["ln2_b"])


def _block_diag(w):
    k, d, _ = w.shape
    eye = jnp.eye(k, dtype=w.dtype)
    return (eye[:, None, :, None] * w[:, :, None, :]).reshape(k * d, k * d)


def _prep_layer(w, l):
    w_in = w["w_in"][l]
    o = 0
    seg = {}
    for nm, width in (("z", D_MODEL), ("xbc", SSD_CONV_DIM), ("dt", SSD_HEADS), ("q", D_MODEL), ("k", D_MODEL),
                      ("v", D_MODEL), ("xl", D_MODEL), ("yl", D_MODEL), ("g", N_BRANCH * D_MODEL)):
        seg[nm] = w_in[:, o:o + width]
        o += width
    row = lambda a: a.reshape(1, -1)
    lane_pad = lambda a: jnp.pad(a.reshape(1, -1), ((0, 0), (0, LANES - a.shape[-1])))
    head_of_channel = jnp.arange(D_MODEL) // SSD_HEAD_DIM
    rel = jnp.arange(CHUNK)[:, None] + ATT_LEFT - jnp.arange(ATT_BAND)[None, :]
    rel_idx = jnp.clip(rel, -REL_CLIP, REL_CLIP) + REL_CLIP
    return dict(
        w_main=jnp.concatenate([seg[k] for k in ("g", "z", "xbc", "q", "k", "v", "xl", "yl")], axis=1).astype(BF16),
        w_dt=jnp.pad(seg["dt"], ((0, 0), (0, LANES - SSD_HEADS))).astype(BF16),
        ssd_conv_w=w["ssd_conv_w"][l], ssd_conv_b=row(w["ssd_conv_b"][l]),
        ssd_dt_bias=lane_pad(w["ssd_dt_bias"][l]), ssd_a_log=lane_pad(w["ssd_a_log"][l]),
        ssd_d_ch=row(w["ssd_d"][l][head_of_channel]), ssd_norm_w=row(w["ssd_norm_w"][l]),
        ssd_expand=(jnp.arange(LANES)[:, None] == head_of_channel[None, :]).astype(F32),
        att_bias=w["att_rel_bias"][l][:, rel_idx],
        lru_conv_w=w["lru_conv_w"][l], lru_conv_b=row(w["lru_conv_b"][l]),
        lru_wa_d=_block_diag(w["lru_wa"][l]).astype(BF16), lru_ba=row(w["lru_ba"][l]),
        lru_wx_d=_block_diag(w["lru_wx"][l]).astype(BF16), lru_bx=row(w["lru_bx"][l]),
        lru_lambda=row(w["lru_lambda"][l]),
        gate_bias=row(w["gate_bias"][l]),
        w_ssd_proj=w["w_ssd_proj"][l].astype(BF16), w_att_proj=w["w_att_proj"][l].astype(BF16),
        w_lru_proj=w["w_lru_proj"][l].astype(BF16), w_out=w["w_out"][l].astype(BF16),
        ln1_g=row(w["ln1_g"][l]), ln1_b=row(w["ln1_b"][l]),
        router_w_t=w["router_w"][l].T, router_bias=w["router_bias"][l].reshape(N_EXPERTS, 1),
        moe_wg=jnp.concatenate([w["exp_w_gate"][l], w["sh_w_gate"][l][None]], axis=0).astype(BF16),
        moe_wu=jnp.concatenate([w["exp_w_up"][l], w["sh_w_up"][l][None]], axis=0).astype(BF16),
        moe_wd=jnp.concatenate([w["exp_w_down"][l], w["sh_w_down"][l][None]], axis=0).astype(BF16),
        ln2_g=row(w["ln2_g"][l]), ln2_b=row(w["ln2_b"][l]),
    )


def _heads_state(h_t, nb):
    return jnp.swapaxes(h_t, 1, 2).reshape(nb, SSD_HEADS, SSD_HEAD_DIM, SSD_STATE)


def _layer(x, xb, st, p, geom, l):
    bp, tp, bs, ts = geom
    n_p = bp * tp
    ncp = tp // CHUNK
    proj = _matmul(xb, p["w_main"], D_MODEL, f"in_proj_{l}")
    proj_dt = _matmul(xb, p["w_dt"], LANES, f"dt_proj_{l}")

    zeros = lambda *s: jnp.zeros(s, F32)
    ssd_p, pconv, ph = _ssd_call(proj, proj_dt, zeros(bp, CONV_WIDTH - 1, SSD_CONV_DIM),
                                 zeros(bp, SSD_STATE, D_MODEL), p, bp, ncp, 0, f"ssd_p_{l}")
    h0_t = jnp.swapaxes(st["state_ssd"].reshape(bs, D_MODEL, SSD_STATE), 1, 2)
    ssd_s, sconv, sh = _ssd_call(proj, proj_dt, st["cache_ssd_conv"], h0_t, p, bs, ts // CHUNK, n_p // CHUNK,
                                 f"ssd_s_{l}")

    att_rows = _tile(tp, (ATT_LEFT,))
    att_p = _att_call(proj, None, None, p["att_bias"], bp, tp // att_rows, att_rows, 0, True, f"att_p_{l}")
    kc = st["cache_att_k"].reshape(bs * ATT_LEFT, D_MODEL)
    vc = st["cache_att_v"].reshape(bs * ATT_LEFT, D_MODEL)
    att_s = _att_call(proj, kc, vc, p["att_bias"], bs, 1, ts, n_p // ts, False, f"att_s_{l}")

    lru_rows = _tile(tp, (256, 128, 64))
    lru_p, plc, plh = _lru_call(proj, zeros(bp, CONV_WIDTH - 1, D_MODEL), zeros(bp, 1, D_MODEL), p, bp,
                                tp // lru_rows, lru_rows, 0, f"lru_p_{l}")
    lru_s, slc, slh = _lru_call(proj, st["cache_lru_conv"], st["state_lru"].reshape(bs, 1, D_MODEL), p, bs,
                                1, ts, n_p // ts, f"lru_s_{l}")

    cat = lambda a, b: jnp.concatenate([a, b], axis=0)
    x1, x1b = _merge_call(cat(ssd_p, ssd_s), cat(att_p, att_s), cat(lru_p, lru_s), proj, x, p, f"merge_{l}")
    gates_t = _router_call(x1, p, f"router_{l}")
    n = x.shape[0]
    gates = jnp.concatenate([gates_t.T, jnp.ones((n, 1), F32), jnp.zeros((n, GATE_COLS - N_EXPERTS - 1), F32)],
                            axis=1)
    x2, x2b = _moe_call(x1b, x1, gates, p, f"moe_{l}")

    keep = min(ATT_LEFT, tp)
    kv = lambda col, rows0, nb, t, kp: proj[rows0:rows0 + nb * t, col * D_MODEL:(col + 1) * D_MODEL].reshape(
        nb, t, ATT_HEADS, ATT_HEAD_DIM)[:, t - kp:]
    states = dict(
        p_ssd_conv=pconv, s_ssd_conv=sconv, p_ssd_state=_heads_state(ph, bp), s_ssd_state=_heads_state(sh, bs),
        p_att_k=kv(COL_K, 0, bp, tp, keep), s_att_k=kv(COL_K, n_p, bs, ts, ts),
        p_att_v=kv(COL_V, 0, bp, tp, keep), s_att_v=kv(COL_V, n_p, bs, ts, ts),
        p_lru_conv=plc, s_lru_conv=slc, p_lru_state=plh.reshape(bp, D_MODEL), s_lru_state=slh.reshape(bs, D_MODEL))
    return x2, x2b, states


def kernel(x_prompt, x_sample, cache_ssd_conv, state_ssd, cache_att_k, cache_att_v, cache_lru_conv, state_lru, w_in, ssd_conv_w, ssd_conv_b, ssd_dt_bias, ssd_a_log, ssd_d, ssd_norm_w, att_rel_bias, lru_conv_w, lru_conv_b, lru_wa, lru_ba, lru_wx, lru_bx, lru_lambda, gate_bias, w_ssd_proj, w_att_proj, w_lru_proj, w_out, ln1_g, ln1_b, router_w, router_bias, exp_w_gate, exp_w_up, exp_w_down, sh_w_gate, sh_w_up, sh_w_down, ln2_g, ln2_b):
    w = dict(w_in=w_in, ssd_conv_w=ssd_conv_w, ssd_conv_b=ssd_conv_b, ssd_dt_bias=ssd_dt_bias, ssd_a_log=ssd_a_log,
             ssd_d=ssd_d, ssd_norm_w=ssd_norm_w, att_rel_bias=att_rel_bias, lru_conv_w=lru_conv_w,
             lru_conv_b=lru_conv_b, lru_wa=lru_wa, lru_ba=lru_ba, lru_wx=lru_wx, lru_bx=lru_bx,
             lru_lambda=lru_lambda, gate_bias=gate_bias, w_ssd_proj=w_ssd_proj, w_att_proj=w_att_proj,
             w_lru_proj=w_lru_proj, w_out=w_out, ln1_g=ln1_g, ln1_b=ln1_b, router_w=router_w,
             router_bias=router_bias, exp_w_gate=exp_w_gate, exp_w_up=exp_w_up, exp_w_down=exp_w_down,
             sh_w_gate=sh_w_gate, sh_w_up=sh_w_up, sh_w_down=sh_w_down, ln2_g=ln2_g, ln2_b=ln2_b)
    bp, tp, _ = x_prompt.shape
    bs, ts, _ = x_sample.shape
    n_p = bp * tp
    x = jnp.concatenate([x_prompt.reshape(n_p, D_MODEL), x_sample.reshape(bs * ts, D_MODEL)], axis=0)
    xb = x.astype(BF16)
    per_layer = []
    for l in range(DEPTH):
        st = dict(cache_ssd_conv=cache_ssd_conv[l], state_ssd=state_ssd[l], cache_att_k=cache_att_k[l],
                  cache_att_v=cache_att_v[l], cache_lru_conv=cache_lru_conv[l], state_lru=state_lru[l])
        x, xb, states = _layer(x, xb, st, _prep_layer(w, l), (bp, tp, bs, ts), l)
        per_layer.append(states)
    stack = lambda k: jnp.stack([s[k] for s in per_layer], axis=0)
    return (x[:n_p].reshape(bp, tp, D_MODEL), x[n_p:].reshape(bs, ts, D_MODEL),
            stack("p_ssd_conv"), stack("s_ssd_conv"), stack("p_ssd_state"), stack("s_ssd_state"),
            stack("p_att_k"), stack("s_att_k"), stack("p_att_v"), stack("s_att_v"),
            stack("p_lru_conv"), stack("s_lru_conv"), stack("p_lru_state"), stack("s_lru_state"))
```

```python
import functools
import math

import jax
import jax.numpy as jnp
from jax import lax
from jax.experimental import pallas as pl
from jax.experimental.pallas import tpu as pltpu

F32 = jnp.float32
BF16 = jnp.bfloat16
HIGHEST = lax.Precision.HIGHEST

D_MODEL = 1024
DEPTH = 2
CHUNK = 64
CONV_WIDTH = 4
SSD_HEADS = 16
SSD_HEAD_DIM = 64
SSD_GROUPS = 4
SSD_STATE = 128
SSD_CONV_DIM = D_MODEL + 2 * SSD_GROUPS * SSD_STATE
ATT_HEADS = 16
ATT_HEAD_DIM = 64
ATT_LEFT = 8 * CHUNK
ATT_BAND = ATT_LEFT + CHUNK
REL_CLIP = 128
ATT_SCALE = ATT_HEAD_DIM ** -0.5
LRU_BLOCKS = 16
LRU_C = 8.0
N_BRANCH = 3
N_EXPERTS = 64
N_GROUPS = 8
GROUP_SIZE = N_EXPERTS // N_GROUPS
TOPK_GROUPS = 4
TOP_K = 8
EXPERT_HIDDEN = D_MODEL // 4
ROUTED_SCALE = 2.5
DN_ALPHA = (2.0 * DEPTH) ** 0.25
LN_EPS = 1e-5
RMS_EPS = 1e-5

LANES = 128
CONV_PAD = 8

COL_G, COL_Z, COL_XBC, COL_XL, COL_YL = 0, 3, 2, 6, 7
COL_Q, COL_K, COL_V = 0, 1, 2
ATT_SUB = 256
NEG_BIG = -1e30


def _params(semantics, vmem_mb):
    return pltpu.CompilerParams(dimension_semantics=semantics, vmem_limit_bytes=vmem_mb * 2 ** 20)


def _tile(n, candidates):
    for c in candidates:
        if n % c == 0:
            return c
    raise ValueError(f"no tile for {n}")


def _nn(a, b, precision=None):
    return jnp.dot(a, b, preferred_element_type=F32, precision=precision)


def _nt(a, b, precision=None):
    return lax.dot_general(a, b, (((1,), (1,)), ((), ())), preferred_element_type=F32, precision=precision)


def _tn(a, b):
    return lax.dot_general(a, b, (((0,), (0,)), ((), ())), preferred_element_type=F32)


def _sigmoid(x):
    return 1.0 / (1.0 + jnp.exp(-x))


def _silu(x):
    return x * _sigmoid(x)


def _softplus(x):
    return jnp.maximum(x, 0.0) + jnp.log1p(jnp.exp(-jnp.abs(x)))


def _layer_norm(y, g, b):
    mu = jnp.mean(y, axis=-1, keepdims=True)
    d = y - mu
    var = jnp.mean(d * d, axis=-1, keepdims=True)
    return d * lax.rsqrt(var + LN_EPS) * g + b


def _const_spec(shape):
    nd = len(shape)
    return pl.BlockSpec(shape, lambda *_: (0,) * nd)


def _shared_slab(ybuf):
    if ybuf is None:
        return [], [], {}
    return [ybuf], [pl.BlockSpec(memory_space=pl.ANY)], None


def _mm_body(a_ref, b_ref, o_ref):
    o_ref[...] = _nn(a_ref[...], b_ref[...]).astype(o_ref.dtype)


def _matmul(a, b, tn, name, out_dtype=F32):
    m, k = a.shape
    n = b.shape[1]
    tm = _tile(m, (1024, 512, 256, 128))
    return pl.pallas_call(
        _mm_body,
        grid=(m // tm, n // tn),
        in_specs=[pl.BlockSpec((tm, k), lambda i, j: (i, 0)), pl.BlockSpec((k, tn), lambda i, j: (0, j))],
        out_specs=pl.BlockSpec((tm, tn), lambda i, j: (i, j)),
        out_shape=jax.ShapeDtypeStruct((m, n), out_dtype),
        compiler_params=_params(("parallel", "arbitrary"), 40),
        name=name,
    )(a, b)


def _conv_block(xp_scr, x_ref, cw_ref, cb_ref, rows):
    lo = CONV_PAD - (CONV_WIDTH - 1)
    xp_scr[CONV_PAD:CONV_PAD + rows, :] = x_ref[...]
    y = cb_ref[...] + xp_scr[lo:lo + rows, :] * cw_ref[0:1, :]
    for k in range(1, CONV_WIDTH):
        y = y + xp_scr[lo + k:lo + k + rows, :] * cw_ref[k:k + 1, :]
    tail = xp_scr[lo + rows:CONV_PAD + rows, :]
    xp_scr[lo:CONV_PAD, :] = tail
    return y, tail


def _ssd_body(z_ref, xbc_ref, dt_ref, buf0_ref, h0_ref, cw_ref, cb_ref, dtb_ref, alog_ref, dch_ref, nw_ref,
              expand_ref, *rest, nc):
    y_ref, bufo_ref, ho_ref, xp_scr, h_scr = rest[-5:]
    c = pl.program_id(1)
    lo = CONV_PAD - (CONV_WIDTH - 1)

    @pl.when(c == 0)
    def _():
        xp_scr[lo:CONV_PAD, :] = buf0_ref[0]
        h_scr[...] = h0_ref[0]

    conv, tail = _conv_block(xp_scr, xbc_ref, cw_ref, cb_ref, CHUNK)
    bufo_ref[0] = tail
    xbc = _silu(conv)
    xs = xbc[:, :D_MODEL]
    bm = xbc[:, D_MODEL:D_MODEL + SSD_GROUPS * SSD_STATE].astype(BF16)
    cm = xbc[:, D_MODEL + SSD_GROUPS * SSD_STATE:].astype(BF16)

    dt = _softplus(dt_ref[...] + dtb_ref[...])
    da = dt * (-jnp.exp(alog_ref[...]))
    row = lax.broadcasted_iota(jnp.int32, (CHUNK, CHUNK), 0)
    col = lax.broadcasted_iota(jnp.int32, (CHUNK, CHUNK), 1)
    causal = row >= col
    cs = _nn(causal.astype(F32), da, HIGHEST)
    cs_last = cs[CHUNK - 1:CHUNK, :]
    to_end = jnp.exp(cs_last - cs)
    ecs = jnp.exp(cs)
    wide = _nn(jnp.concatenate([dt, to_end, ecs], axis=0), expand_ref[...], HIGHEST)
    dt_ch = wide[0:CHUNK]
    to_end_ch = wide[CHUNK:2 * CHUNK]
    ecs_ch = wide[2 * CHUNK:3 * CHUNK]
    dec_ch = ecs_ch[CHUNK - 1:CHUNK, :]
    eye = (lax.broadcasted_iota(jnp.int32, (LANES, LANES), 0)
           == lax.broadcasted_iota(jnp.int32, (LANES, LANES), 1)).astype(F32)
    cs_t = _nt(eye, cs, HIGHEST)

    xdt = xs * dt_ch
    xdt_b = xdt.astype(BF16)
    xw_b = (xdt * to_end_ch).astype(BF16)
    gw = SSD_HEADS // SSD_GROUPS * SSD_HEAD_DIM
    y_parts = []
    for g in range(SSD_GROUPS):
        bg = bm[:, g * SSD_STATE:(g + 1) * SSD_STATE]
        cg = cm[:, g * SSD_STATE:(g + 1) * SSD_STATE]
        gsl = slice(g * gw, (g + 1) * gw)
        cb = _nt(cg, bg)
        h_prev = h_scr[:, gsl]
        y_off = _nn(cg, h_prev.astype(BF16)) * ecs_ch[:, gsl]
        st = _tn(bg, xw_b[:, gsl])
        h_scr[:, gsl] = h_prev * dec_ch[:, gsl] + st
        diag = []
        for r in range(SSD_HEADS // SSD_GROUPS):
            h = g * (SSD_HEADS // SSD_GROUPS) + r
            seg = cs[:, h:h + 1] - cs_t[h:h + 1, :CHUNK]
            decay = jnp.where(causal, jnp.exp(jnp.where(causal, seg, 0.0)), 0.0)
            sc = (cb * decay).astype(BF16)
            diag.append(_nn(sc, xdt_b[:, h * SSD_HEAD_DIM:(h + 1) * SSD_HEAD_DIM]))
        y_parts.append(jnp.concatenate(diag, axis=-1) + y_off)
    y = jnp.concatenate(y_parts, axis=-1) + xs * dch_ref[...]
    v = y * _silu(z_ref[...])
    outs = []
    for g in range(SSD_GROUPS):
        vg = v[:, g * gw:(g + 1) * gw]
        ms = jnp.mean(vg * vg, axis=-1, keepdims=True)
        outs.append(vg * lax.rsqrt(ms + RMS_EPS))
    y_ref[...] = (jnp.concatenate(outs, axis=-1) * nw_ref[...]).astype(y_ref.dtype)

    @pl.when(c == nc - 1)
    def _():
        ho_ref[0] = h_scr[...]


def _ssd_call(proj, proj_dt, buf0, h0_t, p, nb, nc, blk0, ybuf, name):
    n = proj.shape[0]
    row = lambda b, c: blk0 + b * nc + c
    extra, extra_specs, _ = _shared_slab(ybuf)
    in_specs = [
        pl.BlockSpec((CHUNK, D_MODEL), lambda b, c: (row(b, c), COL_Z)),
        pl.BlockSpec((CHUNK, SSD_CONV_DIM), lambda b, c: (row(b, c), COL_XBC)),
        pl.BlockSpec((CHUNK, LANES), lambda b, c: (row(b, c), 0)),
        pl.BlockSpec((1, CONV_WIDTH - 1, SSD_CONV_DIM), lambda b, c: (b, 0, 0)),
        pl.BlockSpec((1, SSD_STATE, D_MODEL), lambda b, c: (b, 0, 0)),
        _const_spec((CONV_WIDTH, SSD_CONV_DIM)),
        _const_spec((1, SSD_CONV_DIM)),
        _const_spec((1, LANES)),
        _const_spec((1, LANES)),
        _const_spec((1, D_MODEL)),
        _const_spec((1, D_MODEL)),
        _const_spec((LANES, D_MODEL)),
    ] + extra_specs
    out_specs = [
        pl.BlockSpec((CHUNK, D_MODEL), lambda b, c: (row(b, c), 0)),
        pl.BlockSpec((1, CONV_WIDTH - 1, SSD_CONV_DIM), lambda b, c: (b, 0, 0)),
        pl.BlockSpec((1, SSD_STATE, D_MODEL), lambda b, c: (b, 0, 0)),
    ]
    out_shape = [
        jax.ShapeDtypeStruct((n, D_MODEL), BF16),
        jax.ShapeDtypeStruct((nb, CONV_WIDTH - 1, SSD_CONV_DIM), F32),
        jax.ShapeDtypeStruct((nb, SSD_STATE, D_MODEL), F32),
    ]
    return pl.pallas_call(
        functools.partial(_ssd_body, nc=nc),
        grid=(nb, nc),
        in_specs=in_specs,
        out_specs=out_specs,
        out_shape=out_shape,
        scratch_shapes=[pltpu.VMEM((CONV_PAD + CHUNK, SSD_CONV_DIM), F32), pltpu.VMEM((SSD_STATE, D_MODEL), F32)],
        input_output_aliases={len(in_specs) - 1: 0} if extra else {},
        compiler_params=_params(("parallel", "arbitrary"), 40),
        name=name,
    )(proj, proj, proj_dt, buf0, h0_t, p["ssd_conv_w"], p["ssd_conv_b"], p["ssd_dt_bias"], p["ssd_a_log"],
      p["ssd_d_ch"], p["ssd_norm_w"], p["ssd_expand"], *extra)


def _att_body(q_ref, kp_ref, kc_ref, vp_ref, vc_ref, bias_ref, *rest, rows, sub, mask_first):
    o_ref, k_scr, v_scr = rest[-3:]
    t = pl.program_id(1)
    k_scr[0:ATT_LEFT, :] = kp_ref[...]
    k_scr[ATT_LEFT:ATT_LEFT + rows, :] = kc_ref[...]
    v_scr[0:ATT_LEFT, :] = vp_ref[...]
    v_scr[ATT_LEFT:ATT_LEFT + rows, :] = vc_ref[...]
    win = sub + ATT_LEFT
    first = lax.broadcasted_iota(jnp.int32, (1, LANES), 1) < ATT_HEAD_DIM
    for s in range(rows // sub):
        r0 = s * sub
        if mask_first:
            pos = lax.broadcasted_iota(jnp.int32, (1, win), 1) + r0
            dead = jnp.logical_and(pos < ATT_LEFT, t == 0)
        for hp in range(ATT_HEADS // 2):
            ls = slice(hp * LANES, (hp + 1) * LANES)
            q2 = q_ref[r0:r0 + sub, ls]
            k2 = k_scr[r0:r0 + win, ls]
            v2 = v_scr[r0:r0 + win, ls]
            zero = jnp.zeros_like(q2)
            outs = []
            for h, qh in ((2 * hp, jnp.where(first, q2, zero)), (2 * hp + 1, jnp.where(first, zero, q2))):
                sc = _nt(qh, k2) + bias_ref[h]
                if mask_first:
                    sc = jnp.where(dead, NEG_BIG, sc)
                m = jnp.max(sc, axis=-1, keepdims=True)
                e = jnp.exp(sc - m)
                den = jnp.sum(e, axis=-1, keepdims=True)
                outs.append(_nn(e.astype(BF16), v2) / den)
            o_ref[r0:r0 + sub, ls] = jnp.where(first, outs[0], outs[1]).astype(o_ref.dtype)


def _att_call(qkv, k_prev, v_prev, bias, nb, nt, rows, blk0, prompt, ybuf, name):
    n = qkv.shape[0]
    sub = min(rows, ATT_SUB)
    cur = lambda col: pl.BlockSpec((rows, D_MODEL), lambda b, t: (blk0 + b * nt + t, col))
    if prompt:
        prev = lambda col: pl.BlockSpec((ATT_LEFT, D_MODEL), lambda b, t: (b * nt + jnp.maximum(t - 1, 0), col))
        kp_spec, vp_spec = prev(COL_K), prev(COL_V)
        k_prev = v_prev = qkv
    else:
        kp_spec = vp_spec = pl.BlockSpec((ATT_LEFT, D_MODEL), lambda b, t: (b, 0))
    extra, extra_specs, _ = _shared_slab(ybuf)
    in_specs = [cur(COL_Q), kp_spec, cur(COL_K), vp_spec, cur(COL_V),
                _const_spec((ATT_HEADS, sub, sub + ATT_LEFT))] + extra_specs
    return pl.pallas_call(
        functools.partial(_att_body, rows=rows, sub=sub, mask_first=prompt),
        grid=(nb, nt),
        in_specs=in_specs,
        out_specs=pl.BlockSpec((rows, D_MODEL), lambda b, t: (blk0 + b * nt + t, 0)),
        out_shape=jax.ShapeDtypeStruct((n, D_MODEL), BF16),
        scratch_shapes=[pltpu.VMEM((ATT_LEFT + rows, D_MODEL), BF16), pltpu.VMEM((ATT_LEFT + rows, D_MODEL), BF16)],
        input_output_aliases={len(in_specs) - 1: 0} if extra else {},
        compiler_params=_params(("parallel", "arbitrary"), 56),
        name=name,
    )(qkv, k_prev, qkv, v_prev, qkv, bias, *extra)


def _band_bias(table, sub):
    r = jnp.arange(sub)[:, None]
    c = jnp.arange(sub + ATT_LEFT)[None, :]
    j = c - (r // CHUNK) * CHUNK
    valid = jnp.logical_and(j >= 0, j < ATT_BAND)
    rel = jnp.clip((r % CHUNK) + ATT_LEFT - j, -REL_CLIP, REL_CLIP) + REL_CLIP
    onehot = (rel[..., None] == jnp.arange(2 * REL_CLIP + 1)).astype(F32)
    bias = jnp.einsum("rck,hk->hrc", onehot, table, precision=HIGHEST)
    return jnp.where(valid[None], bias, NEG_BIG)


def _scan64(a, u):
    row = lax.broadcasted_iota(jnp.int32, (CHUNK, 1), 0)
    d = 1
    while d < CHUNK:
        a_sh = pltpu.roll(a, d, 0)
        u_sh = pltpu.roll(u, d, 0)
        m = row >= d
        u = jnp.where(m, a * u_sh + u, u)
        a = jnp.where(m, a * a_sh, a)
        d *= 2
    return a, u


def _gelu_tanh(x):
    return 0.5 * x * (1.0 + jnp.tanh(math.sqrt(2.0 / math.pi) * (x + 0.044715 * (x * x * x))))


def _lru_body(xl_ref, yl_ref, buf0_ref, h0_ref, cw_ref, cb_ref, wa_ref, ba_ref, wx_ref, bx_ref, lam_ref,
              *rest, rows):
    o_ref, bufo_ref, ho_ref, xp_scr, carry_scr = rest[-5:]
    c = pl.program_id(1)
    lo = CONV_PAD - (CONV_WIDTH - 1)

    @pl.when(c == 0)
    def _():
        xp_scr[lo:CONV_PAD, :] = buf0_ref[0]
        carry_scr[...] = h0_ref[0]

    xc, tail = _conv_block(xp_scr, xl_ref, cw_ref, cb_ref, rows)
    bufo_ref[0] = tail
    xcb = xc.astype(BF16)
    gate_r = _sigmoid(_nn(xcb, wa_ref[...]) + ba_ref[...])
    gate_i = _sigmoid(_nn(xcb, wx_ref[...]) + bx_ref[...])
    log_a = LRU_C * gate_r * (-_softplus(-lam_ref[...]))
    a = jnp.exp(log_a)
    u = jnp.sqrt(1.0 - jnp.exp(2.0 * log_a)) * (gate_i * xc)
    carry = carry_scr[...]
    for j in range(rows // CHUNK):
        sl = slice(j * CHUNK, (j + 1) * CHUNK)
        a_cum, u_cum = _scan64(a[sl], u[sl])
        h = a_cum * carry + u_cum
        carry = h[CHUNK - 1:CHUNK, :]
        o_ref[sl, :] = (h * _gelu_tanh(yl_ref[sl, :])).astype(o_ref.dtype)
    carry_scr[...] = carry
    ho_ref[0] = carry


def _lru_call(proj, buf0, h0, p, nb, nc, rows, blk0, ybuf, name):
    n = proj.shape[0]
    row = lambda b, c: blk0 + b * nc + c
    extra, extra_specs, _ = _shared_slab(ybuf)
    in_specs = [
        pl.BlockSpec((rows, D_MODEL), lambda b, c: (row(b, c), COL_XL)),
        pl.BlockSpec((rows, D_MODEL), lambda b, c: (row(b, c), COL_YL)),
        pl.BlockSpec((1, CONV_WIDTH - 1, D_MODEL), lambda b, c: (b, 0, 0)),
        pl.BlockSpec((1, 1, D_MODEL), lambda b, c: (b, 0, 0)),
        _const_spec((CONV_WIDTH, D_MODEL)),
        _const_spec((1, D_MODEL)),
        _const_spec((D_MODEL, D_MODEL)),
        _const_spec((1, D_MODEL)),
        _const_spec((D_MODEL, D_MODEL)),
        _const_spec((1, D_MODEL)),
        _const_spec((1, D_MODEL)),
    ] + extra_specs
    out_specs = [
        pl.BlockSpec((rows, D_MODEL), lambda b, c: (row(b, c), 0)),
        pl.BlockSpec((1, CONV_WIDTH - 1, D_MODEL), lambda b, c: (b, 0, 0)),
        pl.BlockSpec((1, 1, D_MODEL), lambda b, c: (b, 0, 0)),
    ]
    out_shape = [
        jax.ShapeDtypeStruct((n, D_MODEL), BF16),
        jax.ShapeDtypeStruct((nb, CONV_WIDTH - 1, D_MODEL), F32),
        jax.ShapeDtypeStruct((nb, 1, D_MODEL), F32),
    ]
    return pl.pallas_call(
        functools.partial(_lru_body, rows=rows),
        grid=(nb, nc),
        in_specs=in_specs,
        out_specs=out_specs,
        out_shape=out_shape,
        scratch_shapes=[pltpu.VMEM((CONV_PAD + rows, D_MODEL), F32), pltpu.VMEM((1, D_MODEL), F32)],
        input_output_aliases={len(in_specs) - 1: 0} if extra else {},
        compiler_params=_params(("parallel", "arbitrary"), 48),
        name=name,
    )(proj, proj, buf0, h0, p["lru_conv_w"], p["lru_conv_b"], p["lru_wa_d"], p["lru_ba"], p["lru_wx_d"],
      p["lru_bx"], p["lru_lambda"], *extra)


def _merge_body(s_ref, a_ref, l_ref, g_ref, x_ref, ws_ref, wa_ref, wl_ref, wo_ref, gb_ref, lg_ref, lb_ref,
                o_ref, ob_ref):
    g = _sigmoid(g_ref[...] + gb_ref[...])
    m = (g[:, 0:D_MODEL] * _nn(s_ref[...], ws_ref[...])
         + g[:, D_MODEL:2 * D_MODEL] * _nn(a_ref[...], wa_ref[...])
         + g[:, 2 * D_MODEL:3 * D_MODEL] * _nn(l_ref[...], wl_ref[...]))
    y = DN_ALPHA * x_ref[...] + _nn(m.astype(BF16), wo_ref[...])
    x1 = _layer_norm(y, lg_ref[...], lb_ref[...])
    o_ref[...] = x1
    ob_ref[...] = x1.astype(BF16)


def _merge_call(ssd_y, att_y, lru_y, proj, x, p, name):
    n = x.shape[0]
    tm = _tile(n, (256, 128))
    rowblk = lambda w: pl.BlockSpec((tm, w), lambda i: (i, 0))
    wspec = _const_spec((D_MODEL, D_MODEL))
    return pl.pallas_call(
        _merge_body,
        grid=(n // tm,),
        in_specs=[rowblk(D_MODEL), rowblk(D_MODEL), rowblk(D_MODEL),
                  pl.BlockSpec((tm, N_BRANCH * D_MODEL), lambda i: (i, COL_G)), rowblk(D_MODEL),
                  wspec, wspec, wspec, wspec,
                  _const_spec((1, N_BRANCH * D_MODEL)), _const_spec((1, D_MODEL)), _const_spec((1, D_MODEL))],
        out_specs=[rowblk(D_MODEL), rowblk(D_MODEL)],
        out_shape=[jax.ShapeDtypeStruct((n, D_MODEL), F32), jax.ShapeDtypeStruct((n, D_MODEL), BF16)],
        compiler_params=_params(("parallel",), 48),
        name=name,
    )(ssd_y, att_y, lru_y, proj, x, p["w_ssd_proj"], p["w_att_proj"], p["w_lru_proj"], p["w_out"],
      p["gate_bias"], p["ln1_g"], p["ln1_b"])


def _first_max(x, idx, big):
    m = jnp.max(x, axis=0, keepdims=True)
    first = jnp.min(jnp.where(x == m, idx, big), axis=0, keepdims=True)
    return m, idx == first


def _router_body(x_ref, rwt_ref, rb_ref, o_ref):
    tm = x_ref.shape[0]
    logits = _nt(rwt_ref[...], x_ref[...], HIGHEST)
    scores = _sigmoid(logits)
    sel = scores + rb_ref[...]
    sub = lax.broadcasted_iota(jnp.int32, (GROUP_SIZE, tm), 0)
    gscore = jnp.zeros((N_GROUPS, tm), F32)
    for g in range(N_GROUPS):
        sg = sel[g * GROUP_SIZE:(g + 1) * GROUP_SIZE, :]
        m1, hit = _first_max(sg, sub, GROUP_SIZE)
        m2 = jnp.max(jnp.where(hit, -jnp.inf, sg), axis=0, keepdims=True)
        gscore = jnp.where(sub == g, m1 + m2, gscore)
    gsel = jnp.zeros((N_GROUPS, tm), F32)
    for _ in range(TOPK_GROUPS):
        _, hit = _first_max(gscore, sub, N_GROUPS)
        gsel = jnp.where(hit, 1.0, gsel)
        gscore = jnp.where(hit, -jnp.inf, gscore)
    masked = jnp.concatenate(
        [jnp.where(gsel[g:g + 1, :] > 0.0, sel[g * GROUP_SIZE:(g + 1) * GROUP_SIZE, :], -jnp.inf)
         for g in range(N_GROUPS)], axis=0)
    eidx = lax.broadcasted_iota(jnp.int32, (N_EXPERTS, tm), 0)
    chosen = jnp.zeros((N_EXPERTS, tm), F32)
    for _ in range(TOP_K):
        _, hit = _first_max(masked, eidx, N_EXPERTS)
        chosen = jnp.where(hit, 1.0, chosen)
        masked = jnp.where(hit, -jnp.inf, masked)
    w = chosen * scores
    o_ref[...] = w / jnp.sum(w, axis=0, keepdims=True) * ROUTED_SCALE


def _router_call(x1, p, name):
    n = x1.shape[0]
    tm = _tile(n, (512, 256, 128))
    return pl.pallas_call(
        _router_body,
        grid=(n // tm,),
        in_specs=[pl.BlockSpec((tm, D_MODEL), lambda i: (i, 0)), _const_spec((N_EXPERTS, D_MODEL)),
                  _const_spec((N_EXPERTS, 1))],
        out_specs=pl.BlockSpec((N_EXPERTS, tm), lambda i: (0, i)),
        out_shape=jax.ShapeDtypeStruct((N_EXPERTS, n), F32),
        compiler_params=_params(("parallel",), 32),
        name=name,
    )(x1, p["router_w_t"], p["router_bias"])


def _moe_body(xb_ref, x_ref, g_ref, wg_ref, wu_ref, wd_ref, sg_ref, su_ref, sd_ref, lg_ref, lb_ref,
              o_ref, ob_ref, acc):
    e = pl.program_id(1)

    @pl.when(e == 0)
    def _():
        acc[...] = jnp.zeros_like(acc)

    xb = xb_ref[...]
    hid = _silu(_nn(xb, wg_ref[0])) * _nn(xb, wu_ref[0])
    g = g_ref[...]
    g_hi = g.astype(BF16)
    g_lo = (g - g_hi.astype(F32)).astype(BF16)
    pick = (lax.broadcasted_iota(jnp.int32, (N_EXPERTS, EXPERT_HIDDEN), 0) == e).astype(BF16)
    gb = _nn(g_hi, pick) + _nn(g_lo, pick)
    acc[...] += _nn((hid * gb).astype(BF16), wd_ref[0])

    @pl.when(e == pl.num_programs(1) - 1)
    def _():
        shared = _nn((_silu(_nn(xb, sg_ref[...])) * _nn(xb, su_ref[...])).astype(BF16), sd_ref[...])
        x2 = _layer_norm(DN_ALPHA * x_ref[...] + (acc[...] + shared), lg_ref[...], lb_ref[...])
        o_ref[...] = x2
        ob_ref[...] = x2.astype(BF16)


def _moe_call(x1b, x1, gates, p, name):
    n = x1.shape[0]
    tm = _tile(n, (1024, 512, 256, 128))
    rowblk = lambda w: pl.BlockSpec((tm, w), lambda i, e: (i, 0))
    return pl.pallas_call(
        _moe_body,
        grid=(n // tm, N_EXPERTS),
        in_specs=[rowblk(D_MODEL), rowblk(D_MODEL), rowblk(N_EXPERTS),
                  pl.BlockSpec((1, D_MODEL, EXPERT_HIDDEN), lambda i, e: (e, 0, 0)),
                  pl.BlockSpec((1, D_MODEL, EXPERT_HIDDEN), lambda i, e: (e, 0, 0)),
                  pl.BlockSpec((1, EXPERT_HIDDEN, D_MODEL), lambda i, e: (e, 0, 0)),
                  _const_spec((D_MODEL, EXPERT_HIDDEN)), _const_spec((D_MODEL, EXPERT_HIDDEN)),
                  _const_spec((EXPERT_HIDDEN, D_MODEL)),
                  _const_spec((1, D_MODEL)), _const_spec((1, D_MODEL))],
        out_specs=[rowblk(D_MODEL), rowblk(D_MODEL)],
        out_shape=[jax.ShapeDtypeStruct((n, D_MODEL), F32), jax.ShapeDtypeStruct((n, D_MODEL), BF16)],
        scratch_shapes=[pltpu.VMEM((tm, D_MODEL), F32)],
        compiler_params=_params(("parallel", "arbitrary"), 48),
        name=name,
    )(x1b, x1, gates, p["exp_wg"], p["exp_wu"], p["exp_wd"], p["sh_wg"], p["sh_wu"], p["sh_wd"],
      p["ln2_g"], p["ln2_b"])


def _block_diag(w):
    k, d, _ = w.shape
    eye = jnp.eye(k, dtype=w.dtype)
    return (eye[:, None, :, None] * w[:, :, None, :]).reshape(k * d, k * d)


def _prep_layer(w, l):
    w_in = w["w_in"][l]
    o = 0
    seg = {}
    for nm, width in (("z", D_MODEL), ("xbc", SSD_CONV_DIM), ("dt", SSD_HEADS), ("q", D_MODEL), ("k", D_MODEL),
                      ("v", D_MODEL), ("xl", D_MODEL), ("yl", D_MODEL), ("g", N_BRANCH * D_MODEL)):
        seg[nm] = w_in[:, o:o + width]
        o += width
    row = lambda a: a.reshape(1, -1)
    lane_pad = lambda a: jnp.pad(a.reshape(1, -1), ((0, 0), (0, LANES - a.shape[-1])))
    head_of_channel = jnp.arange(D_MODEL) // SSD_HEAD_DIM
    return dict(
        w_main=jnp.concatenate([seg[k] for k in ("g", "z", "xbc", "xl", "yl")], axis=1).astype(BF16),
        w_qkv=jnp.concatenate([seg["q"] * ATT_SCALE, seg["k"], seg["v"]], axis=1).astype(BF16),
        w_kv=jnp.concatenate([seg["k"], seg["v"]], axis=1).astype(BF16),
        w_dt=jnp.pad(seg["dt"], ((0, 0), (0, LANES - SSD_HEADS))).astype(BF16),
        ssd_conv_w=w["ssd_conv_w"][l], ssd_conv_b=row(w["ssd_conv_b"][l]),
        ssd_dt_bias=lane_pad(w["ssd_dt_bias"][l]), ssd_a_log=lane_pad(w["ssd_a_log"][l]),
        ssd_d_ch=row(w["ssd_d"][l][head_of_channel]), ssd_norm_w=row(w["ssd_norm_w"][l]),
        ssd_expand=(jnp.arange(LANES)[:, None] == head_of_channel[None, :]).astype(F32),
        att_table=w["att_rel_bias"][l],
        lru_conv_w=w["lru_conv_w"][l], lru_conv_b=row(w["lru_conv_b"][l]),
        lru_wa_d=_block_diag(w["lru_wa"][l]).astype(BF16), lru_ba=row(w["lru_ba"][l]),
        lru_wx_d=_block_diag(w["lru_wx"][l]).astype(BF16), lru_bx=row(w["lru_bx"][l]),
        lru_lambda=row(w["lru_lambda"][l]),
        gate_bias=row(w["gate_bias"][l]),
        w_ssd_proj=w["w_ssd_proj"][l].astype(BF16), w_att_proj=w["w_att_proj"][l].astype(BF16),
        w_lru_proj=w["w_lru_proj"][l].astype(BF16), w_out=w["w_out"][l].astype(BF16),
        ln1_g=row(w["ln1_g"][l]), ln1_b=row(w["ln1_b"][l]),
        router_w_t=w["router_w"][l].T, router_bias=w["router_bias"][l].reshape(N_EXPERTS, 1),
        exp_wg=w["exp_w_gate"][l].astype(BF16), exp_wu=w["exp_w_up"][l].astype(BF16),
        exp_wd=w["exp_w_down"][l].astype(BF16),
        sh_wg=w["sh_w_gate"][l].astype(BF16), sh_wu=w["sh_w_up"][l].astype(BF16),
        sh_wd=w["sh_w_down"][l].astype(BF16),
        ln2_g=row(w["ln2_g"][l]), ln2_b=row(w["ln2_b"][l]),
    )


def _heads_state(h_t, nb):
    return jnp.swapaxes(h_t, 1, 2).reshape(nb, SSD_HEADS, SSD_HEAD_DIM, SSD_STATE)


def _layer(x, xb, st, p, geom, l):
    bp, tp, bs, ts = geom
    n_p = bp * tp
    ncp = tp // CHUNK
    keep = min(ATT_LEFT, tp)
    proj = _matmul(xb, p["w_main"], D_MODEL, f"in_proj_{l}")
    qkv = _matmul(xb, p["w_qkv"], D_MODEL, f"qkv_proj_{l}", BF16)
    proj_dt = _matmul(xb, p["w_dt"], LANES, f"dt_proj_{l}")
    xb_keep = jnp.concatenate([xb[:n_p].reshape(bp, tp, D_MODEL)[:, tp - keep:].reshape(bp * keep, D_MODEL),
                               xb[n_p:]], axis=0)
    kv_keep = _matmul(xb_keep, p["w_kv"], D_MODEL, f"kv_keep_{l}")

    zeros = lambda *s: jnp.zeros(s, F32)
    ssd_y, pconv, ph = _ssd_call(proj, proj_dt, zeros(bp, CONV_WIDTH - 1, SSD_CONV_DIM),
                                 zeros(bp, SSD_STATE, D_MODEL), p, bp, ncp, 0, None, f"ssd_p_{l}")
    h0_t = jnp.swapaxes(st["state_ssd"].reshape(bs, D_MODEL, SSD_STATE), 1, 2)
    ssd_y, sconv, sh = _ssd_call(proj, proj_dt, st["cache_ssd_conv"], h0_t, p, bs, ts // CHUNK, n_p // CHUNK,
                                 ssd_y, f"ssd_s_{l}")

    att_rows = _tile(tp, (ATT_LEFT,))
    bias_p = _band_bias(p["att_table"], min(att_rows, ATT_SUB))
    att_y = _att_call(qkv, None, None, bias_p, bp, tp // att_rows, att_rows, 0, True, None, f"att_p_{l}")
    kc = st["cache_att_k"].reshape(bs * ATT_LEFT, D_MODEL).astype(BF16)
    vc = st["cache_att_v"].reshape(bs * ATT_LEFT, D_MODEL).astype(BF16)
    bias_s = bias_p[:, :ts, :ts + ATT_LEFT]
    att_y = _att_call(qkv, kc, vc, bias_s, bs, 1, ts, n_p // ts, False, att_y, f"att_s_{l}")

    lru_rows = _tile(tp, (256, 128, 64))
    lru_y, plc, plh = _lru_call(proj, zeros(bp, CONV_WIDTH - 1, D_MODEL), zeros(bp, 1, D_MODEL), p, bp,
                                tp // lru_rows, lru_rows, 0, None, f"lru_p_{l}")
    lru_y, slc, slh = _lru_call(proj, st["cache_lru_conv"], st["state_lru"].reshape(bs, 1, D_MODEL), p, bs,
                                1, ts, n_p // ts, lru_y, f"lru_s_{l}")

    x1, x1b = _merge_call(ssd_y, att_y, lru_y, proj, x, p, f"merge_{l}")
    gates_t = _router_call(x1, p, f"router_{l}")
    x2, x2b = _moe_call(x1b, x1, gates_t.T, p, f"moe_{l}")

    def kv(col, rows0, nb, t):
        return kv_keep[rows0:rows0 + nb * t, col * D_MODEL:(col + 1) * D_MODEL].reshape(
            nb, t, ATT_HEADS, ATT_HEAD_DIM)

    states = dict(
        p_ssd_conv=pconv, s_ssd_conv=sconv, p_ssd_state=_heads_state(ph, bp), s_ssd_state=_heads_state(sh, bs),
        p_att_k=kv(0, 0, bp, keep), s_att_k=kv(0, bp * keep, bs, ts),
        p_att_v=kv(1, 0, bp, keep), s_att_v=kv(1, bp * keep, bs, ts),
        p_lru_conv=plc, s_lru_conv=slc, p_lru_state=plh.reshape(bp, D_MODEL), s_lru_state=slh.reshape(bs, D_MODEL))
    return x2, x2b, states


def kernel(x_prompt, x_sample, cache_ssd_conv, state_ssd, cache_att_k, cache_att_v, cache_lru_conv, state_lru, w_in, ssd_conv_w, ssd_conv_b, ssd_dt_bias, ssd_a_log, ssd_d, ssd_norm_w, att_rel_bias, lru_conv_w, lru_conv_b, lru_wa, lru_ba, lru_wx, lru_bx, lru_lambda, gate_bias, w_ssd_proj, w_att_proj, w_lru_proj, w_out, ln1_g, ln1_b, router_w, router_bias, exp_w_gate, exp_w_up, exp_w_down, sh_w_gate, sh_w_up, sh_w_down, ln2_g, ln2_b):
    w = dict(w_in=w_in, ssd_conv_w=ssd_conv_w, ssd_conv_b=ssd_conv_b, ssd_dt_bias=ssd_dt_bias, ssd_a_log=ssd_a_log,
             ssd_d=ssd_d, ssd_norm_w=ssd_norm_w, att_rel_bias=att_rel_bias, lru_conv_w=lru_conv_w,
             lru_conv_b=lru_conv_b, lru_wa=lru_wa, lru_ba=lru_ba, lru_wx=lru_wx, lru_bx=lru_bx,
             lru_lambda=lru_lambda, gate_bias=gate_bias, w_ssd_proj=w_ssd_proj, w_att_proj=w_att_proj,
             w_lru_proj=w_lru_proj, w_out=w_out, ln1_g=ln1_g, ln1_b=ln1_b, router_w=router_w,
             router_bias=router_bias, exp_w_gate=exp_w_gate, exp_w_up=exp_w_up, exp_w_down=exp_w_down,
             sh_w_gate=sh_w_gate, sh_w_up=sh_w_up, sh_w_down=sh_w_down, ln2_g=ln2_g, ln2_b=ln2_b)
    bp, tp, _ = x_prompt.shape
    bs, ts, _ = x_sample.shape
    n_p = bp * tp
    x = jnp.concatenate([x_prompt.reshape(n_p, D_MODEL), x_sample.reshape(bs * ts, D_MODEL)], axis=0)
    xb = x.astype(BF16)
    per_layer = []
    for l in range(DEPTH):
        st = dict(cache_ssd_conv=cache_ssd_conv[l], state_ssd=state_ssd[l], cache_att_k=cache_att_k[l],
                  cache_att_v=cache_att_v[l], cache_lru_conv=cache_lru_conv[l], state_lru=state_lru[l])
        x, xb, states = _layer(x, xb, st, _prep_layer(w, l), (bp, tp, bs, ts), l)
        per_layer.append(states)
    stack = lambda k: jnp.stack([s[k] for s in per_layer], axis=0)
    return (x[:n_p].reshape(bp, tp, D_MODEL), x[n_p:].reshape(bs, ts, D_MODEL),
            stack("p_ssd_conv"), stack("s_ssd_conv"), stack("p_ssd_state"), stack("s_ssd_state"),
            stack("p_att_k"), stack("s_att_k"), stack("p_att_v"), stack("s_att_v"),
            stack("p_lru_conv"), stack("s_lru_conv"), stack("p_lru_state"), stack("s_lru_state"))
```

```python
import functools
import math

import jax
import jax.numpy as jnp
from jax import lax
from jax.experimental import pallas as pl
from jax.experimental.pallas import tpu as pltpu

F32 = jnp.float32
BF16 = jnp.bfloat16
HIGHEST = lax.Precision.HIGHEST

D_MODEL = 1024
DEPTH = 2
CHUNK = 64
CONV_WIDTH = 4
SSD_HEADS = 16
SSD_HEAD_DIM = 64
SSD_GROUPS = 4
SSD_STATE = 128
SSD_CONV_DIM = D_MODEL + 2 * SSD_GROUPS * SSD_STATE
ATT_HEADS = 16
ATT_HEAD_DIM = 64
ATT_LEFT = 8 * CHUNK
ATT_BAND = ATT_LEFT + CHUNK
REL_CLIP = 128
ATT_SCALE = ATT_HEAD_DIM ** -0.5
LRU_BLOCKS = 16
LRU_C = 8.0
N_BRANCH = 3
N_EXPERTS = 64
N_GROUPS = 8
GROUP_SIZE = N_EXPERTS // N_GROUPS
TOPK_GROUPS = 4
TOP_K = 8
EXPERT_HIDDEN = D_MODEL // 4
ROUTED_SCALE = 2.5
DN_ALPHA = (2.0 * DEPTH) ** 0.25
LN_EPS = 1e-5
RMS_EPS = 1e-5

LANES = 128
SUBLANES = 8
CONV_PAD = 8

COL_G, COL_Z, COL_XBC, COL_XL, COL_YL = 0, 3, 2, 6, 7
COL_Q, COL_K, COL_V = 0, 1, 2
ATT_SUB = 256
MOE_GROUP = 4
SSD_CHUNKS_PER_STEP = 4
NEG_BIG = -1e30


def _params(semantics, vmem_mb):
    return pltpu.CompilerParams(dimension_semantics=semantics, vmem_limit_bytes=vmem_mb * 2 ** 20)


def _tile(n, candidates):
    for c in candidates:
        if n % c == 0:
            return c
    raise ValueError(f"no tile for {n}")


def _nn(a, b, precision=None):
    return jnp.dot(a, b, preferred_element_type=F32, precision=precision)


def _nt(a, b, precision=None):
    return lax.dot_general(a, b, (((1,), (1,)), ((), ())), preferred_element_type=F32, precision=precision)


def _tn(a, b):
    return lax.dot_general(a, b, (((0,), (0,)), ((), ())), preferred_element_type=F32)


def _sigmoid(x):
    return 0.5 * jnp.tanh(0.5 * x) + 0.5


def _silu(x):
    return x * _sigmoid(x)


def _softplus(x):
    return jnp.maximum(x, 0.0) + jnp.log1p(jnp.exp(-jnp.abs(x)))


def _layer_norm(y, g, b):
    mu = jnp.mean(y, axis=-1, keepdims=True)
    d = y - mu
    var = jnp.mean(d * d, axis=-1, keepdims=True)
    return d * lax.rsqrt(var + LN_EPS) * g + b


def _const_spec(shape):
    nd = len(shape)
    return pl.BlockSpec(shape, lambda *_: (0,) * nd)


def _shared_slab(ybuf):
    if ybuf is None:
        return [], [], {}
    return [ybuf], [pl.BlockSpec(memory_space=pl.ANY)], None


def _mm_body(a_ref, b_ref, o_ref):
    o_ref[...] = _nn(a_ref[...], b_ref[...]).astype(o_ref.dtype)


def _matmul(a, b, tn, name, out_dtype=F32):
    m, k = a.shape
    n = b.shape[1]
    tm = _tile(m, (1024, 512, 256, 128))
    return pl.pallas_call(
        _mm_body,
        grid=(m // tm, n // tn),
        in_specs=[pl.BlockSpec((tm, k), lambda i, j: (i, 0)), pl.BlockSpec((k, tn), lambda i, j: (0, j))],
        out_specs=pl.BlockSpec((tm, tn), lambda i, j: (i, j)),
        out_shape=jax.ShapeDtypeStruct((m, n), out_dtype),
        compiler_params=_params(("parallel", "arbitrary"), 40),
        name=name,
    )(a, b)


def _conv_block(xp_scr, x_ref, cw_ref, cb_ref, rows):
    lo = CONV_PAD - (CONV_WIDTH - 1)
    xp_scr[CONV_PAD:CONV_PAD + rows, :] = x_ref[...]
    y = cb_ref[...] + xp_scr[lo:lo + rows, :] * cw_ref[0:1, :]
    for k in range(1, CONV_WIDTH):
        y = y + xp_scr[lo + k:lo + k + rows, :] * cw_ref[k:k + 1, :]
    tail = xp_scr[lo + rows:CONV_PAD + rows, :]
    xp_scr[lo:CONV_PAD, :] = tail
    return y, tail


def _split3(x):
    hi = x.astype(BF16)
    r1 = x - hi.astype(F32)
    mid = r1.astype(BF16)
    return hi, mid, (r1 - mid.astype(F32)).astype(BF16)


def _ssd_body(z_ref, xbc_ref, dt_ref, buf0_ref, h0_ref, cw_ref, cb_ref, dtb_ref, alog_ref, dch_ref, nw_ref,
              expand_ref, *rest, nc, cps):
    y_ref, bufo_ref, ho_ref, xp_scr, h_scr = rest[-5:]
    c = pl.program_id(1)
    lo = CONV_PAD - (CONV_WIDTH - 1)

    @pl.when(c == 0)
    def _():
        xp_scr[lo:CONV_PAD, :] = buf0_ref[0]
        h_scr[...] = h0_ref[0]

    conv, tail = _conv_block(xp_scr, xbc_ref, cw_ref, cb_ref, cps * CHUNK)
    bufo_ref[0] = tail
    xbc_all = _silu(conv)
    dt_all = _softplus(dt_ref[...] + dtb_ref[...])
    da_all = dt_all * (-jnp.exp(alog_ref[...]))
    row = lax.broadcasted_iota(jnp.int32, (CHUNK, CHUNK), 0)
    col = lax.broadcasted_iota(jnp.int32, (CHUNK, CHUNK), 1)
    causal = row >= col
    tri = causal.astype(BF16)
    eye = (lax.broadcasted_iota(jnp.int32, (LANES, LANES), 0)
           == lax.broadcasted_iota(jnp.int32, (LANES, LANES), 1)).astype(BF16)
    for ci in range(cps):
        rs = slice(ci * CHUNK, (ci + 1) * CHUNK)
        _ssd_chunk(xbc_all[rs], dt_all[rs], da_all[rs], z_ref[rs, :], tri, eye, causal, expand_ref, dch_ref,
                   nw_ref, h_scr, y_ref, rs)

    @pl.when(c == nc - 1)
    def _():
        ho_ref[0] = h_scr[...]


def _ssd_chunk(xbc, dt, da, z, tri, eye, causal, expand_ref, dch_ref, nw_ref, h_scr, y_ref, rs):
    xs = xbc[:, :D_MODEL]
    bm = xbc[:, D_MODEL:D_MODEL + SSD_GROUPS * SSD_STATE].astype(BF16)
    cm = xbc[:, D_MODEL + SSD_GROUPS * SSD_STATE:].astype(BF16)
    cs = sum(_nn(tri, piece) for piece in _split3(da))
    cs_last = cs[CHUNK - 1:CHUNK, :]
    to_end = jnp.exp(cs_last - cs)
    ecs = jnp.exp(cs)
    expand = expand_ref[...]
    wide = sum(_nn(piece, expand) for piece in _split3(jnp.concatenate([dt, to_end, ecs], axis=0)))
    dt_ch = wide[0:CHUNK]
    to_end_ch = wide[CHUNK:2 * CHUNK]
    ecs_ch = wide[2 * CHUNK:3 * CHUNK]
    dec_ch = ecs_ch[CHUNK - 1:CHUNK, :]
    cs_t = sum(_nt(eye, piece) for piece in _split3(cs))

    xdt = xs * dt_ch
    xdt_b = xdt.astype(BF16)
    xw_b = (xdt * to_end_ch).astype(BF16)
    gw = SSD_HEADS // SSD_GROUPS * SSD_HEAD_DIM
    y_parts = []
    for g in range(SSD_GROUPS):
        bg = bm[:, g * SSD_STATE:(g + 1) * SSD_STATE]
        cg = cm[:, g * SSD_STATE:(g + 1) * SSD_STATE]
        gsl = slice(g * gw, (g + 1) * gw)
        cb = _nt(cg, bg)
        h_prev = h_scr[:, gsl]
        y_off = _nn(cg, h_prev.astype(BF16)) * ecs_ch[:, gsl]
        st = _tn(bg, xw_b[:, gsl])
        h_scr[:, gsl] = h_prev * dec_ch[:, gsl] + st
        diag = []
        for r in range(SSD_HEADS // SSD_GROUPS):
            h = g * (SSD_HEADS // SSD_GROUPS) + r
            seg = cs[:, h:h + 1] - cs_t[h:h + 1, :CHUNK]
            decay = jnp.where(causal, jnp.exp(jnp.where(causal, seg, 0.0)), 0.0)
            sc = (cb * decay).astype(BF16)
            diag.append(_nn(sc, xdt_b[:, h * SSD_HEAD_DIM:(h + 1) * SSD_HEAD_DIM]))
        y_parts.append(jnp.concatenate(diag, axis=-1) + y_off)
    y = jnp.concatenate(y_parts, axis=-1) + xs * dch_ref[...]
    v = y * _silu(z)
    outs = []
    for g in range(SSD_GROUPS):
        vg = v[:, g * gw:(g + 1) * gw]
        ms = jnp.mean(vg * vg, axis=-1, keepdims=True)
        outs.append(vg * lax.rsqrt(ms + RMS_EPS))
    y_ref[rs, :] = (jnp.concatenate(outs, axis=-1) * nw_ref[...]).astype(y_ref.dtype)


def _ssd_call(proj, proj_dt, buf0, h0_t, p, nb, nchunks, blk0, ybuf, name):
    n = proj.shape[0]
    cps = _tile(nchunks, (SSD_CHUNKS_PER_STEP, 1))
    nc = nchunks // cps
    rows = cps * CHUNK
    row = lambda b, c: blk0 // cps + b * nc + c
    extra, extra_specs, _ = _shared_slab(ybuf)
    in_specs = [
        pl.BlockSpec((rows, D_MODEL), lambda b, c: (row(b, c), COL_Z)),
        pl.BlockSpec((rows, SSD_CONV_DIM), lambda b, c: (row(b, c), COL_XBC)),
        pl.BlockSpec((rows, LANES), lambda b, c: (row(b, c), 0)),
        pl.BlockSpec((1, CONV_WIDTH - 1, SSD_CONV_DIM), lambda b, c: (b, 0, 0)),
        pl.BlockSpec((1, SSD_STATE, D_MODEL), lambda b, c: (b, 0, 0)),
        _const_spec((CONV_WIDTH, SSD_CONV_DIM)),
        _const_spec((1, SSD_CONV_DIM)),
        _const_spec((1, LANES)),
        _const_spec((1, LANES)),
        _const_spec((1, D_MODEL)),
        _const_spec((1, D_MODEL)),
        _const_spec((LANES, D_MODEL)),
    ] + extra_specs
    out_specs = [
        pl.BlockSpec((rows, D_MODEL), lambda b, c: (row(b, c), 0)),
        pl.BlockSpec((1, CONV_WIDTH - 1, SSD_CONV_DIM), lambda b, c: (b, 0, 0)),
        pl.BlockSpec((1, SSD_STATE, D_MODEL), lambda b, c: (b, 0, 0)),
    ]
    out_shape = [
        jax.ShapeDtypeStruct((n, D_MODEL), BF16),
        jax.ShapeDtypeStruct((nb, CONV_WIDTH - 1, SSD_CONV_DIM), F32),
        jax.ShapeDtypeStruct((nb, SSD_STATE, D_MODEL), F32),
    ]
    return pl.pallas_call(
        functools.partial(_ssd_body, nc=nc, cps=cps),
        grid=(nb, nc),
        in_specs=in_specs,
        out_specs=out_specs,
        out_shape=out_shape,
        scratch_shapes=[pltpu.VMEM((CONV_PAD + rows, SSD_CONV_DIM), F32), pltpu.VMEM((SSD_STATE, D_MODEL), F32)],
        input_output_aliases={len(in_specs) - 1: 0} if extra else {},
        compiler_params=_params(("parallel", "arbitrary"), 40),
        name=name,
    )(proj, proj, proj_dt, buf0, h0_t, p["ssd_conv_w"], p["ssd_conv_b"], p["ssd_dt_bias"], p["ssd_a_log"],
      p["ssd_d_ch"], p["ssd_norm_w"], p["ssd_expand"], *extra)


def _att_body(q_ref, kp_ref, kc_ref, vp_ref, vc_ref, bias_ref, *rest, rows, sub, mask_first):
    o_ref, k_scr, v_scr = rest[-3:]
    t = pl.program_id(1)
    k_scr[0:ATT_LEFT, :] = kp_ref[...]
    k_scr[ATT_LEFT:ATT_LEFT + rows, :] = kc_ref[...]
    v_scr[0:ATT_LEFT, :] = vp_ref[...]
    v_scr[ATT_LEFT:ATT_LEFT + rows, :] = vc_ref[...]
    win = sub + ATT_LEFT
    first = lax.broadcasted_iota(jnp.int32, (1, LANES), 1) < ATT_HEAD_DIM
    for s in range(rows // sub):
        r0 = s * sub
        if mask_first:
            pos = lax.broadcasted_iota(jnp.int32, (1, win), 1) + r0
            dead = jnp.logical_and(pos < ATT_LEFT, t == 0)
        for hp in range(ATT_HEADS // 2):
            ls = slice(hp * LANES, (hp + 1) * LANES)
            q2 = q_ref[r0:r0 + sub, ls]
            k2 = k_scr[r0:r0 + win, ls]
            v2 = v_scr[r0:r0 + win, ls]
            zero = jnp.zeros_like(q2)
            outs = []
            for h, qh in ((2 * hp, jnp.where(first, q2, zero)), (2 * hp + 1, jnp.where(first, zero, q2))):
                sc = _nt(qh, k2) + bias_ref[h]
                if mask_first:
                    sc = jnp.where(dead, NEG_BIG, sc)
                m = jnp.max(sc, axis=-1, keepdims=True)
                e = jnp.exp(sc - m)
                den = jnp.sum(e, axis=-1, keepdims=True)
                outs.append(_nn(e.astype(BF16), v2) / den)
            o_ref[r0:r0 + sub, ls] = jnp.where(first, outs[0], outs[1]).astype(o_ref.dtype)


def _att_call(qkv, k_prev, v_prev, bias, nb, nt, rows, blk0, prompt, ybuf, name):
    n = qkv.shape[0]
    sub = min(rows, ATT_SUB)
    cur = lambda col: pl.BlockSpec((rows, D_MODEL), lambda b, t: (blk0 + b * nt + t, col))
    if prompt:
        prev = lambda col: pl.BlockSpec((ATT_LEFT, D_MODEL), lambda b, t: (b * nt + jnp.maximum(t - 1, 0), col))
        kp_spec, vp_spec = prev(COL_K), prev(COL_V)
        k_prev = v_prev = qkv
    else:
        kp_spec = vp_spec = pl.BlockSpec((ATT_LEFT, D_MODEL), lambda b, t: (b, 0))
    extra, extra_specs, _ = _shared_slab(ybuf)
    in_specs = [cur(COL_Q), kp_spec, cur(COL_K), vp_spec, cur(COL_V),
                _const_spec((ATT_HEADS, sub, sub + ATT_LEFT))] + extra_specs
    return pl.pallas_call(
        functools.partial(_att_body, rows=rows, sub=sub, mask_first=prompt),
        grid=(nb, nt),
        in_specs=in_specs,
        out_specs=pl.BlockSpec((rows, D_MODEL), lambda b, t: (blk0 + b * nt + t, 0)),
        out_shape=jax.ShapeDtypeStruct((n, D_MODEL), BF16),
        scratch_shapes=[pltpu.VMEM((ATT_LEFT + rows, D_MODEL), BF16), pltpu.VMEM((ATT_LEFT + rows, D_MODEL), BF16)],
        input_output_aliases={len(in_specs) - 1: 0} if extra else {},
        compiler_params=_params(("parallel", "arbitrary"), 56),
        name=name,
    )(qkv, k_prev, qkv, v_prev, qkv, bias, *extra)


def _band_bias(table, sub):
    span = CHUNK - 1 + ATT_BAND
    idx = [min(max(ATT_LEFT + CHUNK - 1 - m, -REL_CLIP), REL_CLIP) + REL_CLIP for m in range(span)]
    vec = table[:, jnp.array(idx, jnp.int32)]
    chunk_bias = jnp.stack([vec[:, CHUNK - 1 - i:CHUNK - 1 - i + ATT_BAND] for i in range(CHUNK)], axis=1)
    blocks = [jnp.pad(chunk_bias, ((0, 0), (0, 0), (c * CHUNK, sub - CHUNK - c * CHUNK)), constant_values=NEG_BIG)
              for c in range(sub // CHUNK)]
    return jnp.concatenate(blocks, axis=1)


def _lru_scan(a, u, carry):
    rows, width = a.shape
    a3 = a.reshape(rows // SUBLANES, SUBLANES, width)
    u3 = u.reshape(rows // SUBLANES, SUBLANES, width)
    pos = lax.broadcasted_iota(jnp.int32, (1, SUBLANES, 1), 1)
    d = 1
    while d < SUBLANES:
        a_sh = pltpu.roll(a3, d, 1)
        u_sh = pltpu.roll(u3, d, 1)
        m = pos >= d
        u3 = jnp.where(m, a3 * u_sh + u3, u3)
        a3 = jnp.where(m, a3 * a_sh, a3)
        d *= 2
    out = []
    for g in range(rows // SUBLANES):
        h = a3[g] * carry + u3[g]
        carry = h[SUBLANES - 1:SUBLANES, :]
        out.append(h)
    return jnp.concatenate(out, axis=0), carry


def _gelu_tanh(x):
    return 0.5 * x * (1.0 + jnp.tanh(math.sqrt(2.0 / math.pi) * (x + 0.044715 * (x * x * x))))


def _lru_body(xl_ref, yl_ref, buf0_ref, h0_ref, cw_ref, cb_ref, wa_ref, ba_ref, wx_ref, bx_ref, lam_ref,
              *rest, rows):
    o_ref, bufo_ref, ho_ref, xp_scr, carry_scr = rest[-5:]
    c = pl.program_id(1)
    lo = CONV_PAD - (CONV_WIDTH - 1)

    @pl.when(c == 0)
    def _():
        xp_scr[lo:CONV_PAD, :] = buf0_ref[0]
        carry_scr[...] = h0_ref[0]

    xc, tail = _conv_block(xp_scr, xl_ref, cw_ref, cb_ref, rows)
    bufo_ref[0] = tail
    xcb = xc.astype(BF16)
    gate_r = _sigmoid(_nn(xcb, wa_ref[...]) + ba_ref[...])
    gate_i = _sigmoid(_nn(xcb, wx_ref[...]) + bx_ref[...])
    log_a = LRU_C * gate_r * (-_softplus(-lam_ref[...]))
    a = jnp.exp(log_a)
    u = jnp.sqrt(1.0 - jnp.exp(2.0 * log_a)) * (gate_i * xc)
    h, carry = _lru_scan(a, u, carry_scr[...])
    o_ref[...] = (h * _gelu_tanh(yl_ref[...])).astype(o_ref.dtype)
    carry_scr[...] = carry
    ho_ref[0] = carry


def _lru_call(proj, buf0, h0, p, nb, nc, rows, blk0, ybuf, name):
    n = proj.shape[0]
    row = lambda b, c: blk0 + b * nc + c
    extra, extra_specs, _ = _shared_slab(ybuf)
    in_specs = [
        pl.BlockSpec((rows, D_MODEL), lambda b, c: (row(b, c), COL_XL)),
        pl.BlockSpec((rows, D_MODEL), lambda b, c: (row(b, c), COL_YL)),
        pl.BlockSpec((1, CONV_WIDTH - 1, D_MODEL), lambda b, c: (b, 0, 0)),
        pl.BlockSpec((1, 1, D_MODEL), lambda b, c: (b, 0, 0)),
        _const_spec((CONV_WIDTH, D_MODEL)),
        _const_spec((1, D_MODEL)),
        _const_spec((D_MODEL, D_MODEL)),
        _const_spec((1, D_MODEL)),
        _const_spec((D_MODEL, D_MODEL)),
        _const_spec((1, D_MODEL)),
        _const_spec((1, D_MODEL)),
    ] + extra_specs
    out_specs = [
        pl.BlockSpec((rows, D_MODEL), lambda b, c: (row(b, c), 0)),
        pl.BlockSpec((1, CONV_WIDTH - 1, D_MODEL), lambda b, c: (b, 0, 0)),
        pl.BlockSpec((1, 1, D_MODEL), lambda b, c: (b, 0, 0)),
    ]
    out_shape = [
        jax.ShapeDtypeStruct((n, D_MODEL), BF16),
        jax.ShapeDtypeStruct((nb, CONV_WIDTH - 1, D_MODEL), F32),
        jax.ShapeDtypeStruct((nb, 1, D_MODEL), F32),
    ]
    return pl.pallas_call(
        functools.partial(_lru_body, rows=rows),
        grid=(nb, nc),
        in_specs=in_specs,
        out_specs=out_specs,
        out_shape=out_shape,
        scratch_shapes=[pltpu.VMEM((CONV_PAD + rows, D_MODEL), F32), pltpu.VMEM((1, D_MODEL), F32)],
        input_output_aliases={len(in_specs) - 1: 0} if extra else {},
        compiler_params=_params(("parallel", "arbitrary"), 48),
        name=name,
    )(proj, proj, buf0, h0, p["lru_conv_w"], p["lru_conv_b"], p["lru_wa_d"], p["lru_ba"], p["lru_wx_d"],
      p["lru_bx"], p["lru_lambda"], *extra)


def _merge_body(s_ref, a_ref, l_ref, g_ref, x_ref, ws_ref, wa_ref, wl_ref, wo_ref, gb_ref, lg_ref, lb_ref,
                o_ref, ob_ref):
    g = _sigmoid(g_ref[...] + gb_ref[...])
    m = (g[:, 0:D_MODEL] * _nn(s_ref[...], ws_ref[...])
         + g[:, D_MODEL:2 * D_MODEL] * _nn(a_ref[...], wa_ref[...])
         + g[:, 2 * D_MODEL:3 * D_MODEL] * _nn(l_ref[...], wl_ref[...]))
    y = DN_ALPHA * x_ref[...] + _nn(m.astype(BF16), wo_ref[...])
    x1 = _layer_norm(y, lg_ref[...], lb_ref[...])
    o_ref[...] = x1
    ob_ref[...] = x1.astype(BF16)


def _merge_call(ssd_y, att_y, lru_y, proj, x, p, name):
    n = x.shape[0]
    tm = _tile(n, (256, 128))
    rowblk = lambda w: pl.BlockSpec((tm, w), lambda i: (i, 0))
    wspec = _const_spec((D_MODEL, D_MODEL))
    return pl.pallas_call(
        _merge_body,
        grid=(n // tm,),
        in_specs=[rowblk(D_MODEL), rowblk(D_MODEL), rowblk(D_MODEL),
                  pl.BlockSpec((tm, N_BRANCH * D_MODEL), lambda i: (i, COL_G)), rowblk(D_MODEL),
                  wspec, wspec, wspec, wspec,
                  _const_spec((1, N_BRANCH * D_MODEL)), _const_spec((1, D_MODEL)), _const_spec((1, D_MODEL))],
        out_specs=[rowblk(D_MODEL), rowblk(D_MODEL)],
        out_shape=[jax.ShapeDtypeStruct((n, D_MODEL), F32), jax.ShapeDtypeStruct((n, D_MODEL), BF16)],
        compiler_params=_params(("parallel",), 48),
        name=name,
    )(ssd_y, att_y, lru_y, proj, x, p["w_ssd_proj"], p["w_att_proj"], p["w_lru_proj"], p["w_out"],
      p["gate_bias"], p["ln1_g"], p["ln1_b"])


def _first_max(x, idx, big):
    m = jnp.max(x, axis=0, keepdims=True)
    first = jnp.min(jnp.where(x == m, idx, big), axis=0, keepdims=True)
    return m, idx == first


def _router_body(x_ref, rwt_ref, rb_ref, o_ref):
    tm = x_ref.shape[0]
    logits = _nt(rwt_ref[...], x_ref[...], HIGHEST)
    scores = _sigmoid(logits)
    sel = scores + rb_ref[...]
    sub = lax.broadcasted_iota(jnp.int32, (GROUP_SIZE, tm), 0)
    gscore = jnp.zeros((N_GROUPS, tm), F32)
    for g in range(N_GROUPS):
        sg = sel[g * GROUP_SIZE:(g + 1) * GROUP_SIZE, :]
        m1, hit = _first_max(sg, sub, GROUP_SIZE)
        m2 = jnp.max(jnp.where(hit, -jnp.inf, sg), axis=0, keepdims=True)
        gscore = jnp.where(sub == g, m1 + m2, gscore)
    gsel = jnp.zeros((N_GROUPS, tm), F32)
    for _ in range(TOPK_GROUPS):
        _, hit = _first_max(gscore, sub, N_GROUPS)
        gsel = jnp.where(hit, 1.0, gsel)
        gscore = jnp.where(hit, -jnp.inf, gscore)
    masked = jnp.concatenate(
        [jnp.where(gsel[g:g + 1, :] > 0.0, sel[g * GROUP_SIZE:(g + 1) * GROUP_SIZE, :], -jnp.inf)
         for g in range(N_GROUPS)], axis=0)
    eidx = lax.broadcasted_iota(jnp.int32, (N_EXPERTS, tm), 0)
    chosen = jnp.zeros((N_EXPERTS, tm), F32)
    for _ in range(TOP_K):
        _, hit = _first_max(masked, eidx, N_EXPERTS)
        chosen = jnp.where(hit, 1.0, chosen)
        masked = jnp.where(hit, -jnp.inf, masked)
    w = chosen * scores
    o_ref[...] = w / jnp.sum(w, axis=0, keepdims=True) * ROUTED_SCALE


def _router_call(x1, p, name):
    n = x1.shape[0]
    tm = _tile(n, (512, 256, 128))
    return pl.pallas_call(
        _router_body,
        grid=(n // tm,),
        in_specs=[pl.BlockSpec((tm, D_MODEL), lambda i: (i, 0)), _const_spec((N_EXPERTS, D_MODEL)),
                  _const_spec((N_EXPERTS, 1))],
        out_specs=pl.BlockSpec((N_EXPERTS, tm), lambda i: (0, i)),
        out_shape=jax.ShapeDtypeStruct((N_EXPERTS, n), F32),
        compiler_params=_params(("parallel",), 32),
        name=name,
    )(x1, p["router_w_t"], p["router_bias"])


def _swiglu(xb, wgu, width):
    gu = _nn(xb, wgu)
    return _silu(gu[:, :width]) * gu[:, width:]


def _moe_body(xb_ref, x_ref, g_ref, wgu_ref, wd_ref, sgu_ref, sd_ref, lg_ref, lb_ref, o_ref, ob_ref, acc):
    j = pl.program_id(1)
    width = MOE_GROUP * EXPERT_HIDDEN
    xb = xb_ref[...]
    hid = _swiglu(xb, wgu_ref[0], width)
    g = pltpu.roll(g_ref[...], (LANES - j * MOE_GROUP) % LANES, 1)
    scaled = [hid[:, c * EXPERT_HIDDEN:(c + 1) * EXPERT_HIDDEN] * g[:, c:c + 1] for c in range(MOE_GROUP)]
    part = _nn(jnp.concatenate(scaled, axis=-1).astype(BF16), wd_ref[0])

    @pl.when(j == 0)
    def _():
        acc[...] = part

    @pl.when(j > 0)
    def _():
        acc[...] += part

    @pl.when(j == pl.num_programs(1) - 1)
    def _():
        shared = _nn(_swiglu(xb, sgu_ref[...], EXPERT_HIDDEN).astype(BF16), sd_ref[...])
        x2 = _layer_norm(DN_ALPHA * x_ref[...] + (acc[...] + shared), lg_ref[...], lb_ref[...])
        o_ref[...] = x2
        ob_ref[...] = x2.astype(BF16)


def _moe_call(x1b, x1, gates, p, name):
    n = x1.shape[0]
    tm = _tile(n, (512, 256, 128))
    width = MOE_GROUP * EXPERT_HIDDEN
    rowblk = lambda w: pl.BlockSpec((tm, w), lambda i, j: (i, 0))
    return pl.pallas_call(
        _moe_body,
        grid=(n // tm, N_EXPERTS // MOE_GROUP),
        in_specs=[rowblk(D_MODEL), rowblk(D_MODEL), rowblk(LANES),
                  pl.BlockSpec((1, D_MODEL, 2 * width), lambda i, j: (j, 0, 0)),
                  pl.BlockSpec((1, width, D_MODEL), lambda i, j: (j, 0, 0)),
                  _const_spec((D_MODEL, 2 * EXPERT_HIDDEN)), _const_spec((EXPERT_HIDDEN, D_MODEL)),
                  _const_spec((1, D_MODEL)), _const_spec((1, D_MODEL))],
        out_specs=[rowblk(D_MODEL), rowblk(D_MODEL)],
        out_shape=[jax.ShapeDtypeStruct((n, D_MODEL), F32), jax.ShapeDtypeStruct((n, D_MODEL), BF16)],
        scratch_shapes=[pltpu.VMEM((tm, D_MODEL), F32)],
        compiler_params=_params(("parallel", "arbitrary"), 52),
        name=name,
    )(x1b, x1, gates, p["exp_wgu"], p["exp_wd"], p["sh_wgu"], p["sh_wd"], p["ln2_g"], p["ln2_b"])


def _block_diag(w):
    k, d, _ = w.shape
    eye = jnp.eye(k, dtype=w.dtype)
    return (eye[:, None, :, None] * w[:, :, None, :]).reshape(k * d, k * d)


def _side_by_side(wexp):
    ne, d, h = wexp.shape
    return wexp.reshape(ne // MOE_GROUP, MOE_GROUP, d, h).transpose(0, 2, 1, 3).reshape(
        ne // MOE_GROUP, d, MOE_GROUP * h)


def _prep_layer(w, l):
    w_in = w["w_in"][l]
    o = 0
    seg = {}
    for nm, width in (("z", D_MODEL), ("xbc", SSD_CONV_DIM), ("dt", SSD_HEADS), ("q", D_MODEL), ("k", D_MODEL),
                      ("v", D_MODEL), ("xl", D_MODEL), ("yl", D_MODEL), ("g", N_BRANCH * D_MODEL)):
        seg[nm] = w_in[:, o:o + width]
        o += width
    row = lambda a: a.reshape(1, -1)
    lane_pad = lambda a: jnp.pad(a.reshape(1, -1), ((0, 0), (0, LANES - a.shape[-1])))
    head_of_channel = jnp.arange(D_MODEL) // SSD_HEAD_DIM
    return dict(
        w_main=jnp.concatenate([seg[k] for k in ("g", "z", "xbc", "xl", "yl")], axis=1).astype(BF16),
        w_qkv=jnp.concatenate([seg["q"] * ATT_SCALE, seg["k"], seg["v"]], axis=1).astype(BF16),
        w_kv=jnp.concatenate([seg["k"], seg["v"]], axis=1).astype(BF16),
        w_dt=jnp.pad(seg["dt"], ((0, 0), (0, LANES - SSD_HEADS))).astype(BF16),
        ssd_conv_w=w["ssd_conv_w"][l], ssd_conv_b=row(w["ssd_conv_b"][l]),
        ssd_dt_bias=lane_pad(w["ssd_dt_bias"][l]), ssd_a_log=lane_pad(w["ssd_a_log"][l]),
        ssd_d_ch=row(w["ssd_d"][l][head_of_channel]), ssd_norm_w=row(w["ssd_norm_w"][l]),
        ssd_expand=(jnp.arange(LANES)[:, None] == head_of_channel[None, :]).astype(F32),
        att_table=w["att_rel_bias"][l],
        lru_conv_w=w["lru_conv_w"][l], lru_conv_b=row(w["lru_conv_b"][l]),
        lru_wa_d=_block_diag(w["lru_wa"][l]).astype(BF16), lru_ba=row(w["lru_ba"][l]),
        lru_wx_d=_block_diag(w["lru_wx"][l]).astype(BF16), lru_bx=row(w["lru_bx"][l]),
        lru_lambda=row(w["lru_lambda"][l]),
        gate_bias=row(w["gate_bias"][l]),
        w_ssd_proj=w["w_ssd_proj"][l].astype(BF16), w_att_proj=w["w_att_proj"][l].astype(BF16),
        w_lru_proj=w["w_lru_proj"][l].astype(BF16), w_out=w["w_out"][l].astype(BF16),
        ln1_g=row(w["ln1_g"][l]), ln1_b=row(w["ln1_b"][l]),
        router_w_t=w["router_w"][l].T, router_bias=w["router_bias"][l].reshape(N_EXPERTS, 1),
        exp_wgu=jnp.concatenate([_side_by_side(w["exp_w_gate"][l]), _side_by_side(w["exp_w_up"][l])],
                                axis=-1).astype(BF16),
        exp_wd=w["exp_w_down"][l].reshape(N_EXPERTS // MOE_GROUP, MOE_GROUP * EXPERT_HIDDEN, D_MODEL).astype(BF16),
        sh_wgu=jnp.concatenate([w["sh_w_gate"][l], w["sh_w_up"][l]], axis=-1).astype(BF16),
        sh_wd=w["sh_w_down"][l].astype(BF16),
        ln2_g=row(w["ln2_g"][l]), ln2_b=row(w["ln2_b"][l]),
    )


def _heads_state(h_t, nb):
    return jnp.swapaxes(h_t, 1, 2).reshape(nb, SSD_HEADS, SSD_HEAD_DIM, SSD_STATE)


def _layer(x, xb, st, p, geom, l):
    bp, tp, bs, ts = geom
    n_p = bp * tp
    ncp = tp // CHUNK
    keep = min(ATT_LEFT, tp)
    proj = _matmul(xb, p["w_main"], D_MODEL, f"in_proj_{l}")
    qkv = _matmul(xb, p["w_qkv"], D_MODEL, f"qkv_proj_{l}", BF16)
    proj_dt = _matmul(xb, p["w_dt"], LANES, f"dt_proj_{l}")
    xb_keep = jnp.concatenate([xb[:n_p].reshape(bp, tp, D_MODEL)[:, tp - keep:].reshape(bp * keep, D_MODEL),
                               xb[n_p:]], axis=0)
    kv_keep = _matmul(xb_keep, p["w_kv"], D_MODEL, f"kv_keep_{l}")

    zeros = lambda *s: jnp.zeros(s, F32)
    ssd_y, pconv, ph = _ssd_call(proj, proj_dt, zeros(bp, CONV_WIDTH - 1, SSD_CONV_DIM),
                                 zeros(bp, SSD_STATE, D_MODEL), p, bp, ncp, 0, None, f"ssd_p_{l}")
    h0_t = jnp.swapaxes(st["state_ssd"].reshape(bs, D_MODEL, SSD_STATE), 1, 2)
    ssd_y, sconv, sh = _ssd_call(proj, proj_dt, st["cache_ssd_conv"], h0_t, p, bs, ts // CHUNK, n_p // CHUNK,
                                 ssd_y, f"ssd_s_{l}")

    att_rows = _tile(tp, (ATT_LEFT,))
    bias_p = _band_bias(p["att_table"], min(att_rows, ATT_SUB))
    att_y = _att_call(qkv, None, None, bias_p, bp, tp // att_rows, att_rows, 0, True, None, f"att_p_{l}")
    kc = st["cache_att_k"].reshape(bs * ATT_LEFT, D_MODEL).astype(BF16)
    vc = st["cache_att_v"].reshape(bs * ATT_LEFT, D_MODEL).astype(BF16)
    bias_s = bias_p[:, :ts, :ts + ATT_LEFT]
    att_y = _att_call(qkv, kc, vc, bias_s, bs, 1, ts, n_p // ts, False, att_y, f"att_s_{l}")

    lru_rows = _tile(tp, (256, 128, 64))
    lru_y, plc, plh = _lru_call(proj, zeros(bp, CONV_WIDTH - 1, D_MODEL), zeros(bp, 1, D_MODEL), p, bp,
                                tp // lru_rows, lru_rows, 0, None, f"lru_p_{l}")
    lru_y, slc, slh = _lru_call(proj, st["cache_lru_conv"], st["state_lru"].reshape(bs, 1, D_MODEL), p, bs,
                                1, ts, n_p // ts, lru_y, f"lru_s_{l}")

    x1, x1b = _merge_call(ssd_y, att_y, lru_y, proj, x, p, f"merge_{l}")
    gates_t = _router_call(x1, p, f"router_{l}")
    gates = jnp.pad(gates_t.T, ((0, 0), (0, LANES - N_EXPERTS)))
    x2, x2b = _moe_call(x1b, x1, gates, p, f"moe_{l}")

    def kv(col, rows0, nb, t):
        return kv_keep[rows0:rows0 + nb * t, col * D_MODEL:(col + 1) * D_MODEL].reshape(
            nb, t, ATT_HEADS, ATT_HEAD_DIM)

    states = dict(
        p_ssd_conv=pconv, s_ssd_conv=sconv, p_ssd_state=_heads_state(ph, bp), s_ssd_state=_heads_state(sh, bs),
        p_att_k=kv(0, 0, bp, keep), s_att_k=kv(0, bp * keep, bs, ts),
        p_att_v=kv(1, 0, bp, keep), s_att_v=kv(1, bp * keep, bs, ts),
        p_lru_conv=plc, s_lru_conv=slc, p_lru_state=plh.reshape(bp, D_MODEL), s_lru_state=slh.reshape(bs, D_MODEL))
    return x2, x2b, states


def kernel(x_prompt, x_sample, cache_ssd_conv, state_ssd, cache_att_k, cache_att_v, cache_lru_conv, state_lru, w_in, ssd_conv_w, ssd_conv_b, ssd_dt_bias, ssd_a_log, ssd_d, ssd_norm_w, att_rel_bias, lru_conv_w, lru_conv_b, lru_wa, lru_ba, lru_wx, lru_bx, lru_lambda, gate_bias, w_ssd_proj, w_att_proj, w_lru_proj, w_out, ln1_g, ln1_b, router_w, router_bias, exp_w_gate, exp_w_up, exp_w_down, sh_w_gate, sh_w_up, sh_w_down, ln2_g, ln2_b):
    w = dict(w_in=w_in, ssd_conv_w=ssd_conv_w, ssd_conv_b=ssd_conv_b, ssd_dt_bias=ssd_dt_bias, ssd_a_log=ssd_a_log,
             ssd_d=ssd_d, ssd_norm_w=ssd_norm_w, att_rel_bias=att_rel_bias, lru_conv_w=lru_conv_w,
             lru_conv_b=lru_conv_b, lru_wa=lru_wa, lru_ba=lru_ba, lru_wx=lru_wx, lru_bx=lru_bx,
             lru_lambda=lru_lambda, gate_bias=gate_bias, w_ssd_proj=w_ssd_proj, w_att_proj=w_att_proj,
             w_lru_proj=w_lru_proj, w_out=w_out, ln1_g=ln1_g, ln1_b=ln1_b, router_w=router_w,
             router_bias=router_bias, exp_w_gate=exp_w_gate, exp_w_up=exp_w_up, exp_w_down=exp_w_down,
             sh_w_gate=sh_w_gate, sh_w_up=sh_w_up, sh_w_down=sh_w_down, ln2_g=ln2_g, ln2_b=ln2_b)
    bp, tp, _ = x_prompt.shape
    bs, ts, _ = x_sample.shape
    n_p = bp * tp
    x = jnp.concatenate([x_prompt.reshape(n_p, D_MODEL), x_sample.reshape(bs * ts, D_MODEL)], axis=0)
    xb = x.astype(BF16)
    per_layer = []
    for l in range(DEPTH):
        st = dict(cache_ssd_conv=cache_ssd_conv[l], state_ssd=state_ssd[l], cache_att_k=cache_att_k[l],
                  cache_att_v=cache_att_v[l], cache_lru_conv=cache_lru_conv[l], state_lru=state_lru[l])
        x, xb, states = _layer(x, xb, st, _prep_layer(w, l), (bp, tp, bs, ts), l)
        per_layer.append(states)
    stack = lambda k: jnp.stack([s[k] for s in per_layer], axis=0)
    return (x[:n_p].reshape(bp, tp, D_MODEL), x[n_p:].reshape(bs, ts, D_MODEL),
            stack("p_ssd_conv"), stack("s_ssd_conv"), stack("p_ssd_state"), stack("s_ssd_state"),
            stack("p_att_k"), stack("s_att_k"), stack("p_att_v"), stack("s_att_v"),
            stack("p_lru_conv"), stack("s_lru_conv"), stack("p_lru_state"), stack("s_lru_state"))
```

```python
import functools
import math

import jax
import jax.numpy as jnp
import numpy as np
from jax import lax
from jax.experimental import pallas as pl
from jax.experimental.pallas import tpu as pltpu

F32 = jnp.float32
BF16 = jnp.bfloat16
HIGHEST = lax.Precision.HIGHEST

D_MODEL = 1024
DEPTH = 2
CHUNK = 64
CONV_WIDTH = 4
SSD_HEADS = 16
SSD_HEAD_DIM = 64
SSD_GROUPS = 4
SSD_STATE = 128
SSD_CONV_DIM = D_MODEL + 2 * SSD_GROUPS * SSD_STATE
ATT_HEADS = 16
ATT_HEAD_DIM = 64
ATT_LEFT = 8 * CHUNK
ATT_BAND = ATT_LEFT + CHUNK
REL_CLIP = 128
ATT_SCALE = ATT_HEAD_DIM ** -0.5
LRU_BLOCKS = 16
LRU_C = 8.0
N_BRANCH = 3
N_EXPERTS = 64
N_GROUPS = 8
GROUP_SIZE = N_EXPERTS // N_GROUPS
TOPK_GROUPS = 4
TOP_K = 8
EXPERT_HIDDEN = D_MODEL // 4
ROUTED_SCALE = 2.5
DN_ALPHA = (2.0 * DEPTH) ** 0.25
LN_EPS = 1e-5
RMS_EPS = 1e-5

LANES = 128
SUBLANES = 8
CONV_PAD = 8

COL_G, COL_Z, COL_XBC, COL_XL, COL_YL = 0, 3, 2, 6, 7
COL_Q, COL_K, COL_V = 0, 1, 2
ATT_SUB = 256
MOE_GROUP = 8
SSD_CHUNKS_PER_STEP = 4
NEG_BIG = -1e30


def _params(semantics, vmem_mb):
    return pltpu.CompilerParams(dimension_semantics=semantics, vmem_limit_bytes=vmem_mb * 2 ** 20)


def _tile(n, candidates):
    for c in candidates:
        if n % c == 0:
            return c
    raise ValueError(f"no tile for {n}")


def _nn(a, b, precision=None):
    return jnp.dot(a, b, preferred_element_type=F32, precision=precision)


def _nt(a, b, precision=None):
    return lax.dot_general(a, b, (((1,), (1,)), ((), ())), preferred_element_type=F32, precision=precision)


def _tn(a, b):
    return lax.dot_general(a, b, (((0,), (0,)), ((), ())), preferred_element_type=F32)


def _sigmoid(x):
    return 0.5 * jnp.tanh(0.5 * x) + 0.5


def _silu(x):
    return x * _sigmoid(x)


def _softplus(x):
    return jnp.maximum(x, 0.0) + jnp.log1p(jnp.exp(-jnp.abs(x)))


def _layer_norm(y, g, b):
    mu = jnp.mean(y, axis=-1, keepdims=True)
    d = y - mu
    var = jnp.mean(d * d, axis=-1, keepdims=True)
    return d * lax.rsqrt(var + LN_EPS) * g + b


def _const_spec(shape):
    nd = len(shape)
    return pl.BlockSpec(shape, lambda *_: (0,) * nd)


def _shared_slab(ybuf):
    if ybuf is None:
        return [], [], {}
    return [ybuf], [pl.BlockSpec(memory_space=pl.ANY)], None


def _mm_body(a_ref, b_ref, o_ref):
    o_ref[...] = _nn(a_ref[...], b_ref[...]).astype(o_ref.dtype)


def _matmul(a, b, tn, name, out_dtype=F32):
    m, k = a.shape
    n = b.shape[1]
    tm = _tile(m, (1024, 512, 256, 128))
    return pl.pallas_call(
        _mm_body,
        grid=(m // tm, n // tn),
        in_specs=[pl.BlockSpec((tm, k), lambda i, j: (i, 0)), pl.BlockSpec((k, tn), lambda i, j: (0, j))],
        out_specs=pl.BlockSpec((tm, tn), lambda i, j: (i, j)),
        out_shape=jax.ShapeDtypeStruct((m, n), out_dtype),
        compiler_params=_params(("parallel", "arbitrary"), 40),
        name=name,
    )(a, b)


def _conv_block(xp_scr, x_ref, cw_ref, cb_ref, rows):
    lo = CONV_PAD - (CONV_WIDTH - 1)
    xp_scr[CONV_PAD:CONV_PAD + rows, :] = x_ref[...]
    y = cb_ref[...] + xp_scr[lo:lo + rows, :] * cw_ref[0:1, :]
    for k in range(1, CONV_WIDTH):
        y = y + xp_scr[lo + k:lo + k + rows, :] * cw_ref[k:k + 1, :]
    tail = xp_scr[lo + rows:CONV_PAD + rows, :]
    xp_scr[lo:CONV_PAD, :] = tail
    return y, tail


def _split3(x):
    hi = x.astype(BF16)
    r1 = x - hi.astype(F32)
    mid = r1.astype(BF16)
    return hi, mid, (r1 - mid.astype(F32)).astype(BF16)


def _ssd_body(z_ref, xbc_ref, dt_ref, buf0_ref, h0_ref, cw_ref, cb_ref, dtb_ref, alog_ref, dch_ref, nw_ref,
              expand_ref, *rest, nc, cps):
    y_ref, bufo_ref, ho_ref, xp_scr, h_scr = rest[-5:]
    c = pl.program_id(1)
    lo = CONV_PAD - (CONV_WIDTH - 1)

    @pl.when(c == 0)
    def _():
        xp_scr[lo:CONV_PAD, :] = buf0_ref[0]
        h_scr[...] = h0_ref[0]

    conv, tail = _conv_block(xp_scr, xbc_ref, cw_ref, cb_ref, cps * CHUNK)
    bufo_ref[0] = tail
    xbc_all = _silu(conv)
    dt_all = _softplus(dt_ref[...] + dtb_ref[...])
    da_all = dt_all * (-jnp.exp(alog_ref[...]))
    row = lax.broadcasted_iota(jnp.int32, (CHUNK, CHUNK), 0)
    col = lax.broadcasted_iota(jnp.int32, (CHUNK, CHUNK), 1)
    causal = row >= col
    tri = causal.astype(BF16)
    eye = (lax.broadcasted_iota(jnp.int32, (LANES, LANES), 0)
           == lax.broadcasted_iota(jnp.int32, (LANES, LANES), 1)).astype(BF16)
    for ci in range(cps):
        rs = slice(ci * CHUNK, (ci + 1) * CHUNK)
        _ssd_chunk(xbc_all[rs], dt_all[rs], da_all[rs], z_ref[rs, :], tri, eye, causal, expand_ref, dch_ref,
                   nw_ref, h_scr, y_ref, rs)

    @pl.when(c == nc - 1)
    def _():
        ho_ref[0] = h_scr[...]


def _ssd_chunk(xbc, dt, da, z, tri, eye, causal, expand_ref, dch_ref, nw_ref, h_scr, y_ref, rs):
    xs = xbc[:, :D_MODEL]
    bm = xbc[:, D_MODEL:D_MODEL + SSD_GROUPS * SSD_STATE].astype(BF16)
    cm = xbc[:, D_MODEL + SSD_GROUPS * SSD_STATE:].astype(BF16)
    cs = sum(_nn(tri, piece) for piece in _split3(da))
    cs_last = cs[CHUNK - 1:CHUNK, :]
    to_end = jnp.exp(cs_last - cs)
    ecs = jnp.exp(cs)
    expand = expand_ref[...]
    wide = sum(_nn(piece, expand) for piece in _split3(jnp.concatenate([dt, to_end, ecs], axis=0)))
    dt_ch = wide[0:CHUNK]
    to_end_ch = wide[CHUNK:2 * CHUNK]
    ecs_ch = wide[2 * CHUNK:3 * CHUNK]
    dec_ch = ecs_ch[CHUNK - 1:CHUNK, :]
    cs_t = sum(_nt(eye, piece) for piece in _split3(cs))

    xdt = xs * dt_ch
    xdt_b = xdt.astype(BF16)
    xw_b = (xdt * to_end_ch).astype(BF16)
    gw = SSD_HEADS // SSD_GROUPS * SSD_HEAD_DIM
    y_parts = []
    for g in range(SSD_GROUPS):
        bg = bm[:, g * SSD_STATE:(g + 1) * SSD_STATE]
        cg = cm[:, g * SSD_STATE:(g + 1) * SSD_STATE]
        gsl = slice(g * gw, (g + 1) * gw)
        cb = _nt(cg, bg)
        h_prev = h_scr[:, gsl]
        y_off = _nn(cg, h_prev.astype(BF16)) * ecs_ch[:, gsl]
        st = _tn(bg, xw_b[:, gsl])
        h_scr[:, gsl] = h_prev * dec_ch[:, gsl] + st
        diag = []
        for r in range(SSD_HEADS // SSD_GROUPS):
            h = g * (SSD_HEADS // SSD_GROUPS) + r
            seg = cs[:, h:h + 1] - cs_t[h:h + 1, :CHUNK]
            decay = jnp.where(causal, jnp.exp(jnp.where(causal, seg, 0.0)), 0.0)
            sc = (cb * decay).astype(BF16)
            diag.append(_nn(sc, xdt_b[:, h * SSD_HEAD_DIM:(h + 1) * SSD_HEAD_DIM]))
        y_parts.append(jnp.concatenate(diag, axis=-1) + y_off)
    y = jnp.concatenate(y_parts, axis=-1) + xs * dch_ref[...]
    v = y * _silu(z)
    outs = []
    for g in range(SSD_GROUPS):
        vg = v[:, g * gw:(g + 1) * gw]
        ms = jnp.mean(vg * vg, axis=-1, keepdims=True)
        outs.append(vg * lax.rsqrt(ms + RMS_EPS))
    y_ref[rs, :] = (jnp.concatenate(outs, axis=-1) * nw_ref[...]).astype(y_ref.dtype)


def _ssd_call(proj, proj_dt, buf0, h0_t, p, nb, nchunks, blk0, ybuf, name):
    n = proj.shape[0]
    cps = _tile(nchunks, (SSD_CHUNKS_PER_STEP, 1))
    nc = nchunks // cps
    rows = cps * CHUNK
    row = lambda b, c: blk0 // cps + b * nc + c
    extra, extra_specs, _ = _shared_slab(ybuf)
    in_specs = [
        pl.BlockSpec((rows, D_MODEL), lambda b, c: (row(b, c), COL_Z)),
        pl.BlockSpec((rows, SSD_CONV_DIM), lambda b, c: (row(b, c), COL_XBC)),
        pl.BlockSpec((rows, LANES), lambda b, c: (row(b, c), 0)),
        pl.BlockSpec((1, CONV_WIDTH - 1, SSD_CONV_DIM), lambda b, c: (b, 0, 0)),
        pl.BlockSpec((1, SSD_STATE, D_MODEL), lambda b, c: (b, 0, 0)),
        _const_spec((CONV_WIDTH, SSD_CONV_DIM)),
        _const_spec((1, SSD_CONV_DIM)),
        _const_spec((1, LANES)),
        _const_spec((1, LANES)),
        _const_spec((1, D_MODEL)),
        _const_spec((1, D_MODEL)),
        _const_spec((LANES, D_MODEL)),
    ] + extra_specs
    out_specs = [
        pl.BlockSpec((rows, D_MODEL), lambda b, c: (row(b, c), 0)),
        pl.BlockSpec((1, CONV_WIDTH - 1, SSD_CONV_DIM), lambda b, c: (b, 0, 0)),
        pl.BlockSpec((1, SSD_STATE, D_MODEL), lambda b, c: (b, 0, 0)),
    ]
    out_shape = [
        jax.ShapeDtypeStruct((n, D_MODEL), BF16),
        jax.ShapeDtypeStruct((nb, CONV_WIDTH - 1, SSD_CONV_DIM), F32),
        jax.ShapeDtypeStruct((nb, SSD_STATE, D_MODEL), F32),
    ]
    return pl.pallas_call(
        functools.partial(_ssd_body, nc=nc, cps=cps),
        grid=(nb, nc),
        in_specs=in_specs,
        out_specs=out_specs,
        out_shape=out_shape,
        scratch_shapes=[pltpu.VMEM((CONV_PAD + rows, SSD_CONV_DIM), F32), pltpu.VMEM((SSD_STATE, D_MODEL), F32)],
        input_output_aliases={len(in_specs) - 1: 0} if extra else {},
        compiler_params=_params(("parallel", "arbitrary"), 40),
        name=name,
    )(proj, proj, proj_dt, buf0, h0_t, p["ssd_conv_w"], p["ssd_conv_b"], p["ssd_dt_bias"], p["ssd_a_log"],
      p["ssd_d_ch"], p["ssd_norm_w"], p["ssd_expand"], *extra)


def _att_body(q_ref, kp_ref, kc_ref, vp_ref, vc_ref, bias_ref, *rest, rows, sub, mask_first):
    o_ref, k_scr, v_scr = rest[-3:]
    t = pl.program_id(1)
    if mask_first:
        k_scr[0:ATT_LEFT, :] = kp_ref[...]
        v_scr[0:ATT_LEFT, :] = vp_ref[...]
    else:
        for h in range(ATT_HEADS):
            hs = slice(h * ATT_HEAD_DIM, (h + 1) * ATT_HEAD_DIM)
            k_scr[0:ATT_LEFT, hs] = kp_ref[:, h, :].astype(BF16)
            v_scr[0:ATT_LEFT, hs] = vp_ref[:, h, :].astype(BF16)
    k_scr[ATT_LEFT:ATT_LEFT + rows, :] = kc_ref[...]
    v_scr[ATT_LEFT:ATT_LEFT + rows, :] = vc_ref[...]
    win = sub + ATT_LEFT
    first = lax.broadcasted_iota(jnp.int32, (1, LANES), 1) < ATT_HEAD_DIM
    for s in range(rows // sub):
        r0 = s * sub
        if mask_first:
            pos = lax.broadcasted_iota(jnp.int32, (1, win), 1) + r0
            dead = jnp.logical_and(pos < ATT_LEFT, t == 0)
        for hp in range(ATT_HEADS // 2):
            ls = slice(hp * LANES, (hp + 1) * LANES)
            q2 = q_ref[r0:r0 + sub, ls]
            k2 = k_scr[r0:r0 + win, ls]
            v2 = v_scr[r0:r0 + win, ls]
            zero = jnp.zeros_like(q2)
            outs = []
            for h, qh in ((2 * hp, jnp.where(first, q2, zero)), (2 * hp + 1, jnp.where(first, zero, q2))):
                sc = _nt(qh, k2) + bias_ref[h]
                if mask_first:
                    sc = jnp.where(dead, NEG_BIG, sc)
                m = jnp.max(sc, axis=-1, keepdims=True)
                e = jnp.exp(sc - m)
                den = jnp.sum(e, axis=-1, keepdims=True)
                outs.append(_nn(e.astype(BF16), v2) / den)
            o_ref[r0:r0 + sub, ls] = jnp.where(first, outs[0], outs[1]).astype(o_ref.dtype)


def _att_call(qkv, k_prev, v_prev, bias, nb, nt, rows, blk0, prompt, ybuf, name):
    n = qkv.shape[0]
    sub = min(rows, ATT_SUB)
    cur = lambda col: pl.BlockSpec((rows, D_MODEL), lambda b, t: (blk0 + b * nt + t, col))
    if prompt:
        prev = lambda col: pl.BlockSpec((ATT_LEFT, D_MODEL), lambda b, t: (b * nt + jnp.maximum(t - 1, 0), col))
        kp_spec, vp_spec = prev(COL_K), prev(COL_V)
        k_prev = v_prev = qkv
    else:
        kp_spec = vp_spec = pl.BlockSpec((None, ATT_LEFT, ATT_HEADS, ATT_HEAD_DIM), lambda b, t: (b, 0, 0, 0))
    extra, extra_specs, _ = _shared_slab(ybuf)
    in_specs = [cur(COL_Q), kp_spec, cur(COL_K), vp_spec, cur(COL_V),
                _const_spec((ATT_HEADS, sub, sub + ATT_LEFT))] + extra_specs
    return pl.pallas_call(
        functools.partial(_att_body, rows=rows, sub=sub, mask_first=prompt),
        grid=(nb, nt),
        in_specs=in_specs,
        out_specs=pl.BlockSpec((rows, D_MODEL), lambda b, t: (blk0 + b * nt + t, 0)),
        out_shape=jax.ShapeDtypeStruct((n, D_MODEL), BF16),
        scratch_shapes=[pltpu.VMEM((ATT_LEFT + rows, D_MODEL), BF16), pltpu.VMEM((ATT_LEFT + rows, D_MODEL), BF16)],
        input_output_aliases={len(in_specs) - 1: 0} if extra else {},
        compiler_params=_params(("parallel", "arbitrary"), 56),
        name=name,
    )(qkv, k_prev, qkv, v_prev, qkv, bias, *extra)


def _band_bias(table, sub):
    win = sub + ATT_LEFT
    span = sub - 1 + win
    idx = [min(max(win - 1 - m, -REL_CLIP), REL_CLIP) + REL_CLIP for m in range(span)]
    vec = table[:, np.array(idx, np.int32)]
    rolled = jnp.concatenate([vec[:, sub - 1:], vec[:, :sub - 1]], axis=1)
    toep = jnp.tile(rolled, (1, sub))[:, :sub * (span - 1)].reshape(-1, sub, span - 1)[:, :, :win]
    r = np.arange(sub)[:, None]
    j = np.arange(win)[None, :] - (r // CHUNK) * CHUNK
    in_band = np.logical_and(j >= 0, j < ATT_BAND)
    return jnp.where(in_band[None], toep, NEG_BIG)


def _lru_scan(a, u, carry):
    rows, width = a.shape
    a3 = a.reshape(rows // SUBLANES, SUBLANES, width)
    u3 = u.reshape(rows // SUBLANES, SUBLANES, width)
    pos = lax.broadcasted_iota(jnp.int32, (1, SUBLANES, 1), 1)
    d = 1
    while d < SUBLANES:
        a_sh = pltpu.roll(a3, d, 1)
        u_sh = pltpu.roll(u3, d, 1)
        m = pos >= d
        u3 = jnp.where(m, a3 * u_sh + u3, u3)
        a3 = jnp.where(m, a3 * a_sh, a3)
        d *= 2
    out = []
    for g in range(rows // SUBLANES):
        h = a3[g] * carry + u3[g]
        carry = h[SUBLANES - 1:SUBLANES, :]
        out.append(h)
    return jnp.concatenate(out, axis=0), carry


def _gelu_tanh(x):
    return 0.5 * x * (1.0 + jnp.tanh(math.sqrt(2.0 / math.pi) * (x + 0.044715 * (x * x * x))))


def _lru_body(xl_ref, yl_ref, buf0_ref, h0_ref, cw_ref, cb_ref, wa_ref, ba_ref, wx_ref, bx_ref, lam_ref,
              *rest, rows):
    o_ref, bufo_ref, ho_ref, xp_scr, carry_scr = rest[-5:]
    c = pl.program_id(1)
    lo = CONV_PAD - (CONV_WIDTH - 1)

    @pl.when(c == 0)
    def _():
        xp_scr[lo:CONV_PAD, :] = buf0_ref[0]
        carry_scr[...] = h0_ref[0]

    xc, tail = _conv_block(xp_scr, xl_ref, cw_ref, cb_ref, rows)
    bufo_ref[0] = tail
    xcb = xc.astype(BF16)
    gate_r = _sigmoid(_nn(xcb, wa_ref[...]) + ba_ref[...])
    gate_i = _sigmoid(_nn(xcb, wx_ref[...]) + bx_ref[...])
    log_a = LRU_C * gate_r * (-_softplus(-lam_ref[...]))
    a = jnp.exp(log_a)
    u = jnp.sqrt(1.0 - a * a) * (gate_i * xc)
    h, carry = _lru_scan(a, u, carry_scr[...])
    o_ref[...] = (h * _gelu_tanh(yl_ref[...])).astype(o_ref.dtype)
    carry_scr[...] = carry
    ho_ref[0] = carry


def _lru_call(proj, buf0, h0, p, nb, nc, rows, blk0, ybuf, name):
    n = proj.shape[0]
    row = lambda b, c: blk0 + b * nc + c
    extra, extra_specs, _ = _shared_slab(ybuf)
    in_specs = [
        pl.BlockSpec((rows, D_MODEL), lambda b, c: (row(b, c), COL_XL)),
        pl.BlockSpec((rows, D_MODEL), lambda b, c: (row(b, c), COL_YL)),
        pl.BlockSpec((1, CONV_WIDTH - 1, D_MODEL), lambda b, c: (b, 0, 0)),
        pl.BlockSpec((1, 1, D_MODEL), lambda b, c: (b, 0, 0)),
        _const_spec((CONV_WIDTH, D_MODEL)),
        _const_spec((1, D_MODEL)),
        _const_spec((D_MODEL, D_MODEL)),
        _const_spec((1, D_MODEL)),
        _const_spec((D_MODEL, D_MODEL)),
        _const_spec((1, D_MODEL)),
        _const_spec((1, D_MODEL)),
    ] + extra_specs
    out_specs = [
        pl.BlockSpec((rows, D_MODEL), lambda b, c: (row(b, c), 0)),
        pl.BlockSpec((1, CONV_WIDTH - 1, D_MODEL), lambda b, c: (b, 0, 0)),
        pl.BlockSpec((1, 1, D_MODEL), lambda b, c: (b, 0, 0)),
    ]
    out_shape = [
        jax.ShapeDtypeStruct((n, D_MODEL), BF16),
        jax.ShapeDtypeStruct((nb, CONV_WIDTH - 1, D_MODEL), F32),
        jax.ShapeDtypeStruct((nb, 1, D_MODEL), F32),
    ]
    return pl.pallas_call(
        functools.partial(_lru_body, rows=rows),
        grid=(nb, nc),
        in_specs=in_specs,
        out_specs=out_specs,
        out_shape=out_shape,
        scratch_shapes=[pltpu.VMEM((CONV_PAD + rows, D_MODEL), F32), pltpu.VMEM((1, D_MODEL), F32)],
        input_output_aliases={len(in_specs) - 1: 0} if extra else {},
        compiler_params=_params(("parallel", "arbitrary"), 48),
        name=name,
    )(proj, proj, buf0, h0, p["lru_conv_w"], p["lru_conv_b"], p["lru_wa_d"], p["lru_ba"], p["lru_wx_d"],
      p["lru_bx"], p["lru_lambda"], *extra)


def _merge_body(s_ref, a_ref, l_ref, g_ref, x_ref, ws_ref, wa_ref, wl_ref, wo_ref, gb_ref, lg_ref, lb_ref,
                o_ref, ob_ref):
    g = _sigmoid(g_ref[...] + gb_ref[...])
    m = (g[:, 0:D_MODEL] * _nn(s_ref[...], ws_ref[...])
         + g[:, D_MODEL:2 * D_MODEL] * _nn(a_ref[...], wa_ref[...])
         + g[:, 2 * D_MODEL:3 * D_MODEL] * _nn(l_ref[...], wl_ref[...]))
    y = DN_ALPHA * x_ref[...] + _nn(m.astype(BF16), wo_ref[...])
    x1 = _layer_norm(y, lg_ref[...], lb_ref[...])
    o_ref[...] = x1
    ob_ref[...] = x1.astype(BF16)


def _merge_call(ssd_y, att_y, lru_y, proj, x, p, name):
    n = x.shape[0]
    tm = _tile(n, (256, 128))
    rowblk = lambda w: pl.BlockSpec((tm, w), lambda i: (i, 0))
    wspec = _const_spec((D_MODEL, D_MODEL))
    return pl.pallas_call(
        _merge_body,
        grid=(n // tm,),
        in_specs=[rowblk(D_MODEL), rowblk(D_MODEL), rowblk(D_MODEL),
                  pl.BlockSpec((tm, N_BRANCH * D_MODEL), lambda i: (i, COL_G)), rowblk(D_MODEL),
                  wspec, wspec, wspec, wspec,
                  _const_spec((1, N_BRANCH * D_MODEL)), _const_spec((1, D_MODEL)), _const_spec((1, D_MODEL))],
        out_specs=[rowblk(D_MODEL), rowblk(D_MODEL)],
        out_shape=[jax.ShapeDtypeStruct((n, D_MODEL), F32), jax.ShapeDtypeStruct((n, D_MODEL), BF16)],
        compiler_params=_params(("parallel",), 48),
        name=name,
    )(ssd_y, att_y, lru_y, proj, x, p["w_ssd_proj"], p["w_att_proj"], p["w_lru_proj"], p["w_out"],
      p["gate_bias"], p["ln1_g"], p["ln1_b"])


def _first_max(x, idx, big):
    m = jnp.max(x, axis=0, keepdims=True)
    first = jnp.min(jnp.where(x == m, idx, big), axis=0, keepdims=True)
    return m, idx == first


def _router_body(x_ref, rwt_ref, rb_ref, o_ref):
    tm = x_ref.shape[0]
    logits = _nt(rwt_ref[...], x_ref[...], HIGHEST)
    scores = _sigmoid(logits)
    sel = scores + rb_ref[...]
    sub = lax.broadcasted_iota(jnp.int32, (GROUP_SIZE, tm), 0)
    gscore = jnp.zeros((N_GROUPS, tm), F32)
    for g in range(N_GROUPS):
        sg = sel[g * GROUP_SIZE:(g + 1) * GROUP_SIZE, :]
        m1, hit = _first_max(sg, sub, GROUP_SIZE)
        m2 = jnp.max(jnp.where(hit, -jnp.inf, sg), axis=0, keepdims=True)
        gscore = jnp.where(sub == g, m1 + m2, gscore)
    gsel = jnp.zeros((N_GROUPS, tm), F32)
    for _ in range(TOPK_GROUPS):
        _, hit = _first_max(gscore, sub, N_GROUPS)
        gsel = jnp.where(hit, 1.0, gsel)
        gscore = jnp.where(hit, -jnp.inf, gscore)
    masked = jnp.concatenate(
        [jnp.where(gsel[g:g + 1, :] > 0.0, sel[g * GROUP_SIZE:(g + 1) * GROUP_SIZE, :], -jnp.inf)
         for g in range(N_GROUPS)], axis=0)
    eidx = lax.broadcasted_iota(jnp.int32, (N_EXPERTS, tm), 0)
    chosen = jnp.zeros((N_EXPERTS, tm), F32)
    for _ in range(TOP_K):
        _, hit = _first_max(masked, eidx, N_EXPERTS)
        chosen = jnp.where(hit, 1.0, chosen)
        masked = jnp.where(hit, -jnp.inf, masked)
    w = chosen * scores
    o_ref[...] = w / jnp.sum(w, axis=0, keepdims=True) * ROUTED_SCALE


def _router_call(x1, p, name):
    n = x1.shape[0]
    tm = _tile(n, (512, 256, 128))
    return pl.pallas_call(
        _router_body,
        grid=(n // tm,),
        in_specs=[pl.BlockSpec((tm, D_MODEL), lambda i: (i, 0)), _const_spec((N_EXPERTS, D_MODEL)),
                  _const_spec((N_EXPERTS, 1))],
        out_specs=pl.BlockSpec((N_EXPERTS, tm), lambda i: (0, i)),
        out_shape=jax.ShapeDtypeStruct((N_EXPERTS, n), F32),
        compiler_params=_params(("parallel",), 32),
        name=name,
    )(x1, p["router_w_t"], p["router_bias"])


def _swiglu(xb, wgu, width):
    gu = _nn(xb, wgu)
    return _silu(gu[:, :width]) * gu[:, width:]


def _moe_body(xb_ref, x_ref, g_ref, wg_ref, wu_ref, wd_ref, sgu_ref, sd_ref, lg_ref, lb_ref, o_ref, ob_ref, acc):
    j = pl.program_id(1)
    xb = xb_ref[...]
    g = pltpu.roll(g_ref[...], (LANES - j * MOE_GROUP) % LANES, 1)
    scaled = [_silu(_nn(xb, wg_ref[c])) * _nn(xb, wu_ref[c]) * g[:, c:c + 1] for c in range(MOE_GROUP)]
    part = _nn(jnp.concatenate(scaled, axis=-1).astype(BF16), wd_ref[0])

    @pl.when(j == 0)
    def _():
        acc[...] = part

    @pl.when(j > 0)
    def _():
        acc[...] += part

    @pl.when(j == pl.num_programs(1) - 1)
    def _():
        shared = _nn(_swiglu(xb, sgu_ref[...], EXPERT_HIDDEN).astype(BF16), sd_ref[...])
        x2 = _layer_norm(DN_ALPHA * x_ref[...] + (acc[...] + shared), lg_ref[...], lb_ref[...])
        o_ref[...] = x2
        ob_ref[...] = x2.astype(BF16)


def _moe_call(x1b, x1, gates, p, name):
    n = x1.shape[0]
    tm = _tile(n, (512, 256, 128))
    width = MOE_GROUP * EXPERT_HIDDEN
    rowblk = lambda w: pl.BlockSpec((tm, w), lambda i, j: (i, 0))
    return pl.pallas_call(
        _moe_body,
        grid=(n // tm, N_EXPERTS // MOE_GROUP),
        in_specs=[rowblk(D_MODEL), rowblk(D_MODEL), rowblk(LANES),
                  pl.BlockSpec((MOE_GROUP, D_MODEL, EXPERT_HIDDEN), lambda i, j: (j, 0, 0)),
                  pl.BlockSpec((MOE_GROUP, D_MODEL, EXPERT_HIDDEN), lambda i, j: (j, 0, 0)),
                  pl.BlockSpec((1, width, D_MODEL), lambda i, j: (j, 0, 0)),
                  _const_spec((D_MODEL, 2 * EXPERT_HIDDEN)), _const_spec((EXPERT_HIDDEN, D_MODEL)),
                  _const_spec((1, D_MODEL)), _const_spec((1, D_MODEL))],
        out_specs=[rowblk(D_MODEL), rowblk(D_MODEL)],
        out_shape=[jax.ShapeDtypeStruct((n, D_MODEL), F32), jax.ShapeDtypeStruct((n, D_MODEL), BF16)],
        scratch_shapes=[pltpu.VMEM((tm, D_MODEL), F32)],
        compiler_params=_params(("parallel", "arbitrary"), 58),
        name=name,
    )(x1b, x1, gates, p["exp_wg"], p["exp_wu"], p["exp_wd"], p["sh_wgu"], p["sh_wd"], p["ln2_g"], p["ln2_b"])


def _block_diag(w):
    k, d, _ = w.shape
    eye = jnp.eye(k, dtype=w.dtype)
    return (eye[:, None, :, None] * w[:, :, None, :]).reshape(k * d, k * d)


def _prep_layer(w, l):
    w_in = w["w_in"][l]
    o = 0
    seg = {}
    for nm, width in (("z", D_MODEL), ("xbc", SSD_CONV_DIM), ("dt", SSD_HEADS), ("q", D_MODEL), ("k", D_MODEL),
                      ("v", D_MODEL), ("xl", D_MODEL), ("yl", D_MODEL), ("g", N_BRANCH * D_MODEL)):
        seg[nm] = w_in[:, o:o + width]
        o += width
    row = lambda a: a.reshape(1, -1)
    lane_pad = lambda a: jnp.pad(a.reshape(1, -1), ((0, 0), (0, LANES - a.shape[-1])))
    head_of_channel = jnp.arange(D_MODEL) // SSD_HEAD_DIM
    return dict(
        w_main=jnp.concatenate([seg[k] for k in ("g", "z", "xbc", "xl", "yl")], axis=1).astype(BF16),
        w_qkv=jnp.concatenate([seg["q"] * ATT_SCALE, seg["k"], seg["v"]], axis=1).astype(BF16),
        w_kv=jnp.concatenate([seg["k"], seg["v"]], axis=1).astype(BF16),
        w_dt=jnp.pad(seg["dt"], ((0, 0), (0, LANES - SSD_HEADS))).astype(BF16),
        ssd_conv_w=w["ssd_conv_w"][l], ssd_conv_b=row(w["ssd_conv_b"][l]),
        ssd_dt_bias=lane_pad(w["ssd_dt_bias"][l]), ssd_a_log=lane_pad(w["ssd_a_log"][l]),
        ssd_d_ch=row(w["ssd_d"][l][head_of_channel]), ssd_norm_w=row(w["ssd_norm_w"][l]),
        ssd_expand=(jnp.arange(LANES)[:, None] == head_of_channel[None, :]).astype(F32),
        att_table=w["att_rel_bias"][l],
        lru_conv_w=w["lru_conv_w"][l], lru_conv_b=row(w["lru_conv_b"][l]),
        lru_wa_d=_block_diag(w["lru_wa"][l]).astype(BF16), lru_ba=row(w["lru_ba"][l]),
        lru_wx_d=_block_diag(w["lru_wx"][l]).astype(BF16), lru_bx=row(w["lru_bx"][l]),
        lru_lambda=row(w["lru_lambda"][l]),
        gate_bias=row(w["gate_bias"][l]),
        w_ssd_proj=w["w_ssd_proj"][l].astype(BF16), w_att_proj=w["w_att_proj"][l].astype(BF16),
        w_lru_proj=w["w_lru_proj"][l].astype(BF16), w_out=w["w_out"][l].astype(BF16),
        ln1_g=row(w["ln1_g"][l]), ln1_b=row(w["ln1_b"][l]),
        router_w_t=w["router_w"][l].T, router_bias=w["router_bias"][l].reshape(N_EXPERTS, 1),
        exp_wg=w["exp_w_gate"][l].astype(BF16), exp_wu=w["exp_w_up"][l].astype(BF16),
        exp_wd=w["exp_w_down"][l].reshape(N_EXPERTS // MOE_GROUP, MOE_GROUP * EXPERT_HIDDEN, D_MODEL).astype(BF16),
        sh_wgu=jnp.concatenate([w["sh_w_gate"][l], w["sh_w_up"][l]], axis=-1).astype(BF16),
        sh_wd=w["sh_w_down"][l].astype(BF16),
        ln2_g=row(w["ln2_g"][l]), ln2_b=row(w["ln2_b"][l]),
    )


def _heads_state(h_t, nb):
    return jnp.swapaxes(h_t, 1, 2).reshape(nb, SSD_HEADS, SSD_HEAD_DIM, SSD_STATE)


def _layer(x, xb, st, p, geom, l):
    bp, tp, bs, ts = geom
    n_p = bp * tp
    ncp = tp // CHUNK
    keep = min(ATT_LEFT, tp)
    proj = _matmul(xb, p["w_main"], 2 * D_MODEL, f"in_proj_{l}")
    qkv = _matmul(xb, p["w_qkv"], D_MODEL, f"qkv_proj_{l}", BF16)
    proj_dt = _matmul(xb, p["w_dt"], LANES, f"dt_proj_{l}")
    xb_keep = jnp.concatenate([xb[:n_p].reshape(bp, tp, D_MODEL)[:, tp - keep:].reshape(bp * keep, D_MODEL),
                               xb[n_p:]], axis=0)
    kv_keep = _matmul(xb_keep, p["w_kv"], D_MODEL, f"kv_keep_{l}")

    zeros = lambda *s: jnp.zeros(s, F32)
    ssd_y, pconv, ph = _ssd_call(proj, proj_dt, zeros(bp, CONV_WIDTH - 1, SSD_CONV_DIM),
                                 zeros(bp, SSD_STATE, D_MODEL), p, bp, ncp, 0, None, f"ssd_p_{l}")
    h0_t = jnp.swapaxes(st["state_ssd"].reshape(bs, D_MODEL, SSD_STATE), 1, 2)
    ssd_y, sconv, sh = _ssd_call(proj, proj_dt, st["cache_ssd_conv"], h0_t, p, bs, ts // CHUNK, n_p // CHUNK,
                                 ssd_y, f"ssd_s_{l}")

    att_rows = _tile(tp, (ATT_LEFT,))
    bias_p = _band_bias(p["att_table"], min(att_rows, ATT_SUB))
    att_y = _att_call(qkv, None, None, bias_p, bp, tp // att_rows, att_rows, 0, True, None, f"att_p_{l}")
    bias_s = bias_p[:, :ts, :ts + ATT_LEFT]
    att_y = _att_call(qkv, st["cache_att_k"], st["cache_att_v"], bias_s, bs, 1, ts, n_p // ts, False, att_y, f"att_s_{l}")

    lru_rows = _tile(tp, (256, 128, 64))
    lru_y, plc, plh = _lru_call(proj, zeros(bp, CONV_WIDTH - 1, D_MODEL), zeros(bp, 1, D_MODEL), p, bp,
                                tp // lru_rows, lru_rows, 0, None, f"lru_p_{l}")
    lru_y, slc, slh = _lru_call(proj, st["cache_lru_conv"], st["state_lru"].reshape(bs, 1, D_MODEL), p, bs,
                                1, ts, n_p // ts, lru_y, f"lru_s_{l}")

    x1, x1b = _merge_call(ssd_y, att_y, lru_y, proj, x, p, f"merge_{l}")
    gates_t = _router_call(x1, p, f"router_{l}")
    gates = jnp.pad(gates_t.T, ((0, 0), (0, LANES - N_EXPERTS)))
    x2, x2b = _moe_call(x1b, x1, gates, p, f"moe_{l}")

    def kv(col, rows0, nb, t):
        return kv_keep[rows0:rows0 + nb * t, col * D_MODEL:(col + 1) * D_MODEL].reshape(
            nb, t, ATT_HEADS, ATT_HEAD_DIM)

    states = dict(
        p_ssd_conv=pconv, s_ssd_conv=sconv, p_ssd_state=_heads_state(ph, bp), s_ssd_state=_heads_state(sh, bs),
        p_att_k=kv(0, 0, bp, keep), s_att_k=kv(0, bp * keep, bs, ts),
        p_att_v=kv(1, 0, bp, keep), s_att_v=kv(1, bp * keep, bs, ts),
        p_lru_conv=plc, s_lru_conv=slc, p_lru_state=plh.reshape(bp, D_MODEL), s_lru_state=slh.reshape(bs, D_MODEL))
    return x2, x2b, states


def kernel(x_prompt, x_sample, cache_ssd_conv, state_ssd, cache_att_k, cache_att_v, cache_lru_conv, state_lru, w_in, ssd_conv_w, ssd_conv_b, ssd_dt_bias, ssd_a_log, ssd_d, ssd_norm_w, att_rel_bias, lru_conv_w, lru_conv_b, lru_wa, lru_ba, lru_wx, lru_bx, lru_lambda, gate_bias, w_ssd_proj, w_att_proj, w_lru_proj, w_out, ln1_g, ln1_b, router_w, router_bias, exp_w_gate, exp_w_up, exp_w_down, sh_w_gate, sh_w_up, sh_w_down, ln2_g, ln2_b):
    w = dict(w_in=w_in, ssd_conv_w=ssd_conv_w, ssd_conv_b=ssd_conv_b, ssd_dt_bias=ssd_dt_bias, ssd_a_log=ssd_a_log,
             ssd_d=ssd_d, ssd_norm_w=ssd_norm_w, att_rel_bias=att_rel_bias, lru_conv_w=lru_conv_w,
             lru_conv_b=lru_conv_b, lru_wa=lru_wa, lru_ba=lru_ba, lru_wx=lru_wx, lru_bx=lru_bx,
             lru_lambda=lru_lambda, gate_bias=gate_bias, w_ssd_proj=w_ssd_proj, w_att_proj=w_att_proj,
             w_lru_proj=w_lru_proj, w_out=w_out, ln1_g=ln1_g, ln1_b=ln1_b, router_w=router_w,
             router_bias=router_bias, exp_w_gate=exp_w_gate, exp_w_up=exp_w_up, exp_w_down=exp_w_down,
             sh_w_gate=sh_w_gate, sh_w_up=sh_w_up, sh_w_down=sh_w_down, ln2_g=ln2_g, ln2_b=ln2_b)
    bp, tp, _ = x_prompt.shape
    bs, ts, _ = x_sample.shape
    n_p = bp * tp
    x = jnp.concatenate([x_prompt.reshape(n_p, D_MODEL), x_sample.reshape(bs * ts, D_MODEL)], axis=0)
    xb = x.astype(BF16)
    per_layer = []
    for l in range(DEPTH):
        st = dict(cache_ssd_conv=cache_ssd_conv[l], state_ssd=state_ssd[l], cache_att_k=cache_att_k[l],
                  cache_att_v=cache_att_v[l], cache_lru_conv=cache_lru_conv[l], state_lru=state_lru[l])
        x, xb, states = _layer(x, xb, st, _prep_layer(w, l), (bp, tp, bs, ts), l)
        per_layer.append(states)
    stack = lambda k: jnp.stack([s[k] for s in per_layer], axis=0)
    return (x[:n_p].reshape(bp, tp, D_MODEL), x[n_p:].reshape(bs, ts, D_MODEL),
            stack("p_ssd_conv"), stack("s_ssd_conv"), stack("p_ssd_state"), stack("s_ssd_state"),
            stack("p_att_k"), stack("s_att_k"), stack("p_att_v"), stack("s_att_v"),
            stack("p_lru_conv"), stack("s_lru_conv"), stack("p_lru_state"), stack("s_lru_state"))
```

```python
import functools
import math

import jax
import jax.numpy as jnp
import numpy as np
from jax import lax
from jax.experimental import pallas as pl
from jax.experimental.pallas import tpu as pltpu

F32 = jnp.float32
BF16 = jnp.bfloat16
HIGHEST = lax.Precision.HIGHEST

D_MODEL = 1024
DEPTH = 2
CHUNK = 64
CONV_WIDTH = 4
SSD_HEADS = 16
SSD_HEAD_DIM = 64
SSD_GROUPS = 4
SSD_STATE = 128
SSD_CONV_DIM = D_MODEL + 2 * SSD_GROUPS * SSD_STATE
ATT_HEADS = 16
ATT_HEAD_DIM = 64
ATT_LEFT = 8 * CHUNK
ATT_BAND = ATT_LEFT + CHUNK
REL_CLIP = 128
ATT_SCALE = ATT_HEAD_DIM ** -0.5
LRU_BLOCKS = 16
LRU_C = 8.0
N_BRANCH = 3
N_EXPERTS = 64
N_GROUPS = 8
GROUP_SIZE = N_EXPERTS // N_GROUPS
TOPK_GROUPS = 4
TOP_K = 8
EXPERT_HIDDEN = D_MODEL // 4
ROUTED_SCALE = 2.5
DN_ALPHA = (2.0 * DEPTH) ** 0.25
LN_EPS = 1e-5
RMS_EPS = 1e-5

LANES = 128
SUBLANES = 8
CONV_PAD = 8

COL_G, COL_Z, COL_XBC, COL_XL, COL_YL = 0, 3, 2, 6, 7
COL_Q, COL_K, COL_V = 0, 1, 2
ATT_SUB = 256
MOE_GROUP = 8
SSD_CHUNKS_PER_STEP = 4
NEG_BIG = -1e30


def _params(semantics, vmem_mb):
    return pltpu.CompilerParams(dimension_semantics=semantics, vmem_limit_bytes=vmem_mb * 2 ** 20)


def _tile(n, candidates):
    for c in candidates:
        if n % c == 0:
            return c
    raise ValueError(f"no tile for {n}")


def _nn(a, b, precision=None):
    return jnp.dot(a, b, preferred_element_type=F32, precision=precision)


def _nt(a, b, precision=None):
    return lax.dot_general(a, b, (((1,), (1,)), ((), ())), preferred_element_type=F32, precision=precision)


def _tn(a, b):
    return lax.dot_general(a, b, (((0,), (0,)), ((), ())), preferred_element_type=F32)


def _sigmoid(x):
    return 0.5 * jnp.tanh(0.5 * x) + 0.5


def _silu(x):
    return x * _sigmoid(x)


def _softplus(x):
    return jnp.maximum(x, 0.0) + jnp.log1p(jnp.exp(-jnp.abs(x)))


def _layer_norm(y, g, b):
    mu = jnp.mean(y, axis=-1, keepdims=True)
    d = y - mu
    var = jnp.mean(d * d, axis=-1, keepdims=True)
    return d * lax.rsqrt(var + LN_EPS) * g + b


def _const_spec(shape):
    nd = len(shape)
    return pl.BlockSpec(shape, lambda *_: (0,) * nd)


def _mm_body(a_ref, b_ref, o_ref):
    o_ref[...] = _nn(a_ref[...], b_ref[...]).astype(o_ref.dtype)


def _matmul(a, b, tn, name, out_dtype=F32):
    m, k = a.shape
    n = b.shape[1]
    tm = _tile(m, (1024, 512, 256, 128))
    return pl.pallas_call(
        _mm_body,
        grid=(m // tm, n // tn),
        in_specs=[pl.BlockSpec((tm, k), lambda i, j: (i, 0)), pl.BlockSpec((k, tn), lambda i, j: (0, j))],
        out_specs=pl.BlockSpec((tm, tn), lambda i, j: (i, j)),
        out_shape=jax.ShapeDtypeStruct((m, n), out_dtype),
        compiler_params=_params(("parallel", "arbitrary"), 40),
        name=name,
    )(a, b)


def _conv_block(xp_scr, x_ref, cw_ref, cb_ref, rows):
    lo = CONV_PAD - (CONV_WIDTH - 1)
    xp_scr[CONV_PAD:CONV_PAD + rows, :] = x_ref[...]
    y = cb_ref[...] + xp_scr[lo:lo + rows, :] * cw_ref[0:1, :]
    for k in range(1, CONV_WIDTH):
        y = y + xp_scr[lo + k:lo + k + rows, :] * cw_ref[k:k + 1, :]
    tail = xp_scr[lo + rows:CONV_PAD + rows, :]
    xp_scr[lo:CONV_PAD, :] = tail
    return y, tail


def _split3(x):
    hi = x.astype(BF16)
    r1 = x - hi.astype(F32)
    mid = r1.astype(BF16)
    return hi, mid, (r1 - mid.astype(F32)).astype(BF16)


def _ssd_body(z_ref, xbc_ref, dt_ref, buf0_ref, h0_ref, cw_ref, cb_ref, dtb_ref, alog_ref, dch_ref, nw_ref,
              expand_ref, y_ref, bufo_ref, ho_ref, xp_scr, h_scr, *, nc, cps):
    c = pl.program_id(1)
    lo = CONV_PAD - (CONV_WIDTH - 1)

    @pl.when(c == 0)
    def _():
        xp_scr[lo:CONV_PAD, :] = buf0_ref[0]
        h_scr[...] = h0_ref[0]

    conv, tail = _conv_block(xp_scr, xbc_ref, cw_ref, cb_ref, cps * CHUNK)
    bufo_ref[0] = tail
    xbc_all = _silu(conv)
    dt_all = _softplus(dt_ref[...] + dtb_ref[...])
    da_all = dt_all * (-jnp.exp(alog_ref[...]))
    row = lax.broadcasted_iota(jnp.int32, (CHUNK, CHUNK), 0)
    col = lax.broadcasted_iota(jnp.int32, (CHUNK, CHUNK), 1)
    causal = row >= col
    tri = causal.astype(BF16)
    eye = (lax.broadcasted_iota(jnp.int32, (LANES, LANES), 0)
           == lax.broadcasted_iota(jnp.int32, (LANES, LANES), 1)).astype(BF16)
    for ci in range(cps):
        rs = slice(ci * CHUNK, (ci + 1) * CHUNK)
        _ssd_chunk(xbc_all[rs], dt_all[rs], da_all[rs], z_ref[rs, :], tri, eye, causal, expand_ref, dch_ref,
                   nw_ref, h_scr, y_ref, rs)

    @pl.when(c == nc - 1)
    def _():
        ho_ref[0] = h_scr[...]


def _ssd_chunk(xbc, dt, da, z, tri, eye, causal, expand_ref, dch_ref, nw_ref, h_scr, y_ref, rs):
    xs = xbc[:, :D_MODEL]
    bm = xbc[:, D_MODEL:D_MODEL + SSD_GROUPS * SSD_STATE].astype(BF16)
    cm = xbc[:, D_MODEL + SSD_GROUPS * SSD_STATE:].astype(BF16)
    cs = sum(_nn(tri, piece) for piece in _split3(da))
    cs_last = cs[CHUNK - 1:CHUNK, :]
    to_end = jnp.exp(cs_last - cs)
    ecs = jnp.exp(cs)
    expand = expand_ref[...]
    wide = sum(_nn(piece, expand) for piece in _split3(jnp.concatenate([dt, to_end, ecs], axis=0)))
    dt_ch = wide[0:CHUNK]
    to_end_ch = wide[CHUNK:2 * CHUNK]
    ecs_ch = wide[2 * CHUNK:3 * CHUNK]
    dec_ch = ecs_ch[CHUNK - 1:CHUNK, :]
    cs_t = sum(_nt(eye, piece) for piece in _split3(cs))

    xdt = xs * dt_ch
    xdt_b = xdt.astype(BF16)
    xw_b = (xdt * to_end_ch).astype(BF16)
    gw = SSD_HEADS // SSD_GROUPS * SSD_HEAD_DIM
    y_parts = []
    for g in range(SSD_GROUPS):
        bg = bm[:, g * SSD_STATE:(g + 1) * SSD_STATE]
        cg = cm[:, g * SSD_STATE:(g + 1) * SSD_STATE]
        gsl = slice(g * gw, (g + 1) * gw)
        cb = _nt(cg, bg)
        h_prev = h_scr[:, gsl]
        y_off = _nn(cg, h_prev.astype(BF16)) * ecs_ch[:, gsl]
        st = _tn(bg, xw_b[:, gsl])
        h_scr[:, gsl] = h_prev * dec_ch[:, gsl] + st
        diag = []
        for r in range(SSD_HEADS // SSD_GROUPS):
            h = g * (SSD_HEADS // SSD_GROUPS) + r
            seg = cs[:, h:h + 1] - cs_t[h:h + 1, :CHUNK]
            decay = jnp.where(causal, jnp.exp(jnp.where(causal, seg, 0.0)), 0.0)
            sc = (cb * decay).astype(BF16)
            diag.append(_nn(sc, xdt_b[:, h * SSD_HEAD_DIM:(h + 1) * SSD_HEAD_DIM]))
        y_parts.append(jnp.concatenate(diag, axis=-1) + y_off)
    y = jnp.concatenate(y_parts, axis=-1) + xs * dch_ref[...]
    v = y * _silu(z)
    outs = []
    for g in range(SSD_GROUPS):
        vg = v[:, g * gw:(g + 1) * gw]
        ms = jnp.mean(vg * vg, axis=-1, keepdims=True)
        outs.append(vg * lax.rsqrt(ms + RMS_EPS))
    y_ref[rs, :] = (jnp.concatenate(outs, axis=-1) * nw_ref[...]).astype(y_ref.dtype)


def _ssd_call(proj, proj_dt, buf0, h0_t, p, nb, nchunks, blk0, name):
    cps = _tile(nchunks, (SSD_CHUNKS_PER_STEP, 1))
    nc = nchunks // cps
    rows = cps * CHUNK
    row = lambda b, c: blk0 // cps + b * nc + c
    in_specs = [
        pl.BlockSpec((rows, D_MODEL), lambda b, c: (row(b, c), COL_Z)),
        pl.BlockSpec((rows, SSD_CONV_DIM), lambda b, c: (row(b, c), COL_XBC)),
        pl.BlockSpec((rows, LANES), lambda b, c: (row(b, c), 0)),
        pl.BlockSpec((1, CONV_WIDTH - 1, SSD_CONV_DIM), lambda b, c: (b, 0, 0)),
        pl.BlockSpec((1, SSD_STATE, D_MODEL), lambda b, c: (b, 0, 0)),
        _const_spec((CONV_WIDTH, SSD_CONV_DIM)),
        _const_spec((1, SSD_CONV_DIM)),
        _const_spec((1, LANES)),
        _const_spec((1, LANES)),
        _const_spec((1, D_MODEL)),
        _const_spec((1, D_MODEL)),
        _const_spec((LANES, D_MODEL)),
    ]
    out_specs = [
        pl.BlockSpec((rows, D_MODEL), lambda b, c: (b * nc + c, 0)),
        pl.BlockSpec((1, CONV_WIDTH - 1, SSD_CONV_DIM), lambda b, c: (b, 0, 0)),
        pl.BlockSpec((1, SSD_STATE, D_MODEL), lambda b, c: (b, 0, 0)),
    ]
    out_shape = [
        jax.ShapeDtypeStruct((nb * nchunks * CHUNK, D_MODEL), BF16),
        jax.ShapeDtypeStruct((nb, CONV_WIDTH - 1, SSD_CONV_DIM), F32),
        jax.ShapeDtypeStruct((nb, SSD_STATE, D_MODEL), F32),
    ]
    return pl.pallas_call(
        functools.partial(_ssd_body, nc=nc, cps=cps),
        grid=(nb, nc),
        in_specs=in_specs,
        out_specs=out_specs,
        out_shape=out_shape,
        scratch_shapes=[pltpu.VMEM((CONV_PAD + rows, SSD_CONV_DIM), F32), pltpu.VMEM((SSD_STATE, D_MODEL), F32)],
        compiler_params=_params(("parallel", "arbitrary"), 40),
        name=name,
    )(proj, proj, proj_dt, buf0, h0_t, p["ssd_conv_w"], p["ssd_conv_b"], p["ssd_dt_bias"], p["ssd_a_log"],
      p["ssd_d_ch"], p["ssd_norm_w"], p["ssd_expand"])


def _att_body(q_ref, kp_ref, kc_ref, vp_ref, vc_ref, bias_ref, o_ref, k_scr, v_scr, *, rows, sub, mask_first):
    t = pl.program_id(1)
    if mask_first:
        k_scr[0:ATT_LEFT, :] = kp_ref[...]
        v_scr[0:ATT_LEFT, :] = vp_ref[...]
    else:
        for hp in range(ATT_HEADS // 2):
            ls = slice(hp * LANES, (hp + 1) * LANES)
            for src, dst in ((kp_ref, k_scr), (vp_ref, v_scr)):
                pair = jnp.concatenate([src[:, 2 * hp, :], src[:, 2 * hp + 1, :]], axis=-1)
                dst[0:ATT_LEFT, ls] = pair.astype(BF16)
    k_scr[ATT_LEFT:ATT_LEFT + rows, :] = kc_ref[...]
    v_scr[ATT_LEFT:ATT_LEFT + rows, :] = vc_ref[...]
    win = sub + ATT_LEFT
    first = lax.broadcasted_iota(jnp.int32, (1, LANES), 1) < ATT_HEAD_DIM
    for s in range(rows // sub):
        r0 = s * sub
        if mask_first:
            pos = lax.broadcasted_iota(jnp.int32, (1, win), 1) + r0
            dead = jnp.logical_and(pos < ATT_LEFT, t == 0)
        for hp in range(ATT_HEADS // 2):
            ls = slice(hp * LANES, (hp + 1) * LANES)
            q2 = q_ref[r0:r0 + sub, ls]
            k2 = k_scr[r0:r0 + win, ls]
            v2 = v_scr[r0:r0 + win, ls]
            zero = jnp.zeros_like(q2)
            outs = []
            for h, qh in ((2 * hp, jnp.where(first, q2, zero)), (2 * hp + 1, jnp.where(first, zero, q2))):
                sc = _nt(qh, k2) + bias_ref[h]
                if mask_first:
                    sc = jnp.where(dead, NEG_BIG, sc)
                m = jnp.max(sc, axis=-1, keepdims=True)
                e = jnp.exp(sc - m)
                den = jnp.sum(e, axis=-1, keepdims=True)
                outs.append(_nn(e.astype(BF16), v2) / den)
            o_ref[r0:r0 + sub, ls] = jnp.where(first, outs[0], outs[1]).astype(o_ref.dtype)


def _att_call(qkv, k_prev, v_prev, bias, nb, nt, rows, blk0, prompt, name):
    sub = min(rows, ATT_SUB)
    cur = lambda col: pl.BlockSpec((rows, D_MODEL), lambda b, t: (blk0 + b * nt + t, col))
    if prompt:
        prev = lambda col: pl.BlockSpec((ATT_LEFT, D_MODEL), lambda b, t: (b * nt + jnp.maximum(t - 1, 0), col))
        kp_spec, vp_spec = prev(COL_K), prev(COL_V)
        k_prev = v_prev = qkv
    else:
        kp_spec = vp_spec = pl.BlockSpec((None, ATT_LEFT, ATT_HEADS, ATT_HEAD_DIM), lambda b, t: (b, 0, 0, 0))
    in_specs = [cur(COL_Q), kp_spec, cur(COL_K), vp_spec, cur(COL_V),
                _const_spec((ATT_HEADS, sub, sub + ATT_LEFT))]
    return pl.pallas_call(
        functools.partial(_att_body, rows=rows, sub=sub, mask_first=prompt),
        grid=(nb, nt),
        in_specs=in_specs,
        out_specs=pl.BlockSpec((rows, D_MODEL), lambda b, t: (b * nt + t, 0)),
        out_shape=jax.ShapeDtypeStruct((nb * nt * rows, D_MODEL), BF16),
        scratch_shapes=[pltpu.VMEM((ATT_LEFT + rows, D_MODEL), BF16), pltpu.VMEM((ATT_LEFT + rows, D_MODEL), BF16)],
        compiler_params=_params(("parallel", "arbitrary"), 56),
        name=name,
    )(qkv, k_prev, qkv, v_prev, qkv, bias)


def _band_bias(table, sub):
    win = sub + ATT_LEFT
    span = -(-(sub - 1 + win) // LANES) * LANES
    idx = [min(max(win - 1 - m, -REL_CLIP), REL_CLIP) + REL_CLIP for m in range(span)]
    vec = table[:, np.array(idx, np.int32)].reshape(ATT_HEADS, 1, span)
    return pl.pallas_call(
        functools.partial(_band_bias_body, sub=sub, win=win),
        grid=(ATT_HEADS,),
        in_specs=[pl.BlockSpec((1, 1, span), lambda h: (h, 0, 0))],
        out_specs=pl.BlockSpec((1, sub, win), lambda h: (h, 0, 0)),
        out_shape=jax.ShapeDtypeStruct((ATT_HEADS, sub, win), F32),
        compiler_params=_params(("parallel",), 32),
        name="band_bias",
    )(vec)


def _band_bias_body(vec_ref, o_ref, *, sub, win):
    span = vec_ref.shape[-1]
    toep = pltpu.roll(jnp.broadcast_to(vec_ref[0], (sub, span)), span - (sub - 1), 1, stride=1, stride_axis=0)
    r = lax.broadcasted_iota(jnp.int32, (sub, win), 0)
    j = lax.broadcasted_iota(jnp.int32, (sub, win), 1) - (r - jnp.bitwise_and(r, CHUNK - 1))
    o_ref[0] = jnp.where(jnp.logical_and(j >= 0, j < ATT_BAND), toep[:, :win], NEG_BIG)


def _lru_scan(a, u, carry):
    rows, width = a.shape
    a3 = a.reshape(rows // SUBLANES, SUBLANES, width)
    u3 = u.reshape(rows // SUBLANES, SUBLANES, width)
    pos = lax.broadcasted_iota(jnp.int32, (1, SUBLANES, 1), 1)
    d = 1
    while d < SUBLANES:
        a_sh = pltpu.roll(a3, d, 1)
        u_sh = pltpu.roll(u3, d, 1)
        m = pos >= d
        u3 = jnp.where(m, a3 * u_sh + u3, u3)
        a3 = jnp.where(m, a3 * a_sh, a3)
        d *= 2
    out = []
    for g in range(rows // SUBLANES):
        h = a3[g] * carry + u3[g]
        carry = h[SUBLANES - 1:SUBLANES, :]
        out.append(h)
    return jnp.concatenate(out, axis=0), carry


def _gelu_tanh(x):
    return 0.5 * x * (1.0 + jnp.tanh(math.sqrt(2.0 / math.pi) * (x + 0.044715 * (x * x * x))))


def _lru_body(xl_ref, yl_ref, buf0_ref, h0_ref, cw_ref, cb_ref, wa_ref, ba_ref, wx_ref, bx_ref, lam_ref,
              o_ref, bufo_ref, ho_ref, xp_scr, carry_scr, *, rows):
    c = pl.program_id(1)
    lo = CONV_PAD - (CONV_WIDTH - 1)

    @pl.when(c == 0)
    def _():
        xp_scr[lo:CONV_PAD, :] = buf0_ref[0]
        carry_scr[...] = h0_ref[0]

    xc, tail = _conv_block(xp_scr, xl_ref, cw_ref, cb_ref, rows)
    bufo_ref[0] = tail
    xcb = xc.astype(BF16)
    gate_r = _sigmoid(_nn(xcb, wa_ref[...]) + ba_ref[...])
    gate_i = _sigmoid(_nn(xcb, wx_ref[...]) + bx_ref[...])
    log_a = LRU_C * gate_r * (-_softplus(-lam_ref[...]))
    a = jnp.exp(log_a)
    u = jnp.sqrt(1.0 - a * a) * (gate_i * xc)
    h, carry = _lru_scan(a, u, carry_scr[...])
    o_ref[...] = (h * _gelu_tanh(yl_ref[...])).astype(o_ref.dtype)
    carry_scr[...] = carry
    ho_ref[0] = carry


def _lru_call(proj, buf0, h0, p, nb, nc, rows, blk0, name):
    row = lambda b, c: blk0 + b * nc + c
    in_specs = [
        pl.BlockSpec((rows, D_MODEL), lambda b, c: (row(b, c), COL_XL)),
        pl.BlockSpec((rows, D_MODEL), lambda b, c: (row(b, c), COL_YL)),
        pl.BlockSpec((1, CONV_WIDTH - 1, D_MODEL), lambda b, c: (b, 0, 0)),
        pl.BlockSpec((1, 1, D_MODEL), lambda b, c: (b, 0, 0)),
        _const_spec((CONV_WIDTH, D_MODEL)),
        _const_spec((1, D_MODEL)),
        _const_spec((D_MODEL, D_MODEL)),
        _const_spec((1, D_MODEL)),
        _const_spec((D_MODEL, D_MODEL)),
        _const_spec((1, D_MODEL)),
        _const_spec((1, D_MODEL)),
    ]
    out_specs = [
        pl.BlockSpec((rows, D_MODEL), lambda b, c: (b * nc + c, 0)),
        pl.BlockSpec((1, CONV_WIDTH - 1, D_MODEL), lambda b, c: (b, 0, 0)),
        pl.BlockSpec((1, 1, D_MODEL), lambda b, c: (b, 0, 0)),
    ]
    out_shape = [
        jax.ShapeDtypeStruct((nb * nc * rows, D_MODEL), BF16),
        jax.ShapeDtypeStruct((nb, CONV_WIDTH - 1, D_MODEL), F32),
        jax.ShapeDtypeStruct((nb, 1, D_MODEL), F32),
    ]
    return pl.pallas_call(
        functools.partial(_lru_body, rows=rows),
        grid=(nb, nc),
        in_specs=in_specs,
        out_specs=out_specs,
        out_shape=out_shape,
        scratch_shapes=[pltpu.VMEM((CONV_PAD + rows, D_MODEL), F32), pltpu.VMEM((1, D_MODEL), F32)],
        compiler_params=_params(("parallel", "arbitrary"), 48),
        name=name,
    )(proj, proj, buf0, h0, p["lru_conv_w"], p["lru_conv_b"], p["lru_wa_d"], p["lru_ba"], p["lru_wx_d"],
      p["lru_bx"], p["lru_lambda"])


def _merge_body(sp_ref, ss_ref, ap_ref, as_ref, lp_ref, ls_ref, g_ref, x_ref, ws_ref, wa_ref, wl_ref, wo_ref,
                gb_ref, lg_ref, lb_ref, o_ref, ob_ref, *, prompt_tiles):
    in_prompt = pl.program_id(0) < prompt_tiles
    branch = lambda p_ref, s_ref: jnp.where(in_prompt, p_ref[...], s_ref[...])
    g = _sigmoid(g_ref[...] + gb_ref[...])
    m = (g[:, 0:D_MODEL] * _nn(branch(sp_ref, ss_ref), ws_ref[...])
         + g[:, D_MODEL:2 * D_MODEL] * _nn(branch(ap_ref, as_ref), wa_ref[...])
         + g[:, 2 * D_MODEL:3 * D_MODEL] * _nn(branch(lp_ref, ls_ref), wl_ref[...]))
    y = DN_ALPHA * x_ref[...] + _nn(m.astype(BF16), wo_ref[...])
    x1 = _layer_norm(y, lg_ref[...], lb_ref[...])
    o_ref[...] = x1
    ob_ref[...] = x1.astype(BF16)


def _merge_call(branches, proj, x, p, name):
    n = x.shape[0]
    n_p = branches[0][0].shape[0]
    tm = _tile(math.gcd(n_p, n - n_p), (256, 128))
    pt = n_p // tm
    rowblk = lambda w: pl.BlockSpec((tm, w), lambda i: (i, 0))
    pblk = pl.BlockSpec((tm, D_MODEL), lambda i: (jnp.minimum(i, pt - 1), 0))
    sblk = pl.BlockSpec((tm, D_MODEL), lambda i: (jnp.maximum(i - pt, 0), 0))
    wspec = _const_spec((D_MODEL, D_MODEL))
    return pl.pallas_call(
        functools.partial(_merge_body, prompt_tiles=pt),
        grid=(n // tm,),
        in_specs=[pblk, sblk, pblk, sblk, pblk, sblk,
                  pl.BlockSpec((tm, N_BRANCH * D_MODEL), lambda i: (i, COL_G)), rowblk(D_MODEL),
                  wspec, wspec, wspec, wspec,
                  _const_spec((1, N_BRANCH * D_MODEL)), _const_spec((1, D_MODEL)), _const_spec((1, D_MODEL))],
        out_specs=[rowblk(D_MODEL), rowblk(D_MODEL)],
        out_shape=[jax.ShapeDtypeStruct((n, D_MODEL), F32), jax.ShapeDtypeStruct((n, D_MODEL), BF16)],
        compiler_params=_params(("parallel",), 48),
        name=name,
    )(*branches[0], *branches[1], *branches[2], proj, x, p["w_ssd_proj"], p["w_att_proj"], p["w_lru_proj"],
      p["w_out"], p["gate_bias"], p["ln1_g"], p["ln1_b"])


def _first_max(x, idx, big):
    m = jnp.max(x, axis=0, keepdims=True)
    first = jnp.min(jnp.where(x == m, idx, big), axis=0, keepdims=True)
    return m, idx == first


def _router_body(x_ref, rwt_ref, rb_ref, o_ref):
    tm = x_ref.shape[0]
    logits = _nt(rwt_ref[...], x_ref[...], HIGHEST)
    scores = _sigmoid(logits)
    sel = scores + rb_ref[...]
    sub = lax.broadcasted_iota(jnp.int32, (GROUP_SIZE, tm), 0)
    gscore = jnp.zeros((N_GROUPS, tm), F32)
    for g in range(N_GROUPS):
        sg = sel[g * GROUP_SIZE:(g + 1) * GROUP_SIZE, :]
        m1, hit = _first_max(sg, sub, GROUP_SIZE)
        m2 = jnp.max(jnp.where(hit, -jnp.inf, sg), axis=0, keepdims=True)
        gscore = jnp.where(sub == g, m1 + m2, gscore)
    gsel = jnp.zeros((N_GROUPS, tm), F32)
    for _ in range(TOPK_GROUPS):
        _, hit = _first_max(gscore, sub, N_GROUPS)
        gsel = jnp.where(hit, 1.0, gsel)
        gscore = jnp.where(hit, -jnp.inf, gscore)
    masked = jnp.concatenate(
        [jnp.where(gsel[g:g + 1, :] > 0.0, sel[g * GROUP_SIZE:(g + 1) * GROUP_SIZE, :], -jnp.inf)
         for g in range(N_GROUPS)], axis=0)
    eidx = lax.broadcasted_iota(jnp.int32, (N_EXPERTS, tm), 0)
    chosen = jnp.zeros((N_EXPERTS, tm), F32)
    for _ in range(TOP_K):
        _, hit = _first_max(masked, eidx, N_EXPERTS)
        chosen = jnp.where(hit, 1.0, chosen)
        masked = jnp.where(hit, -jnp.inf, masked)
    w = chosen * scores
    o_ref[...] = w / jnp.sum(w, axis=0, keepdims=True) * ROUTED_SCALE


def _router_call(x1, p, name):
    n = x1.shape[0]
    tm = _tile(n, (512, 256, 128))
    return pl.pallas_call(
        _router_body,
        grid=(n // tm,),
        in_specs=[pl.BlockSpec((tm, D_MODEL), lambda i: (i, 0)), _const_spec((N_EXPERTS, D_MODEL)),
                  _const_spec((N_EXPERTS, 1))],
        out_specs=pl.BlockSpec((N_EXPERTS, tm), lambda i: (0, i)),
        out_shape=jax.ShapeDtypeStruct((N_EXPERTS, n), F32),
        compiler_params=_params(("parallel",), 32),
        name=name,
    )(x1, p["router_w_t"], p["router_bias"])


def _swiglu(xb, wgu, width):
    gu = _nn(xb, wgu)
    return _silu(gu[:, :width]) * gu[:, width:]


def _moe_body(xb_ref, x_ref, g_ref, wg_ref, wu_ref, wd_ref, sgu_ref, sd_ref, lg_ref, lb_ref, o_ref, o2_ref, acc,
              *, prompt_tiles):
    j = pl.program_id(1)
    xb = xb_ref[...]
    g = pltpu.roll(g_ref[...], (LANES - j * MOE_GROUP) % LANES, 1)
    scaled = [_silu(_nn(xb, wg_ref[c])) * _nn(xb, wu_ref[c]) * g[:, c:c + 1] for c in range(MOE_GROUP)]
    part = _nn(jnp.concatenate(scaled, axis=-1).astype(BF16), wd_ref[0])

    @pl.when(j == 0)
    def _():
        acc[...] = part

    @pl.when(j > 0)
    def _():
        acc[...] += part

    @pl.when(j == pl.num_programs(1) - 1)
    def _():
        shared = _nn(_swiglu(xb, sgu_ref[...], EXPERT_HIDDEN).astype(BF16), sd_ref[...])
        x2 = _layer_norm(DN_ALPHA * x_ref[...] + (acc[...] + shared), lg_ref[...], lb_ref[...])
        if prompt_tiles is None:
            o_ref[...] = x2
            o2_ref[...] = x2.astype(BF16)
        else:
            @pl.when(pl.program_id(0) < prompt_tiles)
            def _():
                o_ref[...] = x2

            @pl.when(pl.program_id(0) >= prompt_tiles)
            def _():
                o2_ref[...] = x2


def _moe_call(x1b, x1, gates, p, name, n_prompt=None):
    n = x1.shape[0]
    tm = _tile(n if n_prompt is None else math.gcd(n_prompt, n - n_prompt), (512, 256, 128))
    width = MOE_GROUP * EXPERT_HIDDEN
    rowblk = lambda w: pl.BlockSpec((tm, w), lambda i, j: (i, 0))
    if n_prompt is None:
        pt = None
        out_specs = [rowblk(D_MODEL), rowblk(D_MODEL)]
        out_shape = [jax.ShapeDtypeStruct((n, D_MODEL), F32), jax.ShapeDtypeStruct((n, D_MODEL), BF16)]
        semantics = ("parallel", "arbitrary")
    else:
        pt = n_prompt // tm
        out_specs = [pl.BlockSpec((tm, D_MODEL), lambda i, j: (jnp.minimum(i, pt - 1), 0)),
                     pl.BlockSpec((tm, D_MODEL), lambda i, j: (jnp.maximum(i - pt, 0), 0))]
        out_shape = [jax.ShapeDtypeStruct((n_prompt, D_MODEL), F32),
                     jax.ShapeDtypeStruct((n - n_prompt, D_MODEL), F32)]
        semantics = ("arbitrary", "arbitrary")
    return pl.pallas_call(
        functools.partial(_moe_body, prompt_tiles=pt),
        grid=(n // tm, N_EXPERTS // MOE_GROUP),
        in_specs=[rowblk(D_MODEL), rowblk(D_MODEL), rowblk(LANES),
                  pl.BlockSpec((MOE_GROUP, D_MODEL, EXPERT_HIDDEN), lambda i, j: (j, 0, 0)),
                  pl.BlockSpec((MOE_GROUP, D_MODEL, EXPERT_HIDDEN), lambda i, j: (j, 0, 0)),
                  pl.BlockSpec((1, width, D_MODEL), lambda i, j: (j, 0, 0)),
                  _const_spec((D_MODEL, 2 * EXPERT_HIDDEN)), _const_spec((EXPERT_HIDDEN, D_MODEL)),
                  _const_spec((1, D_MODEL)), _const_spec((1, D_MODEL))],
        out_specs=out_specs,
        out_shape=out_shape,
        scratch_shapes=[pltpu.VMEM((tm, D_MODEL), F32)],
        compiler_params=_params(semantics, 58),
        name=name,
    )(x1b, x1, gates, p["exp_wg"], p["exp_wu"], p["exp_wd"], p["sh_wgu"], p["sh_wd"], p["ln2_g"], p["ln2_b"])


def _block_diag(w):
    k, d, _ = w.shape
    eye = jnp.eye(k, dtype=w.dtype)
    return (eye[:, None, :, None] * w[:, :, None, :]).reshape(k * d, k * d)


def _prep_layer(w, l):
    w_in = w["w_in"][l]
    o = 0
    seg = {}
    for nm, width in (("z", D_MODEL), ("xbc", SSD_CONV_DIM), ("dt", SSD_HEADS), ("q", D_MODEL), ("k", D_MODEL),
                      ("v", D_MODEL), ("xl", D_MODEL), ("yl", D_MODEL), ("g", N_BRANCH * D_MODEL)):
        seg[nm] = w_in[:, o:o + width]
        o += width
    row = lambda a: a.reshape(1, -1)
    lane_pad = lambda a: jnp.pad(a.reshape(1, -1), ((0, 0), (0, LANES - a.shape[-1])))
    head_of_channel = jnp.arange(D_MODEL) // SSD_HEAD_DIM
    return dict(
        w_main=jnp.concatenate([seg[k] for k in ("g", "z", "xbc", "xl", "yl")], axis=1).astype(BF16),
        w_qkv=jnp.concatenate([seg["q"] * ATT_SCALE, seg["k"], seg["v"]], axis=1).astype(BF16),
        w_kv=jnp.concatenate([seg["k"], seg["v"]], axis=1).astype(BF16),
        w_dt=jnp.pad(seg["dt"], ((0, 0), (0, LANES - SSD_HEADS))).astype(BF16),
        ssd_conv_w=w["ssd_conv_w"][l], ssd_conv_b=row(w["ssd_conv_b"][l]),
        ssd_dt_bias=lane_pad(w["ssd_dt_bias"][l]), ssd_a_log=lane_pad(w["ssd_a_log"][l]),
        ssd_d_ch=row(w["ssd_d"][l][head_of_channel]), ssd_norm_w=row(w["ssd_norm_w"][l]),
        ssd_expand=(jnp.arange(LANES)[:, None] == head_of_channel[None, :]).astype(F32),
        att_table=w["att_rel_bias"][l],
        lru_conv_w=w["lru_conv_w"][l], lru_conv_b=row(w["lru_conv_b"][l]),
        lru_wa_d=_block_diag(w["lru_wa"][l]).astype(BF16), lru_ba=row(w["lru_ba"][l]),
        lru_wx_d=_block_diag(w["lru_wx"][l]).astype(BF16), lru_bx=row(w["lru_bx"][l]),
        lru_lambda=row(w["lru_lambda"][l]),
        gate_bias=row(w["gate_bias"][l]),
        w_ssd_proj=w["w_ssd_proj"][l].astype(BF16), w_att_proj=w["w_att_proj"][l].astype(BF16),
        w_lru_proj=w["w_lru_proj"][l].astype(BF16), w_out=w["w_out"][l].astype(BF16),
        ln1_g=row(w["ln1_g"][l]), ln1_b=row(w["ln1_b"][l]),
        router_w_t=w["router_w"][l].T, router_bias=w["router_bias"][l].reshape(N_EXPERTS, 1),
        exp_wg=w["exp_w_gate"][l].astype(BF16), exp_wu=w["exp_w_up"][l].astype(BF16),
        exp_wd=w["exp_w_down"][l].reshape(N_EXPERTS // MOE_GROUP, MOE_GROUP * EXPERT_HIDDEN, D_MODEL).astype(BF16),
        sh_wgu=jnp.concatenate([w["sh_w_gate"][l], w["sh_w_up"][l]], axis=-1).astype(BF16),
        sh_wd=w["sh_w_down"][l].astype(BF16),
        ln2_g=row(w["ln2_g"][l]), ln2_b=row(w["ln2_b"][l]),
    )


def _heads_state(h_t, nb):
    return jnp.swapaxes(h_t, 1, 2).reshape(nb, SSD_HEADS, SSD_HEAD_DIM, SSD_STATE)


def _layer(x, xb, st, p, geom, l):
    bp, tp, bs, ts = geom
    n_p = bp * tp
    ncp = tp // CHUNK
    keep = min(ATT_LEFT, tp)
    proj = _matmul(xb, p["w_main"], 2 * D_MODEL, f"in_proj_{l}")
    qkv = _matmul(xb, p["w_qkv"], D_MODEL, f"qkv_proj_{l}", BF16)
    proj_dt = _matmul(xb, p["w_dt"], LANES, f"dt_proj_{l}")
    xb_keep = jnp.concatenate([xb[(b + 1) * tp - keep:(b + 1) * tp] for b in range(bp)] + [xb[n_p:]], axis=0)
    kv_keep = _matmul(xb_keep, p["w_kv"], D_MODEL, f"kv_keep_{l}")

    zeros = lambda *s: jnp.zeros(s, F32)
    ssd_p, pconv, ph = _ssd_call(proj, proj_dt, zeros(bp, CONV_WIDTH - 1, SSD_CONV_DIM),
                                 zeros(bp, SSD_STATE, D_MODEL), p, bp, ncp, 0, f"ssd_p_{l}")
    h0_t = jnp.swapaxes(st["state_ssd"].reshape(bs, D_MODEL, SSD_STATE), 1, 2)
    ssd_s, sconv, sh = _ssd_call(proj, proj_dt, st["cache_ssd_conv"], h0_t, p, bs, ts // CHUNK, n_p // CHUNK,
                                 f"ssd_s_{l}")

    att_rows = _tile(tp, (ATT_LEFT,))
    bias_p = _band_bias(p["att_table"], min(att_rows, ATT_SUB))
    att_p = _att_call(qkv, None, None, bias_p, bp, tp // att_rows, att_rows, 0, True, f"att_p_{l}")
    bias_s = bias_p[:, :ts, :ts + ATT_LEFT]
    att_s = _att_call(qkv, st["cache_att_k"], st["cache_att_v"], bias_s, bs, 1, ts, n_p // ts, False, f"att_s_{l}")

    lru_rows = _tile(tp, (256, 128, 64))
    lru_p, plc, plh = _lru_call(proj, zeros(bp, CONV_WIDTH - 1, D_MODEL), zeros(bp, 1, D_MODEL), p, bp,
                                tp // lru_rows, lru_rows, 0, f"lru_p_{l}")
    lru_s, slc, slh = _lru_call(proj, st["cache_lru_conv"], st["state_lru"].reshape(bs, 1, D_MODEL), p, bs,
                                1, ts, n_p // ts, f"lru_s_{l}")

    x1, x1b = _merge_call(((ssd_p, ssd_s), (att_p, att_s), (lru_p, lru_s)), proj, x, p, f"merge_{l}")
    gates_t = _router_call(x1, p, f"router_{l}")
    gates = jnp.pad(gates_t.T, ((0, 0), (0, LANES - N_EXPERTS)))
    x2, x2b = _moe_call(x1b, x1, gates, p, f"moe_{l}", n_p if l == DEPTH - 1 else None)

    def kv(col, rows0, nb, t):
        return kv_keep[rows0:rows0 + nb * t, col * D_MODEL:(col + 1) * D_MODEL].reshape(
            nb, t, ATT_HEADS, ATT_HEAD_DIM)

    states = dict(
        p_ssd_conv=pconv, s_ssd_conv=sconv, p_ssd_state=_heads_state(ph, bp), s_ssd_state=_heads_state(sh, bs),
        p_att_k=kv(0, 0, bp, keep), s_att_k=kv(0, bp * keep, bs, ts),
        p_att_v=kv(1, 0, bp, keep), s_att_v=kv(1, bp * keep, bs, ts),
        p_lru_conv=plc, s_lru_conv=slc, p_lru_state=plh.reshape(bp, D_MODEL), s_lru_state=slh.reshape(bs, D_MODEL))
    return x2, x2b, states


def kernel(x_prompt, x_sample, cache_ssd_conv, state_ssd, cache_att_k, cache_att_v, cache_lru_conv, state_lru, w_in, ssd_conv_w, ssd_conv_b, ssd_dt_bias, ssd_a_log, ssd_d, ssd_norm_w, att_rel_bias, lru_conv_w, lru_conv_b, lru_wa, lru_ba, lru_wx, lru_bx, lru_lambda, gate_bias, w_ssd_proj, w_att_proj, w_lru_proj, w_out, ln1_g, ln1_b, router_w, router_bias, exp_w_gate, exp_w_up, exp_w_down, sh_w_gate, sh_w_up, sh_w_down, ln2_g, ln2_b):
    w = dict(w_in=w_in, ssd_conv_w=ssd_conv_w, ssd_conv_b=ssd_conv_b, ssd_dt_bias=ssd_dt_bias, ssd_a_log=ssd_a_log,
             ssd_d=ssd_d, ssd_norm_w=ssd_norm_w, att_rel_bias=att_rel_bias, lru_conv_w=lru_conv_w,
             lru_conv_b=lru_conv_b, lru_wa=lru_wa, lru_ba=lru_ba, lru_wx=lru_wx, lru_bx=lru_bx,
             lru_lambda=lru_lambda, gate_bias=gate_bias, w_ssd_proj=w_ssd_proj, w_att_proj=w_att_proj,
             w_lru_proj=w_lru_proj, w_out=w_out, ln1_g=ln1_g, ln1_b=ln1_b, router_w=router_w,
             router_bias=router_bias, exp_w_gate=exp_w_gate, exp_w_up=exp_w_up, exp_w_down=exp_w_down,
             sh_w_gate=sh_w_gate, sh_w_up=sh_w_up, sh_w_down=sh_w_down, ln2_g=ln2_g, ln2_b=ln2_b)
    bp, tp, _ = x_prompt.shape
    bs, ts, _ = x_sample.shape
    n_p = bp * tp
    x = jnp.concatenate([x_prompt.reshape(n_p, D_MODEL), x_sample.reshape(bs * ts, D_MODEL)], axis=0)
    xb = x.astype(BF16)
    per_layer = []
    for l in range(DEPTH):
        st = dict(cache_ssd_conv=cache_ssd_conv[l], state_ssd=state_ssd[l], cache_att_k=cache_att_k[l],
                  cache_att_v=cache_att_v[l], cache_lru_conv=cache_lru_conv[l], state_lru=state_lru[l])
        x, xb, states = _layer(x, xb, st, _prep_layer(w, l), (bp, tp, bs, ts), l)
        per_layer.append(states)
    y_prompt, y_sample = x, xb
    stack = lambda k: jnp.stack([s[k] for s in per_layer], axis=0)
    return (y_prompt.reshape(bp, tp, D_MODEL), y_sample.reshape(bs, ts, D_MODEL),
            stack("p_ssd_conv"), stack("s_ssd_conv"), stack("p_ssd_state"), stack("s_ssd_state"),
            stack("p_att_k"), stack("s_att_k"), stack("p_att_v"), stack("s_att_v"),
            stack("p_lru_conv"), stack("s_lru_conv"), stack("p_lru_state"), stack("s_lru_state"))
```

```python
import functools
import math

import jax
import jax.numpy as jnp
import numpy as np
from jax import lax
from jax.experimental import pallas as pl
from jax.experimental.pallas import tpu as pltpu

F32 = jnp.float32
BF16 = jnp.bfloat16
HIGHEST = lax.Precision.HIGHEST

D_MODEL = 1024
DEPTH = 2
CHUNK = 64
CONV_WIDTH = 4
SSD_HEADS = 16
SSD_HEAD_DIM = 64
SSD_GROUPS = 4
SSD_STATE = 128
SSD_CONV_DIM = D_MODEL + 2 * SSD_GROUPS * SSD_STATE
ATT_HEADS = 16
ATT_HEAD_DIM = 64
ATT_LEFT = 8 * CHUNK
ATT_BAND = ATT_LEFT + CHUNK
REL_CLIP = 128
ATT_SCALE = ATT_HEAD_DIM ** -0.5
LRU_BLOCKS = 16
LRU_C = 8.0
N_BRANCH = 3
N_EXPERTS = 64
N_GROUPS = 8
GROUP_SIZE = N_EXPERTS // N_GROUPS
TOPK_GROUPS = 4
TOP_K = 8
EXPERT_HIDDEN = D_MODEL // 4
ROUTED_SCALE = 2.5
DN_ALPHA = (2.0 * DEPTH) ** 0.25
LN_EPS = 1e-5
RMS_EPS = 1e-5

LANES = 128
SUBLANES = 8
CONV_PAD = 8

COL_G, COL_Z, COL_XBC, COL_XL, COL_YL = 0, 3, 2, 6, 7
COL_Q, COL_K, COL_V = 0, 1, 2
ATT_SUB = 256
MOE_GROUP = 8
SSD_CHUNKS_PER_STEP = 4
NEG_BIG = -1e30


def _params(semantics, vmem_mb):
    return pltpu.CompilerParams(dimension_semantics=semantics, vmem_limit_bytes=vmem_mb * 2 ** 20)


def _tile(n, candidates):
    for c in candidates:
        if n % c == 0:
            return c
    raise ValueError(f"no tile for {n}")


def _nn(a, b, precision=None):
    return jnp.dot(a, b, preferred_element_type=F32, precision=precision)


def _nt(a, b, precision=None):
    return lax.dot_general(a, b, (((1,), (1,)), ((), ())), preferred_element_type=F32, precision=precision)


def _tn(a, b):
    return lax.dot_general(a, b, (((0,), (0,)), ((), ())), preferred_element_type=F32)


def _sigmoid(x):
    return 0.5 * jnp.tanh(0.5 * x) + 0.5


def _silu(x):
    return x * _sigmoid(x)


def _softplus(x):
    return jnp.maximum(x, 0.0) + jnp.log1p(jnp.exp(-jnp.abs(x)))


def _layer_norm(y, g, b):
    mu = jnp.mean(y, axis=-1, keepdims=True)
    d = y - mu
    var = jnp.mean(d * d, axis=-1, keepdims=True)
    return d * lax.rsqrt(var + LN_EPS) * g + b


def _const_spec(shape):
    nd = len(shape)
    return pl.BlockSpec(shape, lambda *_: (0,) * nd)


def _mm_body(a_ref, b_ref, o_ref):
    o_ref[...] = _nn(a_ref[...], b_ref[...]).astype(o_ref.dtype)


def _matmul(a, b, tn, name, out_dtype=F32):
    m, k = a.shape
    n = b.shape[1]
    tm = _tile(m, (1024, 512, 256, 128))
    return pl.pallas_call(
        _mm_body,
        grid=(m // tm, n // tn),
        in_specs=[pl.BlockSpec((tm, k), lambda i, j: (i, 0)), pl.BlockSpec((k, tn), lambda i, j: (0, j))],
        out_specs=pl.BlockSpec((tm, tn), lambda i, j: (i, j)),
        out_shape=jax.ShapeDtypeStruct((m, n), out_dtype),
        compiler_params=_params(("parallel", "arbitrary"), 40),
        name=name,
    )(a, b)


def _conv_block(xp_scr, x_ref, cw_ref, cb_ref, rows):
    lo = CONV_PAD - (CONV_WIDTH - 1)
    xp_scr[CONV_PAD:CONV_PAD + rows, :] = x_ref[...]
    y = cb_ref[...] + xp_scr[lo:lo + rows, :] * cw_ref[0:1, :]
    for k in range(1, CONV_WIDTH):
        y = y + xp_scr[lo + k:lo + k + rows, :] * cw_ref[k:k + 1, :]
    tail = xp_scr[lo + rows:CONV_PAD + rows, :]
    xp_scr[lo:CONV_PAD, :] = tail
    return y, tail


def _split3(x):
    hi = x.astype(BF16)
    r1 = x - hi.astype(F32)
    mid = r1.astype(BF16)
    return hi, mid, (r1 - mid.astype(F32)).astype(BF16)


def _ssd_body(z_ref, xbc_ref, dt_ref, buf0_ref, h0_ref, cw_ref, cb_ref, dtb_ref, alog_ref, dch_ref, nw_ref,
              expand_ref, y_ref, bufo_ref, ho_ref, xp_scr, h_scr, *, nc, cps):
    c = pl.program_id(1)
    lo = CONV_PAD - (CONV_WIDTH - 1)

    @pl.when(c == 0)
    def _():
        xp_scr[lo:CONV_PAD, :] = buf0_ref[0]
        h_scr[...] = h0_ref[0]

    conv, tail = _conv_block(xp_scr, xbc_ref, cw_ref, cb_ref, cps * CHUNK)
    bufo_ref[0] = tail
    xbc_all = _silu(conv)
    dt_all = _softplus(dt_ref[...] + dtb_ref[...])
    da_all = dt_all * (-jnp.exp(alog_ref[...]))
    row = lax.broadcasted_iota(jnp.int32, (CHUNK, CHUNK), 0)
    col = lax.broadcasted_iota(jnp.int32, (CHUNK, CHUNK), 1)
    causal = row >= col
    tri = causal.astype(BF16)
    eye = (lax.broadcasted_iota(jnp.int32, (LANES, LANES), 0)
           == lax.broadcasted_iota(jnp.int32, (LANES, LANES), 1)).astype(BF16)
    for ci in range(cps):
        rs = slice(ci * CHUNK, (ci + 1) * CHUNK)
        _ssd_chunk(xbc_all[rs], dt_all[rs], da_all[rs], z_ref[rs, :], tri, eye, causal, expand_ref, dch_ref,
                   nw_ref, h_scr, y_ref, rs)

    @pl.when(c == nc - 1)
    def _():
        ho_ref[0] = h_scr[...]


def _ssd_chunk(xbc, dt, da, z, tri, eye, causal, expand_ref, dch_ref, nw_ref, h_scr, y_ref, rs):
    xs = xbc[:, :D_MODEL]
    bm = xbc[:, D_MODEL:D_MODEL + SSD_GROUPS * SSD_STATE].astype(BF16)
    cm = xbc[:, D_MODEL + SSD_GROUPS * SSD_STATE:].astype(BF16)
    cs = sum(_nn(tri, piece) for piece in _split3(da))
    cs_last = cs[CHUNK - 1:CHUNK, :]
    to_end = jnp.exp(cs_last - cs)
    ecs = jnp.exp(cs)
    expand = expand_ref[...]
    wide = sum(_nn(piece, expand) for piece in _split3(jnp.concatenate([dt, to_end, ecs], axis=0)))
    dt_ch = wide[0:CHUNK]
    to_end_ch = wide[CHUNK:2 * CHUNK]
    ecs_ch = wide[2 * CHUNK:3 * CHUNK]
    dec_ch = ecs_ch[CHUNK - 1:CHUNK, :]
    cs_t = sum(_nt(eye, piece) for piece in _split3(cs))

    xdt = xs * dt_ch
    xdt_b = xdt.astype(BF16)
    xw_b = (xdt * to_end_ch).astype(BF16)
    gw = SSD_HEADS // SSD_GROUPS * SSD_HEAD_DIM
    y_parts = []
    for g in range(SSD_GROUPS):
        bg = bm[:, g * SSD_STATE:(g + 1) * SSD_STATE]
        cg = cm[:, g * SSD_STATE:(g + 1) * SSD_STATE]
        gsl = slice(g * gw, (g + 1) * gw)
        cb = _nt(cg, bg)
        h_prev = h_scr[:, gsl]
        y_off = _nn(cg, h_prev.astype(BF16)) * ecs_ch[:, gsl]
        st = _tn(bg, xw_b[:, gsl])
        h_scr[:, gsl] = h_prev * dec_ch[:, gsl] + st
        diag = []
        for r in range(SSD_HEADS // SSD_GROUPS):
            h = g * (SSD_HEADS // SSD_GROUPS) + r
            seg = cs[:, h:h + 1] - cs_t[h:h + 1, :CHUNK]
            decay = jnp.where(causal, jnp.exp(jnp.where(causal, seg, 0.0)), 0.0)
            sc = (cb * decay).astype(BF16)
            diag.append(_nn(sc, xdt_b[:, h * SSD_HEAD_DIM:(h + 1) * SSD_HEAD_DIM]))
        y_parts.append(jnp.concatenate(diag, axis=-1) + y_off)
    y = jnp.concatenate(y_parts, axis=-1) + xs * dch_ref[...]
    v = y * _silu(z)
    outs = []
    for g in range(SSD_GROUPS):
        vg = v[:, g * gw:(g + 1) * gw]
        ms = jnp.mean(vg * vg, axis=-1, keepdims=True)
        outs.append(vg * lax.rsqrt(ms + RMS_EPS))
    y_ref[rs, :] = (jnp.concatenate(outs, axis=-1) * nw_ref[...]).astype(y_ref.dtype)


def _ssd_call(proj, proj_dt, buf0, h0_t, p, nb, nchunks, blk0, name):
    cps = _tile(nchunks, (SSD_CHUNKS_PER_STEP, 1))
    nc = nchunks // cps
    rows = cps * CHUNK
    row = lambda b, c: blk0 // cps + b * nc + c
    in_specs = [
        pl.BlockSpec((rows, D_MODEL), lambda b, c: (row(b, c), COL_Z)),
        pl.BlockSpec((rows, SSD_CONV_DIM), lambda b, c: (row(b, c), COL_XBC)),
        pl.BlockSpec((rows, LANES), lambda b, c: (row(b, c), 0)),
        pl.BlockSpec((1, CONV_WIDTH - 1, SSD_CONV_DIM), lambda b, c: (b, 0, 0)),
        pl.BlockSpec((1, SSD_STATE, D_MODEL), lambda b, c: (b, 0, 0)),
        _const_spec((CONV_WIDTH, SSD_CONV_DIM)),
        _const_spec((1, SSD_CONV_DIM)),
        _const_spec((1, LANES)),
        _const_spec((1, LANES)),
        _const_spec((1, D_MODEL)),
        _const_spec((1, D_MODEL)),
        _const_spec((LANES, D_MODEL)),
    ]
    out_specs = [
        pl.BlockSpec((rows, D_MODEL), lambda b, c: (b * nc + c, 0)),
        pl.BlockSpec((1, CONV_WIDTH - 1, SSD_CONV_DIM), lambda b, c: (b, 0, 0)),
        pl.BlockSpec((1, SSD_STATE, D_MODEL), lambda b, c: (b, 0, 0)),
    ]
    out_shape = [
        jax.ShapeDtypeStruct((nb * nchunks * CHUNK, D_MODEL), BF16),
        jax.ShapeDtypeStruct((nb, CONV_WIDTH - 1, SSD_CONV_DIM), F32),
        jax.ShapeDtypeStruct((nb, SSD_STATE, D_MODEL), F32),
    ]
    return pl.pallas_call(
        functools.partial(_ssd_body, nc=nc, cps=cps),
        grid=(nb, nc),
        in_specs=in_specs,
        out_specs=out_specs,
        out_shape=out_shape,
        scratch_shapes=[pltpu.VMEM((CONV_PAD + rows, SSD_CONV_DIM), F32), pltpu.VMEM((SSD_STATE, D_MODEL), F32)],
        compiler_params=_params(("parallel", "arbitrary"), 40),
        name=name,
    )(proj, proj, proj_dt, buf0, h0_t, p["ssd_conv_w"], p["ssd_conv_b"], p["ssd_dt_bias"], p["ssd_a_log"],
      p["ssd_d_ch"], p["ssd_norm_w"], p["ssd_expand"])


def _att_body(q_ref, kp_ref, kc_ref, vp_ref, vc_ref, bias_ref, o_ref, k_scr, v_scr, *, rows, sub, mask_first):
    t = pl.program_id(1)
    if mask_first:
        k_scr[0:ATT_LEFT, :] = kp_ref[...]
        v_scr[0:ATT_LEFT, :] = vp_ref[...]
    else:
        for hp in range(ATT_HEADS // 2):
            ls = slice(hp * LANES, (hp + 1) * LANES)
            for src, dst in ((kp_ref, k_scr), (vp_ref, v_scr)):
                pair = jnp.concatenate([src[:, 2 * hp, :], src[:, 2 * hp + 1, :]], axis=-1)
                dst[0:ATT_LEFT, ls] = pair.astype(BF16)
    k_scr[ATT_LEFT:ATT_LEFT + rows, :] = kc_ref[...]
    v_scr[ATT_LEFT:ATT_LEFT + rows, :] = vc_ref[...]
    win = sub + ATT_LEFT
    first = lax.broadcasted_iota(jnp.int32, (1, LANES), 1) < ATT_HEAD_DIM
    for s in range(rows // sub):
        r0 = s * sub
        if mask_first:
            pos = lax.broadcasted_iota(jnp.int32, (1, win), 1) + r0
            dead = jnp.logical_and(pos < ATT_LEFT, t == 0)
        for hp in range(ATT_HEADS // 2):
            ls = slice(hp * LANES, (hp + 1) * LANES)
            q2 = q_ref[r0:r0 + sub, ls]
            k2 = k_scr[r0:r0 + win, ls]
            v2 = v_scr[r0:r0 + win, ls]
            zero = jnp.zeros_like(q2)
            outs = []
            for h, qh in ((2 * hp, jnp.where(first, q2, zero)), (2 * hp + 1, jnp.where(first, zero, q2))):
                sc = _nt(qh, k2) + bias_ref[h]
                if mask_first:
                    sc = jnp.where(dead, NEG_BIG, sc)
                m = jnp.max(sc, axis=-1, keepdims=True)
                e = jnp.exp(sc - m)
                den = jnp.sum(e, axis=-1, keepdims=True)
                outs.append(_nn(e.astype(BF16), v2) / den)
            o_ref[r0:r0 + sub, ls] = jnp.where(first, outs[0], outs[1]).astype(o_ref.dtype)


def _att_call(qkv, k_prev, v_prev, bias, nb, nt, rows, blk0, prompt, name, layer=0):
    sub = min(rows, ATT_SUB)
    cur = lambda col: pl.BlockSpec((rows, D_MODEL), lambda b, t: (blk0 + b * nt + t, col))
    if prompt:
        prev = lambda col: pl.BlockSpec((ATT_LEFT, D_MODEL), lambda b, t: (b * nt + jnp.maximum(t - 1, 0), col))
        kp_spec, vp_spec = prev(COL_K), prev(COL_V)
        k_prev = v_prev = qkv
    else:
        kp_spec = vp_spec = pl.BlockSpec((None, None, ATT_LEFT, ATT_HEADS, ATT_HEAD_DIM),
                                         lambda b, t: (layer, b, 0, 0, 0))
    in_specs = [cur(COL_Q), kp_spec, cur(COL_K), vp_spec, cur(COL_V),
                _const_spec((ATT_HEADS, sub, sub + ATT_LEFT))]
    return pl.pallas_call(
        functools.partial(_att_body, rows=rows, sub=sub, mask_first=prompt),
        grid=(nb, nt),
        in_specs=in_specs,
        out_specs=pl.BlockSpec((rows, D_MODEL), lambda b, t: (b * nt + t, 0)),
        out_shape=jax.ShapeDtypeStruct((nb * nt * rows, D_MODEL), BF16),
        scratch_shapes=[pltpu.VMEM((ATT_LEFT + rows, D_MODEL), BF16), pltpu.VMEM((ATT_LEFT + rows, D_MODEL), BF16)],
        compiler_params=_params(("parallel", "arbitrary"), 56),
        name=name,
    )(qkv, k_prev, qkv, v_prev, qkv, bias)


def _band_bias(table, sub):
    win = sub + ATT_LEFT
    span = -(-(sub - 1 + win) // LANES) * LANES
    idx = [min(max(win - 1 - m, -REL_CLIP), REL_CLIP) + REL_CLIP for m in range(span)]
    vec = table[:, np.array(idx, np.int32)].reshape(ATT_HEADS, 1, span)
    return pl.pallas_call(
        functools.partial(_band_bias_body, sub=sub, win=win),
        grid=(ATT_HEADS,),
        in_specs=[pl.BlockSpec((1, 1, span), lambda h: (h, 0, 0))],
        out_specs=pl.BlockSpec((1, sub, win), lambda h: (h, 0, 0)),
        out_shape=jax.ShapeDtypeStruct((ATT_HEADS, sub, win), F32),
        compiler_params=_params(("parallel",), 32),
        name="band_bias",
    )(vec)


def _band_bias_body(vec_ref, o_ref, *, sub, win):
    span = vec_ref.shape[-1]
    toep = pltpu.roll(jnp.broadcast_to(vec_ref[0], (sub, span)), span - (sub - 1), 1, stride=1, stride_axis=0)
    r = lax.broadcasted_iota(jnp.int32, (sub, win), 0)
    j = lax.broadcasted_iota(jnp.int32, (sub, win), 1) - (r - jnp.bitwise_and(r, CHUNK - 1))
    o_ref[0] = jnp.where(jnp.logical_and(j >= 0, j < ATT_BAND), toep[:, :win], NEG_BIG)


def _lru_scan(a, u, carry):
    rows, width = a.shape
    a3 = a.reshape(rows // SUBLANES, SUBLANES, width)
    u3 = u.reshape(rows // SUBLANES, SUBLANES, width)
    pos = lax.broadcasted_iota(jnp.int32, (1, SUBLANES, 1), 1)
    d = 1
    while d < SUBLANES:
        a_sh = pltpu.roll(a3, d, 1)
        u_sh = pltpu.roll(u3, d, 1)
        m = pos >= d
        u3 = jnp.where(m, a3 * u_sh + u3, u3)
        a3 = jnp.where(m, a3 * a_sh, a3)
        d *= 2
    out = []
    for g in range(rows // SUBLANES):
        h = a3[g] * carry + u3[g]
        carry = h[SUBLANES - 1:SUBLANES, :]
        out.append(h)
    return jnp.concatenate(out, axis=0), carry


def _gelu_tanh(x):
    return 0.5 * x * (1.0 + jnp.tanh(math.sqrt(2.0 / math.pi) * (x + 0.044715 * (x * x * x))))


def _lru_body(xl_ref, yl_ref, buf0_ref, h0_ref, cw_ref, cb_ref, wa_ref, ba_ref, wx_ref, bx_ref, lam_ref,
              o_ref, bufo_ref, ho_ref, xp_scr, carry_scr, *, rows):
    c = pl.program_id(1)
    lo = CONV_PAD - (CONV_WIDTH - 1)

    @pl.when(c == 0)
    def _():
        xp_scr[lo:CONV_PAD, :] = buf0_ref[0]
        carry_scr[...] = h0_ref[0]

    xc, tail = _conv_block(xp_scr, xl_ref, cw_ref, cb_ref, rows)
    bufo_ref[0] = tail
    xcb = xc.astype(BF16)
    gate_r = _sigmoid(_nn(xcb, wa_ref[...]) + ba_ref[...])
    gate_i = _sigmoid(_nn(xcb, wx_ref[...]) + bx_ref[...])
    log_a = LRU_C * gate_r * (-_softplus(-lam_ref[...]))
    a = jnp.exp(log_a)
    u = jnp.sqrt(1.0 - a * a) * (gate_i * xc)
    h, carry = _lru_scan(a, u, carry_scr[...])
    o_ref[...] = (h * _gelu_tanh(yl_ref[...])).astype(o_ref.dtype)
    carry_scr[...] = carry
    ho_ref[0] = carry


def _lru_call(proj, buf0, h0, p, nb, nc, rows, blk0, name):
    row = lambda b, c: blk0 + b * nc + c
    in_specs = [
        pl.BlockSpec((rows, D_MODEL), lambda b, c: (row(b, c), COL_XL)),
        pl.BlockSpec((rows, D_MODEL), lambda b, c: (row(b, c), COL_YL)),
        pl.BlockSpec((1, CONV_WIDTH - 1, D_MODEL), lambda b, c: (b, 0, 0)),
        pl.BlockSpec((1, 1, D_MODEL), lambda b, c: (b, 0, 0)),
        _const_spec((CONV_WIDTH, D_MODEL)),
        _const_spec((1, D_MODEL)),
        _const_spec((D_MODEL, D_MODEL)),
        _const_spec((1, D_MODEL)),
        _const_spec((D_MODEL, D_MODEL)),
        _const_spec((1, D_MODEL)),
        _const_spec((1, D_MODEL)),
    ]
    out_specs = [
        pl.BlockSpec((rows, D_MODEL), lambda b, c: (b * nc + c, 0)),
        pl.BlockSpec((1, CONV_WIDTH - 1, D_MODEL), lambda b, c: (b, 0, 0)),
        pl.BlockSpec((1, 1, D_MODEL), lambda b, c: (b, 0, 0)),
    ]
    out_shape = [
        jax.ShapeDtypeStruct((nb * nc * rows, D_MODEL), BF16),
        jax.ShapeDtypeStruct((nb, CONV_WIDTH - 1, D_MODEL), F32),
        jax.ShapeDtypeStruct((nb, 1, D_MODEL), F32),
    ]
    return pl.pallas_call(
        functools.partial(_lru_body, rows=rows),
        grid=(nb, nc),
        in_specs=in_specs,
        out_specs=out_specs,
        out_shape=out_shape,
        scratch_shapes=[pltpu.VMEM((CONV_PAD + rows, D_MODEL), F32), pltpu.VMEM((1, D_MODEL), F32)],
        compiler_params=_params(("parallel", "arbitrary"), 48),
        name=name,
    )(proj, proj, buf0, h0, p["lru_conv_w"], p["lru_conv_b"], p["lru_wa_d"], p["lru_ba"], p["lru_wx_d"],
      p["lru_bx"], p["lru_lambda"])


def _merge_body(sp_ref, ss_ref, ap_ref, as_ref, lp_ref, ls_ref, g_ref, x_ref, ws_ref, wa_ref, wl_ref, wo_ref,
                gb_ref, lg_ref, lb_ref, o_ref, ob_ref, *, prompt_tiles):
    in_prompt = pl.program_id(0) < prompt_tiles
    branch = lambda p_ref, s_ref: jnp.where(in_prompt, p_ref[...], s_ref[...])
    g = _sigmoid(g_ref[...] + gb_ref[...])
    m = (g[:, 0:D_MODEL] * _nn(branch(sp_ref, ss_ref), ws_ref[...])
         + g[:, D_MODEL:2 * D_MODEL] * _nn(branch(ap_ref, as_ref), wa_ref[...])
         + g[:, 2 * D_MODEL:3 * D_MODEL] * _nn(branch(lp_ref, ls_ref), wl_ref[...]))
    y = DN_ALPHA * x_ref[...] + _nn(m.astype(BF16), wo_ref[...])
    x1 = _layer_norm(y, lg_ref[...], lb_ref[...])
    o_ref[...] = x1
    ob_ref[...] = x1.astype(BF16)


def _merge_call(branches, proj, x, p, name):
    n = x.shape[0]
    n_p = branches[0][0].shape[0]
    tm = _tile(math.gcd(n_p, n - n_p), (256, 128))
    pt = n_p // tm
    rowblk = lambda w: pl.BlockSpec((tm, w), lambda i: (i, 0))
    pblk = pl.BlockSpec((tm, D_MODEL), lambda i: (jnp.minimum(i, pt - 1), 0))
    sblk = pl.BlockSpec((tm, D_MODEL), lambda i: (jnp.maximum(i - pt, 0), 0))
    wspec = _const_spec((D_MODEL, D_MODEL))
    return pl.pallas_call(
        functools.partial(_merge_body, prompt_tiles=pt),
        grid=(n // tm,),
        in_specs=[pblk, sblk, pblk, sblk, pblk, sblk,
                  pl.BlockSpec((tm, N_BRANCH * D_MODEL), lambda i: (i, COL_G)), rowblk(D_MODEL),
                  wspec, wspec, wspec, wspec,
                  _const_spec((1, N_BRANCH * D_MODEL)), _const_spec((1, D_MODEL)), _const_spec((1, D_MODEL))],
        out_specs=[rowblk(D_MODEL), rowblk(D_MODEL)],
        out_shape=[jax.ShapeDtypeStruct((n, D_MODEL), F32), jax.ShapeDtypeStruct((n, D_MODEL), BF16)],
        compiler_params=_params(("parallel",), 48),
        name=name,
    )(*branches[0], *branches[1], *branches[2], proj, x, p["w_ssd_proj"], p["w_att_proj"], p["w_lru_proj"],
      p["w_out"], p["gate_bias"], p["ln1_g"], p["ln1_b"])


def _first_max(x, idx, big):
    m = jnp.max(x, axis=0, keepdims=True)
    first = jnp.min(jnp.where(x == m, idx, big), axis=0, keepdims=True)
    return m, idx == first


def _router_body(x_ref, rwt_ref, rb_ref, o_ref):
    tm = x_ref.shape[0]
    logits = _nt(rwt_ref[...], x_ref[...], HIGHEST)
    scores = _sigmoid(logits)
    sel = scores + rb_ref[...]
    sub = lax.broadcasted_iota(jnp.int32, (GROUP_SIZE, tm), 0)
    gscore = jnp.zeros((N_GROUPS, tm), F32)
    for g in range(N_GROUPS):
        sg = sel[g * GROUP_SIZE:(g + 1) * GROUP_SIZE, :]
        m1, hit = _first_max(sg, sub, GROUP_SIZE)
        m2 = jnp.max(jnp.where(hit, -jnp.inf, sg), axis=0, keepdims=True)
        gscore = jnp.where(sub == g, m1 + m2, gscore)
    gsel = jnp.zeros((N_GROUPS, tm), F32)
    for _ in range(TOPK_GROUPS):
        _, hit = _first_max(gscore, sub, N_GROUPS)
        gsel = jnp.where(hit, 1.0, gsel)
        gscore = jnp.where(hit, -jnp.inf, gscore)
    masked = jnp.concatenate(
        [jnp.where(gsel[g:g + 1, :] > 0.0, sel[g * GROUP_SIZE:(g + 1) * GROUP_SIZE, :], -jnp.inf)
         for g in range(N_GROUPS)], axis=0)
    eidx = lax.broadcasted_iota(jnp.int32, (N_EXPERTS, tm), 0)
    chosen = jnp.zeros((N_EXPERTS, tm), F32)
    for _ in range(TOP_K):
        _, hit = _first_max(masked, eidx, N_EXPERTS)
        chosen = jnp.where(hit, 1.0, chosen)
        masked = jnp.where(hit, -jnp.inf, masked)
    w = chosen * scores
    o_ref[...] = w / jnp.sum(w, axis=0, keepdims=True) * ROUTED_SCALE


def _router_call(x1, p, name):
    n = x1.shape[0]
    tm = _tile(n, (512, 256, 128))
    return pl.pallas_call(
        _router_body,
        grid=(n // tm,),
        in_specs=[pl.BlockSpec((tm, D_MODEL), lambda i: (i, 0)), _const_spec((N_EXPERTS, D_MODEL)),
                  _const_spec((N_EXPERTS, 1))],
        out_specs=pl.BlockSpec((N_EXPERTS, tm), lambda i: (0, i)),
        out_shape=jax.ShapeDtypeStruct((N_EXPERTS, n), F32),
        compiler_params=_params(("parallel",), 32),
        name=name,
    )(x1, p["router_w_t"], p["router_bias"])


def _swiglu(xb, wgu, width):
    gu = _nn(xb, wgu)
    return _silu(gu[:, :width]) * gu[:, width:]


def _moe_body(xb_ref, x_ref, g_ref, wg_ref, wu_ref, wd_ref, sgu_ref, sd_ref, lg_ref, lb_ref, o_ref, o2_ref, acc,
              *, prompt_tiles):
    j = pl.program_id(1)
    xb = xb_ref[...]
    g = pltpu.roll(g_ref[...], (LANES - j * MOE_GROUP) % LANES, 1)
    scaled = [_silu(_nn(xb, wg_ref[c])) * _nn(xb, wu_ref[c]) * g[:, c:c + 1] for c in range(MOE_GROUP)]
    part = _nn(jnp.concatenate(scaled, axis=-1).astype(BF16), wd_ref[0])

    @pl.when(j == 0)
    def _():
        acc[...] = part

    @pl.when(j > 0)
    def _():
        acc[...] += part

    @pl.when(j == pl.num_programs(1) - 1)
    def _():
        shared = _nn(_swiglu(xb, sgu_ref[...], EXPERT_HIDDEN).astype(BF16), sd_ref[...])
        x2 = _layer_norm(DN_ALPHA * x_ref[...] + (acc[...] + shared), lg_ref[...], lb_ref[...])
        if prompt_tiles is None:
            o_ref[...] = x2
            o2_ref[...] = x2.astype(BF16)
        else:
            @pl.when(pl.program_id(0) < prompt_tiles)
            def _():
                o_ref[...] = x2

            @pl.when(pl.program_id(0) >= prompt_tiles)
            def _():
                o2_ref[...] = x2


def _moe_call(x1b, x1, gates, p, name, n_prompt=None):
    n = x1.shape[0]
    layer = p["layer"]
    tm = _tile(n if n_prompt is None else math.gcd(n_prompt, n - n_prompt), (512, 256, 128))
    width = MOE_GROUP * EXPERT_HIDDEN
    rowblk = lambda w: pl.BlockSpec((tm, w), lambda i, j: (i, 0))
    if n_prompt is None:
        pt = None
        out_specs = [rowblk(D_MODEL), rowblk(D_MODEL)]
        out_shape = [jax.ShapeDtypeStruct((n, D_MODEL), F32), jax.ShapeDtypeStruct((n, D_MODEL), BF16)]
        semantics = ("parallel", "arbitrary")
    else:
        pt = n_prompt // tm
        out_specs = [pl.BlockSpec((tm, D_MODEL), lambda i, j: (jnp.minimum(i, pt - 1), 0)),
                     pl.BlockSpec((tm, D_MODEL), lambda i, j: (jnp.maximum(i - pt, 0), 0))]
        out_shape = [jax.ShapeDtypeStruct((n_prompt, D_MODEL), F32),
                     jax.ShapeDtypeStruct((n - n_prompt, D_MODEL), F32)]
        semantics = ("arbitrary", "arbitrary")
    return pl.pallas_call(
        functools.partial(_moe_body, prompt_tiles=pt),
        grid=(n // tm, N_EXPERTS // MOE_GROUP),
        in_specs=[rowblk(D_MODEL), rowblk(D_MODEL), rowblk(LANES),
                  pl.BlockSpec((None, MOE_GROUP, D_MODEL, EXPERT_HIDDEN), lambda i, j: (layer, j, 0, 0)),
                  pl.BlockSpec((None, MOE_GROUP, D_MODEL, EXPERT_HIDDEN), lambda i, j: (layer, j, 0, 0)),
                  pl.BlockSpec((None, 1, width, D_MODEL), lambda i, j: (layer, j, 0, 0)),
                  _const_spec((D_MODEL, 2 * EXPERT_HIDDEN)), _const_spec((EXPERT_HIDDEN, D_MODEL)),
                  _const_spec((1, D_MODEL)), _const_spec((1, D_MODEL))],
        out_specs=out_specs,
        out_shape=out_shape,
        scratch_shapes=[pltpu.VMEM((tm, D_MODEL), F32)],
        compiler_params=_params(semantics, 58),
        name=name,
    )(x1b, x1, gates, p["exp_wg"], p["exp_wu"], p["exp_wd"], p["sh_wgu"], p["sh_wd"], p["ln2_g"], p["ln2_b"])


def _block_diag(w):
    k, d, _ = w.shape
    eye = jnp.eye(k, dtype=w.dtype)
    return (eye[:, None, :, None] * w[:, :, None, :]).reshape(k * d, k * d)


def _prep_stacks(w):
    depth = w["exp_w_down"].shape[0]
    return dict(
        w_in_b=w["w_in"].astype(BF16),
        exp_wg_all=w["exp_w_gate"].astype(BF16), exp_wu_all=w["exp_w_up"].astype(BF16),
        exp_wd_all=w["exp_w_down"].reshape(depth, N_EXPERTS // MOE_GROUP, MOE_GROUP * EXPERT_HIDDEN,
                                           D_MODEL).astype(BF16))


def _prep_layer(w, l):
    w_in = w["w_in_b"][l]
    o = 0
    seg = {}
    for nm, width in (("z", D_MODEL), ("xbc", SSD_CONV_DIM), ("dt", SSD_HEADS), ("q", D_MODEL), ("k", D_MODEL),
                      ("v", D_MODEL), ("xl", D_MODEL), ("yl", D_MODEL), ("g", N_BRANCH * D_MODEL)):
        seg[nm] = w_in[:, o:o + width]
        o += width
    row = lambda a: a.reshape(1, -1)
    lane_pad = lambda a: jnp.pad(a.reshape(1, -1), ((0, 0), (0, LANES - a.shape[-1])))
    head_of_channel = jnp.arange(D_MODEL) // SSD_HEAD_DIM
    return dict(
        w_main=jnp.concatenate([seg[k] for k in ("g", "z", "xbc", "xl", "yl")], axis=1).astype(BF16),
        w_qkv=jnp.concatenate([seg["q"] * ATT_SCALE, seg["k"], seg["v"]], axis=1).astype(BF16),
        w_kv=jnp.concatenate([seg["k"], seg["v"]], axis=1).astype(BF16),
        w_dt=jnp.pad(seg["dt"], ((0, 0), (0, LANES - SSD_HEADS))).astype(BF16),
        ssd_conv_w=w["ssd_conv_w"][l], ssd_conv_b=row(w["ssd_conv_b"][l]),
        ssd_dt_bias=lane_pad(w["ssd_dt_bias"][l]), ssd_a_log=lane_pad(w["ssd_a_log"][l]),
        ssd_d_ch=row(w["ssd_d"][l][head_of_channel]), ssd_norm_w=row(w["ssd_norm_w"][l]),
        ssd_expand=(jnp.arange(LANES)[:, None] == head_of_channel[None, :]).astype(F32),
        att_table=w["att_rel_bias"][l],
        lru_conv_w=w["lru_conv_w"][l], lru_conv_b=row(w["lru_conv_b"][l]),
        lru_wa_d=_block_diag(w["lru_wa"][l]).astype(BF16), lru_ba=row(w["lru_ba"][l]),
        lru_wx_d=_block_diag(w["lru_wx"][l]).astype(BF16), lru_bx=row(w["lru_bx"][l]),
        lru_lambda=row(w["lru_lambda"][l]),
        gate_bias=row(w["gate_bias"][l]),
        w_ssd_proj=w["w_ssd_proj"][l].astype(BF16), w_att_proj=w["w_att_proj"][l].astype(BF16),
        w_lru_proj=w["w_lru_proj"][l].astype(BF16), w_out=w["w_out"][l].astype(BF16),
        ln1_g=row(w["ln1_g"][l]), ln1_b=row(w["ln1_b"][l]),
        router_w_t=w["router_w"][l].T, router_bias=w["router_bias"][l].reshape(N_EXPERTS, 1),
        layer=l, exp_wg=w["exp_wg_all"], exp_wu=w["exp_wu_all"], exp_wd=w["exp_wd_all"],
        sh_wgu=jnp.concatenate([w["sh_w_gate"][l], w["sh_w_up"][l]], axis=-1).astype(BF16),
        sh_wd=w["sh_w_down"][l].astype(BF16),
        ln2_g=row(w["ln2_g"][l]), ln2_b=row(w["ln2_b"][l]),
    )


def _heads_state(h_t, nb):
    return jnp.swapaxes(h_t, 1, 2).reshape(nb, SSD_HEADS, SSD_HEAD_DIM, SSD_STATE)


def _layer(x, xb, st, p, geom, l):
    bp, tp, bs, ts = geom
    n_p = bp * tp
    ncp = tp // CHUNK
    keep = min(ATT_LEFT, tp)
    proj = _matmul(xb, p["w_main"], 2 * D_MODEL, f"in_proj_{l}")
    qkv = _matmul(xb, p["w_qkv"], D_MODEL, f"qkv_proj_{l}", BF16)
    proj_dt = _matmul(xb, p["w_dt"], LANES, f"dt_proj_{l}")
    xb_keep = jnp.concatenate([xb[(b + 1) * tp - keep:(b + 1) * tp] for b in range(bp)] + [xb[n_p:]], axis=0)
    kv_keep = _matmul(xb_keep, p["w_kv"], D_MODEL, f"kv_keep_{l}")

    zeros = lambda *s: jnp.zeros(s, F32)
    ssd_p, pconv, ph = _ssd_call(proj, proj_dt, zeros(bp, CONV_WIDTH - 1, SSD_CONV_DIM),
                                 zeros(bp, SSD_STATE, D_MODEL), p, bp, ncp, 0, f"ssd_p_{l}")
    h0_t = jnp.swapaxes(st["state_ssd"].reshape(bs, D_MODEL, SSD_STATE), 1, 2)
    ssd_s, sconv, sh = _ssd_call(proj, proj_dt, st["cache_ssd_conv"], h0_t, p, bs, ts // CHUNK, n_p // CHUNK,
                                 f"ssd_s_{l}")

    att_rows = _tile(tp, (ATT_LEFT,))
    bias_p = _band_bias(p["att_table"], min(att_rows, ATT_SUB))
    att_p = _att_call(qkv, None, None, bias_p, bp, tp // att_rows, att_rows, 0, True, f"att_p_{l}")
    bias_s = bias_p[:, :ts, :ts + ATT_LEFT]
    att_s = _att_call(qkv, st["cache_att_k"], st["cache_att_v"], bias_s, bs, 1, ts, n_p // ts, False,
                      f"att_s_{l}", layer=l)

    lru_rows = _tile(tp, (256, 128, 64))
    lru_p, plc, plh = _lru_call(proj, zeros(bp, CONV_WIDTH - 1, D_MODEL), zeros(bp, 1, D_MODEL), p, bp,
                                tp // lru_rows, lru_rows, 0, f"lru_p_{l}")
    lru_s, slc, slh = _lru_call(proj, st["cache_lru_conv"], st["state_lru"].reshape(bs, 1, D_MODEL), p, bs,
                                1, ts, n_p // ts, f"lru_s_{l}")

    x1, x1b = _merge_call(((ssd_p, ssd_s), (att_p, att_s), (lru_p, lru_s)), proj, x, p, f"merge_{l}")
    gates_t = _router_call(x1, p, f"router_{l}")
    gates = jnp.pad(gates_t.T, ((0, 0), (0, LANES - N_EXPERTS)))
    x2, x2b = _moe_call(x1b, x1, gates, p, f"moe_{l}", n_p if l == DEPTH - 1 else None)

    def kv(col, rows0, nb, t):
        return kv_keep[rows0:rows0 + nb * t, col * D_MODEL:(col + 1) * D_MODEL].reshape(
            nb, t, ATT_HEADS, ATT_HEAD_DIM)

    states = dict(
        p_ssd_conv=pconv, s_ssd_conv=sconv, p_ssd_state=_heads_state(ph, bp), s_ssd_state=_heads_state(sh, bs),
        p_att_k=kv(0, 0, bp, keep), s_att_k=kv(0, bp * keep, bs, ts),
        p_att_v=kv(1, 0, bp, keep), s_att_v=kv(1, bp * keep, bs, ts),
        p_lru_conv=plc, s_lru_conv=slc, p_lru_state=plh.reshape(bp, D_MODEL), s_lru_state=slh.reshape(bs, D_MODEL))
    return x2, x2b, states


def kernel(x_prompt, x_sample, cache_ssd_conv, state_ssd, cache_att_k, cache_att_v, cache_lru_conv, state_lru, w_in, ssd_conv_w, ssd_conv_b, ssd_dt_bias, ssd_a_log, ssd_d, ssd_norm_w, att_rel_bias, lru_conv_w, lru_conv_b, lru_wa, lru_ba, lru_wx, lru_bx, lru_lambda, gate_bias, w_ssd_proj, w_att_proj, w_lru_proj, w_out, ln1_g, ln1_b, router_w, router_bias, exp_w_gate, exp_w_up, exp_w_down, sh_w_gate, sh_w_up, sh_w_down, ln2_g, ln2_b):
    w = dict(w_in=w_in, ssd_conv_w=ssd_conv_w, ssd_conv_b=ssd_conv_b, ssd_dt_bias=ssd_dt_bias, ssd_a_log=ssd_a_log,
             ssd_d=ssd_d, ssd_norm_w=ssd_norm_w, att_rel_bias=att_rel_bias, lru_conv_w=lru_conv_w,
             lru_conv_b=lru_conv_b, lru_wa=lru_wa, lru_ba=lru_ba, lru_wx=lru_wx, lru_bx=lru_bx,
             lru_lambda=lru_lambda, gate_bias=gate_bias, w_ssd_proj=w_ssd_proj, w_att_proj=w_att_proj,
             w_lru_proj=w_lru_proj, w_out=w_out, ln1_g=ln1_g, ln1_b=ln1_b, router_w=router_w,
             router_bias=router_bias, exp_w_gate=exp_w_gate, exp_w_up=exp_w_up, exp_w_down=exp_w_down,
             sh_w_gate=sh_w_gate, sh_w_up=sh_w_up, sh_w_down=sh_w_down, ln2_g=ln2_g, ln2_b=ln2_b)
    bp, tp, _ = x_prompt.shape
    bs, ts, _ = x_sample.shape
    n_p = bp * tp
    x = jnp.concatenate([x_prompt.reshape(n_p, D_MODEL), x_sample.reshape(bs * ts, D_MODEL)], axis=0)
    xb = x.astype(BF16)
    w.update(_prep_stacks(w))
    per_layer = []
    for l in range(DEPTH):
        st = dict(cache_ssd_conv=cache_ssd_conv[l], state_ssd=state_ssd[l], cache_att_k=cache_att_k,
                  cache_att_v=cache_att_v, cache_lru_conv=cache_lru_conv[l], state_lru=state_lru[l])
        x, xb, states = _layer(x, xb, st, _prep_layer(w, l), (bp, tp, bs, ts), l)
        per_layer.append(states)
    y_prompt, y_sample = x, xb
    stack = lambda k: jnp.stack([s[k] for s in per_layer], axis=0)
    return (y_prompt.reshape(bp, tp, D_MODEL), y_sample.reshape(bs, ts, D_MODEL),
            stack("p_ssd_conv"), stack("s_ssd_conv"), stack("p_ssd_state"), stack("s_ssd_state"),
            stack("p_att_k"), stack("s_att_k"), stack("p_att_v"), stack("s_att_v"),
            stack("p_lru_conv"), stack("s_lru_conv"), stack("p_lru_state"), stack("s_lru_state"))
```

```python
import functools
import math

import jax
import jax.numpy as jnp
import numpy as np
from jax import lax
from jax.experimental import pallas as pl
from jax.experimental.pallas import tpu as pltpu

F32 = jnp.float32
BF16 = jnp.bfloat16
HIGHEST = lax.Precision.HIGHEST

D_MODEL = 1024
DEPTH = 2
CHUNK = 64
CONV_WIDTH = 4
SSD_HEADS = 16
SSD_HEAD_DIM = 64
SSD_GROUPS = 4
SSD_STATE = 128
SSD_CONV_DIM = D_MODEL + 2 * SSD_GROUPS * SSD_STATE
ATT_HEADS = 16
ATT_HEAD_DIM = 64
ATT_LEFT = 8 * CHUNK
ATT_BAND = ATT_LEFT + CHUNK
REL_CLIP = 128
ATT_SCALE = ATT_HEAD_DIM ** -0.5
LRU_BLOCKS = 16
LRU_C = 8.0
N_BRANCH = 3
N_EXPERTS = 64
N_GROUPS = 8
GROUP_SIZE = N_EXPERTS // N_GROUPS
TOPK_GROUPS = 4
TOP_K = 8
EXPERT_HIDDEN = D_MODEL // 4
ROUTED_SCALE = 2.5
DN_ALPHA = (2.0 * DEPTH) ** 0.25
LN_EPS = 1e-5
RMS_EPS = 1e-5

LANES = 128
SUBLANES = 8
CONV_PAD = 8

COL_G, COL_Z, COL_XBC, COL_XL, COL_YL = 0, 3, 2, 6, 7
COL_Q, COL_K, COL_V = 0, 1, 2
ATT_SUB = 256
MOE_GROUP = 8
SSD_CHUNKS_PER_STEP = 4
NEG_BIG = -1e30


def _params(semantics, vmem_mb):
    return pltpu.CompilerParams(dimension_semantics=semantics, vmem_limit_bytes=vmem_mb * 2 ** 20)


def _tile(n, candidates):
    for c in candidates:
        if n % c == 0:
            return c
    raise ValueError(f"no tile for {n}")


def _nn(a, b, precision=None):
    return jnp.dot(a, b, preferred_element_type=F32, precision=precision)


def _nt(a, b, precision=None):
    return lax.dot_general(a, b, (((1,), (1,)), ((), ())), preferred_element_type=F32, precision=precision)


def _tn(a, b):
    return lax.dot_general(a, b, (((0,), (0,)), ((), ())), preferred_element_type=F32)


def _sigmoid(x):
    return 0.5 * jnp.tanh(0.5 * x) + 0.5


def _silu(x):
    return x * _sigmoid(x)


def _softplus(x):
    return jnp.maximum(x, 0.0) + jnp.log1p(jnp.exp(-jnp.abs(x)))


def _layer_norm(y, g, b):
    mu = jnp.mean(y, axis=-1, keepdims=True)
    d = y - mu
    var = jnp.mean(d * d, axis=-1, keepdims=True)
    return d * lax.rsqrt(var + LN_EPS) * g + b


def _const_spec(shape):
    nd = len(shape)
    return pl.BlockSpec(shape, lambda *_: (0,) * nd)


def _mm_body(a_ref, b_ref, o_ref):
    o_ref[...] = _nn(a_ref[...], b_ref[...]).astype(o_ref.dtype)


def _matmul(a, b, tn, name, out_dtype=F32):
    m, k = a.shape
    n = b.shape[1]
    tm = _tile(m, (1024, 512, 256, 128))
    return pl.pallas_call(
        _mm_body,
        grid=(m // tm, n // tn),
        in_specs=[pl.BlockSpec((tm, k), lambda i, j: (i, 0)), pl.BlockSpec((k, tn), lambda i, j: (0, j))],
        out_specs=pl.BlockSpec((tm, tn), lambda i, j: (i, j)),
        out_shape=jax.ShapeDtypeStruct((m, n), out_dtype),
        compiler_params=_params(("parallel", "arbitrary"), 40),
        name=name,
    )(a, b)


def _conv_block(xp_scr, x_ref, cw_ref, cb_ref, rows):
    lo = CONV_PAD - (CONV_WIDTH - 1)
    xp_scr[CONV_PAD:CONV_PAD + rows, :] = x_ref[...]
    y = cb_ref[...] + xp_scr[lo:lo + rows, :] * cw_ref[0:1, :]
    for k in range(1, CONV_WIDTH):
        y = y + xp_scr[lo + k:lo + k + rows, :] * cw_ref[k:k + 1, :]
    tail = xp_scr[lo + rows:CONV_PAD + rows, :]
    xp_scr[lo:CONV_PAD, :] = tail
    return y, tail


def _split3(x):
    hi = x.astype(BF16)
    r1 = x - hi.astype(F32)
    mid = r1.astype(BF16)
    return hi, mid, (r1 - mid.astype(F32)).astype(BF16)


def _ssd_body(z_ref, xbc_ref, dt_ref, buf0_ref, h0_ref, cw_ref, cb_ref, dtb_ref, alog_ref, dch_ref, nw_ref,
              expand_ref, y_ref, bufo_ref, ho_ref, xp_scr, h_scr, *, nc, cps):
    c = pl.program_id(1)
    lo = CONV_PAD - (CONV_WIDTH - 1)

    @pl.when(c == 0)
    def _():
        xp_scr[lo:CONV_PAD, :] = buf0_ref[0]
        h_scr[...] = h0_ref[0]

    conv, tail = _conv_block(xp_scr, xbc_ref, cw_ref, cb_ref, cps * CHUNK)
    bufo_ref[0] = tail
    xbc_all = _silu(conv)
    dt_all = _softplus(dt_ref[...] + dtb_ref[...])
    da_all = dt_all * (-jnp.exp(alog_ref[...]))
    row = lax.broadcasted_iota(jnp.int32, (CHUNK, CHUNK), 0)
    col = lax.broadcasted_iota(jnp.int32, (CHUNK, CHUNK), 1)
    causal = row >= col
    tri = causal.astype(BF16)
    eye = (lax.broadcasted_iota(jnp.int32, (LANES, LANES), 0)
           == lax.broadcasted_iota(jnp.int32, (LANES, LANES), 1)).astype(BF16)
    for ci in range(cps):
        rs = slice(ci * CHUNK, (ci + 1) * CHUNK)
        _ssd_chunk(xbc_all[rs], dt_all[rs], da_all[rs], z_ref[rs, :], tri, eye, causal, expand_ref, dch_ref,
                   nw_ref, h_scr, y_ref, rs)

    @pl.when(c == nc - 1)
    def _():
        ho_ref[0] = h_scr[...]


def _ssd_chunk(xbc, dt, da, z, tri, eye, causal, expand_ref, dch_ref, nw_ref, h_scr, y_ref, rs):
    xs = xbc[:, :D_MODEL]
    bm = xbc[:, D_MODEL:D_MODEL + SSD_GROUPS * SSD_STATE].astype(BF16)
    cm = xbc[:, D_MODEL + SSD_GROUPS * SSD_STATE:].astype(BF16)
    cs = sum(_nn(tri, piece) for piece in _split3(da))
    cs_last = cs[CHUNK - 1:CHUNK, :]
    to_end = jnp.exp(cs_last - cs)
    ecs = jnp.exp(cs)
    expand = expand_ref[...]
    wide = sum(_nn(piece, expand) for piece in _split3(jnp.concatenate([dt, to_end, ecs], axis=0)))
    dt_ch = wide[0:CHUNK]
    to_end_ch = wide[CHUNK:2 * CHUNK]
    ecs_ch = wide[2 * CHUNK:3 * CHUNK]
    dec_ch = ecs_ch[CHUNK - 1:CHUNK, :]
    cs_t = sum(_nt(eye, piece) for piece in _split3(cs))

    xdt = xs * dt_ch
    xdt_b = xdt.astype(BF16)
    xw_b = (xdt * to_end_ch).astype(BF16)
    gw = SSD_HEADS // SSD_GROUPS * SSD_HEAD_DIM
    y_parts = []
    for g in range(SSD_GROUPS):
        bg = bm[:, g * SSD_STATE:(g + 1) * SSD_STATE]
        cg = cm[:, g * SSD_STATE:(g + 1) * SSD_STATE]
        gsl = slice(g * gw, (g + 1) * gw)
        cb = _nt(cg, bg)
        h_prev = h_scr[:, gsl]
        y_off = _nn(cg, h_prev.astype(BF16)) * ecs_ch[:, gsl]
        st = _tn(bg, xw_b[:, gsl])
        h_scr[:, gsl] = h_prev * dec_ch[:, gsl] + st
        diag = []
        for r in range(SSD_HEADS // SSD_GROUPS):
            h = g * (SSD_HEADS // SSD_GROUPS) + r
            seg = cs[:, h:h + 1] - cs_t[h:h + 1, :CHUNK]
            decay = jnp.where(causal, jnp.exp(jnp.where(causal, seg, 0.0)), 0.0)
            sc = (cb * decay).astype(BF16)
            diag.append(_nn(sc, xdt_b[:, h * SSD_HEAD_DIM:(h + 1) * SSD_HEAD_DIM]))
        y_parts.append(jnp.concatenate(diag, axis=-1) + y_off)
    y = jnp.concatenate(y_parts, axis=-1) + xs * dch_ref[...]
    v = y * _silu(z)
    outs = []
    for g in range(SSD_GROUPS):
        vg = v[:, g * gw:(g + 1) * gw]
        ms = jnp.mean(vg * vg, axis=-1, keepdims=True)
        outs.append(vg * lax.rsqrt(ms + RMS_EPS))
    y_ref[rs, :] = (jnp.concatenate(outs, axis=-1) * nw_ref[...]).astype(y_ref.dtype)


def _ssd_call(proj, proj_dt, buf0, h0_t, p, nb, nchunks, blk0, name):
    cps = _tile(nchunks, (SSD_CHUNKS_PER_STEP, 1))
    nc = nchunks // cps
    rows = cps * CHUNK
    row = lambda b, c: blk0 // cps + b * nc + c
    in_specs = [
        pl.BlockSpec((rows, D_MODEL), lambda b, c: (row(b, c), COL_Z)),
        pl.BlockSpec((rows, SSD_CONV_DIM), lambda b, c: (row(b, c), COL_XBC)),
        pl.BlockSpec((rows, LANES), lambda b, c: (row(b, c), 0)),
        pl.BlockSpec((1, CONV_WIDTH - 1, SSD_CONV_DIM), lambda b, c: (b, 0, 0)),
        pl.BlockSpec((1, SSD_STATE, D_MODEL), lambda b, c: (b, 0, 0)),
        _const_spec((CONV_WIDTH, SSD_CONV_DIM)),
        _const_spec((1, SSD_CONV_DIM)),
        _const_spec((1, LANES)),
        _const_spec((1, LANES)),
        _const_spec((1, D_MODEL)),
        _const_spec((1, D_MODEL)),
        _const_spec((LANES, D_MODEL)),
    ]
    out_specs = [
        pl.BlockSpec((rows, D_MODEL), lambda b, c: (b * nc + c, 0)),
        pl.BlockSpec((1, CONV_WIDTH - 1, SSD_CONV_DIM), lambda b, c: (b, 0, 0)),
        pl.BlockSpec((1, SSD_STATE, D_MODEL), lambda b, c: (b, 0, 0)),
    ]
    out_shape = [
        jax.ShapeDtypeStruct((nb * nchunks * CHUNK, D_MODEL), BF16),
        jax.ShapeDtypeStruct((nb, CONV_WIDTH - 1, SSD_CONV_DIM), F32),
        jax.ShapeDtypeStruct((nb, SSD_STATE, D_MODEL), F32),
    ]
    return pl.pallas_call(
        functools.partial(_ssd_body, nc=nc, cps=cps),
        grid=(nb, nc),
        in_specs=in_specs,
        out_specs=out_specs,
        out_shape=out_shape,
        scratch_shapes=[pltpu.VMEM((CONV_PAD + rows, SSD_CONV_DIM), F32), pltpu.VMEM((SSD_STATE, D_MODEL), F32)],
        compiler_params=_params(("parallel", "arbitrary"), 40),
        name=name,
    )(proj, proj, proj_dt, buf0, h0_t, p["ssd_conv_w"], p["ssd_conv_b"], p["ssd_dt_bias"], p["ssd_a_log"],
      p["ssd_d_ch"], p["ssd_norm_w"], p["ssd_expand"])


def _head_pair_queries(q2):
    first = lax.broadcasted_iota(jnp.int32, (1, LANES), 1) < ATT_HEAD_DIM
    zero = jnp.zeros_like(q2)
    return first, (jnp.where(first, q2, zero), jnp.where(first, zero, q2))


def _att_prompt_body(q_ref, kp_ref, kc_ref, vp_ref, vc_ref, bias_ref, o_ref, k_scr, v_scr, *, rows, sub):
    t = pl.program_id(1)
    k_scr[0:ATT_LEFT, :] = kp_ref[...]
    v_scr[0:ATT_LEFT, :] = vp_ref[...]
    k_scr[ATT_LEFT:ATT_LEFT + rows, :] = kc_ref[...]
    v_scr[ATT_LEFT:ATT_LEFT + rows, :] = vc_ref[...]
    win = sub + ATT_LEFT
    for s in range(rows // sub):
        r0 = s * sub
        pos = lax.broadcasted_iota(jnp.int32, (1, win), 1) + r0
        dead = jnp.logical_and(pos < ATT_LEFT, t == 0)
        for hp in range(ATT_HEADS // 2):
            ls = slice(hp * LANES, (hp + 1) * LANES)
            first, queries = _head_pair_queries(q_ref[r0:r0 + sub, ls])
            k2 = k_scr[r0:r0 + win, ls]
            v2 = v_scr[r0:r0 + win, ls]
            outs = []
            for h, qh in zip((2 * hp, 2 * hp + 1), queries):
                sc = jnp.where(dead, NEG_BIG, _nt(qh, k2) + bias_ref[h])
                m = jnp.max(sc, axis=-1, keepdims=True)
                e = jnp.exp(sc - m)
                den = jnp.sum(e, axis=-1, keepdims=True)
                outs.append(_nn(e.astype(BF16), v2) / den)
            o_ref[r0:r0 + sub, ls] = jnp.where(first, outs[0], outs[1]).astype(o_ref.dtype)


def _att_prompt_call(qkv, bias, nb, nt, rows, name):
    sub = min(rows, ATT_SUB)
    cur = lambda col: pl.BlockSpec((rows, D_MODEL), lambda b, t: (b * nt + t, col))
    prev = lambda col: pl.BlockSpec((ATT_LEFT, D_MODEL), lambda b, t: (b * nt + jnp.maximum(t - 1, 0), col))
    return pl.pallas_call(
        functools.partial(_att_prompt_body, rows=rows, sub=sub),
        grid=(nb, nt),
        in_specs=[cur(COL_Q), prev(COL_K), cur(COL_K), prev(COL_V), cur(COL_V),
                  _const_spec((ATT_HEADS, sub, sub + ATT_LEFT))],
        out_specs=pl.BlockSpec((rows, D_MODEL), lambda b, t: (b * nt + t, 0)),
        out_shape=jax.ShapeDtypeStruct((nb * nt * rows, D_MODEL), BF16),
        scratch_shapes=[pltpu.VMEM((ATT_LEFT + rows, D_MODEL), BF16), pltpu.VMEM((ATT_LEFT + rows, D_MODEL), BF16)],
        compiler_params=_params(("parallel", "arbitrary"), 56),
        name=name,
    )(qkv, qkv, qkv, qkv, qkv, bias)


def _att_sample_body(q_ref, kn_ref, vn_ref, kt_ref, vt_ref, bias_ref, o_ref):
    for hp in range(ATT_HEADS // 2):
        ls = slice(hp * LANES, (hp + 1) * LANES)
        first, queries = _head_pair_queries(q_ref[:, ls])
        kn2 = kn_ref[:, ls]
        vn2 = vn_ref[:, ls]
        kt2 = jnp.concatenate([kt_ref[2 * hp], kt_ref[2 * hp + 1]], axis=0).astype(BF16)
        vt2 = jnp.concatenate([vt_ref[2 * hp], vt_ref[2 * hp + 1]], axis=0).astype(BF16)
        outs = []
        for h, qh in zip((2 * hp, 2 * hp + 1), queries):
            bias = bias_ref[h]
            s_old = _nn(qh, kt2) + bias[:, :ATT_LEFT]
            s_new = _nt(qh, kn2) + bias[:, ATT_LEFT:]
            m = jnp.maximum(jnp.max(s_old, axis=-1, keepdims=True), jnp.max(s_new, axis=-1, keepdims=True))
            e_old = jnp.exp(s_old - m)
            e_new = jnp.exp(s_new - m)
            den = jnp.sum(e_old, axis=-1, keepdims=True) + jnp.sum(e_new, axis=-1, keepdims=True)
            outs.append((_nt(e_old.astype(BF16), vt2) + _nn(e_new.astype(BF16), vn2)) / den)
        o_ref[:, ls] = jnp.where(first, outs[0], outs[1]).astype(o_ref.dtype)


def _att_sample_call(qkv, k_cache_t, v_cache_t, bias, nb, rows, blk0, layer, name):
    cur = lambda col: pl.BlockSpec((rows, D_MODEL), lambda b: (blk0 + b, col))
    cache = pl.BlockSpec((None, None, ATT_HEADS, ATT_HEAD_DIM, ATT_LEFT), lambda b: (layer, b, 0, 0, 0))
    return pl.pallas_call(
        _att_sample_body,
        grid=(nb,),
        in_specs=[cur(COL_Q), cur(COL_K), cur(COL_V), cache, cache, _const_spec((ATT_HEADS, rows, rows + ATT_LEFT))],
        out_specs=pl.BlockSpec((rows, D_MODEL), lambda b: (b, 0)),
        out_shape=jax.ShapeDtypeStruct((nb * rows, D_MODEL), BF16),
        compiler_params=_params(("parallel",), 40),
        name=name,
    )(qkv, qkv, qkv, k_cache_t, v_cache_t, bias)


def _band_bias(table, sub):
    win = sub + ATT_LEFT
    span = -(-(sub - 1 + win) // LANES) * LANES
    idx = [min(max(win - 1 - m, -REL_CLIP), REL_CLIP) + REL_CLIP for m in range(span)]
    vec = table[:, np.array(idx, np.int32)].reshape(ATT_HEADS, 1, span)
    return pl.pallas_call(
        functools.partial(_band_bias_body, sub=sub, win=win),
        grid=(ATT_HEADS,),
        in_specs=[pl.BlockSpec((1, 1, span), lambda h: (h, 0, 0))],
        out_specs=pl.BlockSpec((1, sub, win), lambda h: (h, 0, 0)),
        out_shape=jax.ShapeDtypeStruct((ATT_HEADS, sub, win), F32),
        compiler_params=_params(("parallel",), 32),
        name="band_bias",
    )(vec)


def _band_bias_body(vec_ref, o_ref, *, sub, win):
    span = vec_ref.shape[-1]
    toep = pltpu.roll(jnp.broadcast_to(vec_ref[0], (sub, span)), span - (sub - 1), 1, stride=1, stride_axis=0)
    r = lax.broadcasted_iota(jnp.int32, (sub, win), 0)
    j = lax.broadcasted_iota(jnp.int32, (sub, win), 1) - (r - jnp.bitwise_and(r, CHUNK - 1))
    o_ref[0] = jnp.where(jnp.logical_and(j >= 0, j < ATT_BAND), toep[:, :win], NEG_BIG)


def _lru_scan(a, u, carry):
    rows, width = a.shape
    a3 = a.reshape(rows // SUBLANES, SUBLANES, width)
    u3 = u.reshape(rows // SUBLANES, SUBLANES, width)
    pos = lax.broadcasted_iota(jnp.int32, (1, SUBLANES, 1), 1)
    d = 1
    while d < SUBLANES:
        a_sh = pltpu.roll(a3, d, 1)
        u_sh = pltpu.roll(u3, d, 1)
        m = pos >= d
        u3 = jnp.where(m, a3 * u_sh + u3, u3)
        a3 = jnp.where(m, a3 * a_sh, a3)
        d *= 2
    out = []
    for g in range(rows // SUBLANES):
        h = a3[g] * carry + u3[g]
        carry = h[SUBLANES - 1:SUBLANES, :]
        out.append(h)
    return jnp.concatenate(out, axis=0), carry


def _gelu_tanh(x):
    return 0.5 * x * (1.0 + jnp.tanh(math.sqrt(2.0 / math.pi) * (x + 0.044715 * (x * x * x))))


def _lru_body(xl_ref, yl_ref, buf0_ref, h0_ref, cw_ref, cb_ref, wa_ref, ba_ref, wx_ref, bx_ref, lam_ref,
              o_ref, bufo_ref, ho_ref, xp_scr, carry_scr, *, rows):
    c = pl.program_id(1)
    lo = CONV_PAD - (CONV_WIDTH - 1)

    @pl.when(c == 0)
    def _():
        xp_scr[lo:CONV_PAD, :] = buf0_ref[0]
        carry_scr[...] = h0_ref[0]

    xc, tail = _conv_block(xp_scr, xl_ref, cw_ref, cb_ref, rows)
    bufo_ref[0] = tail
    xcb = xc.astype(BF16)
    gate_r = _sigmoid(_nn(xcb, wa_ref[...]) + ba_ref[...])
    gate_i = _sigmoid(_nn(xcb, wx_ref[...]) + bx_ref[...])
    log_a = LRU_C * gate_r * (-_softplus(-lam_ref[...]))
    a = jnp.exp(log_a)
    u = jnp.sqrt(1.0 - a * a) * (gate_i * xc)
    h, carry = _lru_scan(a, u, carry_scr[...])
    o_ref[...] = (h * _gelu_tanh(yl_ref[...])).astype(o_ref.dtype)
    carry_scr[...] = carry
    ho_ref[0] = carry


def _lru_call(proj, buf0, h0, p, nb, nc, rows, blk0, name):
    row = lambda b, c: blk0 + b * nc + c
    in_specs = [
        pl.BlockSpec((rows, D_MODEL), lambda b, c: (row(b, c), COL_XL)),
        pl.BlockSpec((rows, D_MODEL), lambda b, c: (row(b, c), COL_YL)),
        pl.BlockSpec((1, CONV_WIDTH - 1, D_MODEL), lambda b, c: (b, 0, 0)),
        pl.BlockSpec((1, 1, D_MODEL), lambda b, c: (b, 0, 0)),
        _const_spec((CONV_WIDTH, D_MODEL)),
        _const_spec((1, D_MODEL)),
        _const_spec((D_MODEL, D_MODEL)),
        _const_spec((1, D_MODEL)),
        _const_spec((D_MODEL, D_MODEL)),
        _const_spec((1, D_MODEL)),
        _const_spec((1, D_MODEL)),
    ]
    out_specs = [
        pl.BlockSpec((rows, D_MODEL), lambda b, c: (b * nc + c, 0)),
        pl.BlockSpec((1, CONV_WIDTH - 1, D_MODEL), lambda b, c: (b, 0, 0)),
        pl.BlockSpec((1, 1, D_MODEL), lambda b, c: (b, 0, 0)),
    ]
    out_shape = [
        jax.ShapeDtypeStruct((nb * nc * rows, D_MODEL), BF16),
        jax.ShapeDtypeStruct((nb, CONV_WIDTH - 1, D_MODEL), F32),
        jax.ShapeDtypeStruct((nb, 1, D_MODEL), F32),
    ]
    return pl.pallas_call(
        functools.partial(_lru_body, rows=rows),
        grid=(nb, nc),
        in_specs=in_specs,
        out_specs=out_specs,
        out_shape=out_shape,
        scratch_shapes=[pltpu.VMEM((CONV_PAD + rows, D_MODEL), F32), pltpu.VMEM((1, D_MODEL), F32)],
        compiler_params=_params(("parallel", "arbitrary"), 48),
        name=name,
    )(proj, proj, buf0, h0, p["lru_conv_w"], p["lru_conv_b"], p["lru_wa_d"], p["lru_ba"], p["lru_wx_d"],
      p["lru_bx"], p["lru_lambda"])


def _merge_body(sp_ref, ss_ref, ap_ref, as_ref, lp_ref, ls_ref, g_ref, x_ref, ws_ref, wa_ref, wl_ref, wo_ref,
                gb_ref, lg_ref, lb_ref, o_ref, ob_ref, *, prompt_tiles):
    in_prompt = pl.program_id(0) < prompt_tiles
    branch = lambda p_ref, s_ref: jnp.where(in_prompt, p_ref[...], s_ref[...])
    g = _sigmoid(g_ref[...] + gb_ref[...])
    m = (g[:, 0:D_MODEL] * _nn(branch(sp_ref, ss_ref), ws_ref[...])
         + g[:, D_MODEL:2 * D_MODEL] * _nn(branch(ap_ref, as_ref), wa_ref[...])
         + g[:, 2 * D_MODEL:3 * D_MODEL] * _nn(branch(lp_ref, ls_ref), wl_ref[...]))
    y = DN_ALPHA * x_ref[...] + _nn(m.astype(BF16), wo_ref[...])
    x1 = _layer_norm(y, lg_ref[...], lb_ref[...])
    o_ref[...] = x1
    ob_ref[...] = x1.astype(BF16)


def _merge_call(branches, proj, x, p, name):
    n = x.shape[0]
    n_p = branches[0][0].shape[0]
    tm = _tile(math.gcd(n_p, n - n_p), (256, 128))
    pt = n_p // tm
    rowblk = lambda w: pl.BlockSpec((tm, w), lambda i: (i, 0))
    pblk = pl.BlockSpec((tm, D_MODEL), lambda i: (jnp.minimum(i, pt - 1), 0))
    sblk = pl.BlockSpec((tm, D_MODEL), lambda i: (jnp.maximum(i - pt, 0), 0))
    wspec = _const_spec((D_MODEL, D_MODEL))
    return pl.pallas_call(
        functools.partial(_merge_body, prompt_tiles=pt),
        grid=(n // tm,),
        in_specs=[pblk, sblk, pblk, sblk, pblk, sblk,
                  pl.BlockSpec((tm, N_BRANCH * D_MODEL), lambda i: (i, COL_G)), rowblk(D_MODEL),
                  wspec, wspec, wspec, wspec,
                  _const_spec((1, N_BRANCH * D_MODEL)), _const_spec((1, D_MODEL)), _const_spec((1, D_MODEL))],
        out_specs=[rowblk(D_MODEL), rowblk(D_MODEL)],
        out_shape=[jax.ShapeDtypeStruct((n, D_MODEL), F32), jax.ShapeDtypeStruct((n, D_MODEL), BF16)],
        compiler_params=_params(("parallel",), 48),
        name=name,
    )(*branches[0], *branches[1], *branches[2], proj, x, p["w_ssd_proj"], p["w_att_proj"], p["w_lru_proj"],
      p["w_out"], p["gate_bias"], p["ln1_g"], p["ln1_b"])


def _first_max(x, idx, big):
    m = jnp.max(x, axis=0, keepdims=True)
    first = jnp.min(jnp.where(x == m, idx, big), axis=0, keepdims=True)
    return m, idx == first


def _router_body(x_ref, rwt_ref, rb_ref, o_ref):
    tm = x_ref.shape[0]
    logits = _nt(rwt_ref[...], x_ref[...], HIGHEST)
    scores = _sigmoid(logits)
    sel = scores + rb_ref[...]
    sub = lax.broadcasted_iota(jnp.int32, (GROUP_SIZE, tm), 0)
    gscore = jnp.zeros((N_GROUPS, tm), F32)
    for g in range(N_GROUPS):
        sg = sel[g * GROUP_SIZE:(g + 1) * GROUP_SIZE, :]
        m1, hit = _first_max(sg, sub, GROUP_SIZE)
        m2 = jnp.max(jnp.where(hit, -jnp.inf, sg), axis=0, keepdims=True)
        gscore = jnp.where(sub == g, m1 + m2, gscore)
    gsel = jnp.zeros((N_GROUPS, tm), F32)
    for _ in range(TOPK_GROUPS):
        _, hit = _first_max(gscore, sub, N_GROUPS)
        gsel = jnp.where(hit, 1.0, gsel)
        gscore = jnp.where(hit, -jnp.inf, gscore)
    masked = jnp.concatenate(
        [jnp.where(gsel[g:g + 1, :] > 0.0, sel[g * GROUP_SIZE:(g + 1) * GROUP_SIZE, :], -jnp.inf)
         for g in range(N_GROUPS)], axis=0)
    eidx = lax.broadcasted_iota(jnp.int32, (N_EXPERTS, tm), 0)
    chosen = jnp.zeros((N_EXPERTS, tm), F32)
    for _ in range(TOP_K):
        _, hit = _first_max(masked, eidx, N_EXPERTS)
        chosen = jnp.where(hit, 1.0, chosen)
        masked = jnp.where(hit, -jnp.inf, masked)
    w = chosen * scores
    o_ref[...] = w / jnp.sum(w, axis=0, keepdims=True) * ROUTED_SCALE


def _router_call(x1, p, name):
    n = x1.shape[0]
    tm = _tile(n, (512, 256, 128))
    return pl.pallas_call(
        _router_body,
        grid=(n // tm,),
        in_specs=[pl.BlockSpec((tm, D_MODEL), lambda i: (i, 0)), _const_spec((N_EXPERTS, D_MODEL)),
                  _const_spec((N_EXPERTS, 1))],
        out_specs=pl.BlockSpec((N_EXPERTS, tm), lambda i: (0, i)),
        out_shape=jax.ShapeDtypeStruct((N_EXPERTS, n), F32),
        compiler_params=_params(("parallel",), 32),
        name=name,
    )(x1, p["router_w_t"], p["router_bias"])


def _swiglu(xb, wgu, width):
    gu = _nn(xb, wgu)
    return _silu(gu[:, :width]) * gu[:, width:]


def _moe_body(xb_ref, x_ref, g_ref, wg_ref, wu_ref, wd_ref, sgu_ref, sd_ref, lg_ref, lb_ref, o_ref, o2_ref, acc,
              *, prompt_tiles):
    j = pl.program_id(1)
    xb = xb_ref[...]
    g = pltpu.roll(g_ref[...], (LANES - j * MOE_GROUP) % LANES, 1)
    scaled = [_silu(_nn(xb, wg_ref[c])) * _nn(xb, wu_ref[c]) * g[:, c:c + 1] for c in range(MOE_GROUP)]
    part = _nn(jnp.concatenate(scaled, axis=-1).astype(BF16), wd_ref[0])

    @pl.when(j == 0)
    def _():
        acc[...] = part

    @pl.when(j > 0)
    def _():
        acc[...] += part

    @pl.when(j == pl.num_programs(1) - 1)
    def _():
        shared = _nn(_swiglu(xb, sgu_ref[...], EXPERT_HIDDEN).astype(BF16), sd_ref[...])
        x2 = _layer_norm(DN_ALPHA * x_ref[...] + (acc[...] + shared), lg_ref[...], lb_ref[...])
        if prompt_tiles is None:
            o_ref[...] = x2
            o2_ref[...] = x2.astype(BF16)
        else:
            @pl.when(pl.program_id(0) < prompt_tiles)
            def _():
                o_ref[...] = x2

            @pl.when(pl.program_id(0) >= prompt_tiles)
            def _():
                o2_ref[...] = x2


def _moe_call(x1b, x1, gates, p, name, n_prompt=None):
    n = x1.shape[0]
    layer = p["layer"]
    tm = _tile(n if n_prompt is None else math.gcd(n_prompt, n - n_prompt), (512, 256, 128))
    width = MOE_GROUP * EXPERT_HIDDEN
    rowblk = lambda w: pl.BlockSpec((tm, w), lambda i, j: (i, 0))
    if n_prompt is None:
        pt = None
        out_specs = [rowblk(D_MODEL), rowblk(D_MODEL)]
        out_shape = [jax.ShapeDtypeStruct((n, D_MODEL), F32), jax.ShapeDtypeStruct((n, D_MODEL), BF16)]
        semantics = ("parallel", "arbitrary")
    else:
        pt = n_prompt // tm
        out_specs = [pl.BlockSpec((tm, D_MODEL), lambda i, j: (jnp.minimum(i, pt - 1), 0)),
                     pl.BlockSpec((tm, D_MODEL), lambda i, j: (jnp.maximum(i - pt, 0), 0))]
        out_shape = [jax.ShapeDtypeStruct((n_prompt, D_MODEL), F32),
                     jax.ShapeDtypeStruct((n - n_prompt, D_MODEL), F32)]
        semantics = ("arbitrary", "arbitrary")
    return pl.pallas_call(
        functools.partial(_moe_body, prompt_tiles=pt),
        grid=(n // tm, N_EXPERTS // MOE_GROUP),
        in_specs=[rowblk(D_MODEL), rowblk(D_MODEL), rowblk(LANES),
                  pl.BlockSpec((None, MOE_GROUP, D_MODEL, EXPERT_HIDDEN), lambda i, j: (layer, j, 0, 0)),
                  pl.BlockSpec((None, MOE_GROUP, D_MODEL, EXPERT_HIDDEN), lambda i, j: (layer, j, 0, 0)),
                  pl.BlockSpec((None, 1, width, D_MODEL), lambda i, j: (layer, j, 0, 0)),
                  _const_spec((D_MODEL, 2 * EXPERT_HIDDEN)), _const_spec((EXPERT_HIDDEN, D_MODEL)),
                  _const_spec((1, D_MODEL)), _const_spec((1, D_MODEL))],
        out_specs=out_specs,
        out_shape=out_shape,
        scratch_shapes=[pltpu.VMEM((tm, D_MODEL), F32)],
        compiler_params=_params(semantics, 58),
        name=name,
    )(x1b, x1, gates, p["exp_wg"], p["exp_wu"], p["exp_wd"], p["sh_wgu"], p["sh_wd"], p["ln2_g"], p["ln2_b"])


def _block_diag(w):
    k, d, _ = w.shape
    eye = jnp.eye(k, dtype=w.dtype)
    return (eye[:, None, :, None] * w[:, :, None, :]).reshape(k * d, k * d)


def _prep_stacks(w):
    depth = w["exp_w_down"].shape[0]
    return dict(
        w_in_b=w["w_in"].astype(BF16),
        exp_wg_all=w["exp_w_gate"].astype(BF16), exp_wu_all=w["exp_w_up"].astype(BF16),
        exp_wd_all=w["exp_w_down"].reshape(depth, N_EXPERTS // MOE_GROUP, MOE_GROUP * EXPERT_HIDDEN,
                                           D_MODEL).astype(BF16))


def _prep_layer(w, l):
    w_in = w["w_in_b"][l]
    o = 0
    seg = {}
    for nm, width in (("z", D_MODEL), ("xbc", SSD_CONV_DIM), ("dt", SSD_HEADS), ("q", D_MODEL), ("k", D_MODEL),
                      ("v", D_MODEL), ("xl", D_MODEL), ("yl", D_MODEL), ("g", N_BRANCH * D_MODEL)):
        seg[nm] = w_in[:, o:o + width]
        o += width
    row = lambda a: a.reshape(1, -1)
    lane_pad = lambda a: jnp.pad(a.reshape(1, -1), ((0, 0), (0, LANES - a.shape[-1])))
    head_of_channel = jnp.arange(D_MODEL) // SSD_HEAD_DIM
    return dict(
        w_main=jnp.concatenate([seg[k] for k in ("g", "z", "xbc", "xl", "yl")], axis=1).astype(BF16),
        w_qkv=jnp.concatenate([seg["q"] * ATT_SCALE, seg["k"], seg["v"]], axis=1).astype(BF16),
        w_kv=jnp.concatenate([seg["k"], seg["v"]], axis=1).astype(BF16),
        w_dt=jnp.pad(seg["dt"], ((0, 0), (0, LANES - SSD_HEADS))).astype(BF16),
        ssd_conv_w=w["ssd_conv_w"][l], ssd_conv_b=row(w["ssd_conv_b"][l]),
        ssd_dt_bias=lane_pad(w["ssd_dt_bias"][l]), ssd_a_log=lane_pad(w["ssd_a_log"][l]),
        ssd_d_ch=row(w["ssd_d"][l][head_of_channel]), ssd_norm_w=row(w["ssd_norm_w"][l]),
        ssd_expand=(jnp.arange(LANES)[:, None] == head_of_channel[None, :]).astype(F32),
        att_table=w["att_rel_bias"][l],
        lru_conv_w=w["lru_conv_w"][l], lru_conv_b=row(w["lru_conv_b"][l]),
        lru_wa_d=_block_diag(w["lru_wa"][l]).astype(BF16), lru_ba=row(w["lru_ba"][l]),
        lru_wx_d=_block_diag(w["lru_wx"][l]).astype(BF16), lru_bx=row(w["lru_bx"][l]),
        lru_lambda=row(w["lru_lambda"][l]),
        gate_bias=row(w["gate_bias"][l]),
        w_ssd_proj=w["w_ssd_proj"][l].astype(BF16), w_att_proj=w["w_att_proj"][l].astype(BF16),
        w_lru_proj=w["w_lru_proj"][l].astype(BF16), w_out=w["w_out"][l].astype(BF16),
        ln1_g=row(w["ln1_g"][l]), ln1_b=row(w["ln1_b"][l]),
        router_w_t=w["router_w"][l].T, router_bias=w["router_bias"][l].reshape(N_EXPERTS, 1),
        layer=l, exp_wg=w["exp_wg_all"], exp_wu=w["exp_wu_all"], exp_wd=w["exp_wd_all"],
        sh_wgu=jnp.concatenate([w["sh_w_gate"][l], w["sh_w_up"][l]], axis=-1).astype(BF16),
        sh_wd=w["sh_w_down"][l].astype(BF16),
        ln2_g=row(w["ln2_g"][l]), ln2_b=row(w["ln2_b"][l]),
    )


def _heads_state(h_t, nb):
    return jnp.swapaxes(h_t, 1, 2).reshape(nb, SSD_HEADS, SSD_HEAD_DIM, SSD_STATE)


def _layer(x, xb, st, p, geom, l):
    bp, tp, bs, ts = geom
    n_p = bp * tp
    ncp = tp // CHUNK
    keep = min(ATT_LEFT, tp)
    proj = _matmul(xb, p["w_main"], 2 * D_MODEL, f"in_proj_{l}")
    qkv = _matmul(xb, p["w_qkv"], D_MODEL, f"qkv_proj_{l}", BF16)
    proj_dt = _matmul(xb, p["w_dt"], LANES, f"dt_proj_{l}")
    xb_keep = jnp.concatenate([xb[(b + 1) * tp - keep:(b + 1) * tp] for b in range(bp)] + [xb[n_p:]], axis=0)
    kv_keep = _matmul(xb_keep, p["w_kv"], D_MODEL, f"kv_keep_{l}")

    zeros = lambda *s: jnp.zeros(s, F32)
    ssd_p, pconv, ph = _ssd_call(proj, proj_dt, zeros(bp, CONV_WIDTH - 1, SSD_CONV_DIM),
                                 zeros(bp, SSD_STATE, D_MODEL), p, bp, ncp, 0, f"ssd_p_{l}")
    h0_t = jnp.swapaxes(st["state_ssd"].reshape(bs, D_MODEL, SSD_STATE), 1, 2)
    ssd_s, sconv, sh = _ssd_call(proj, proj_dt, st["cache_ssd_conv"], h0_t, p, bs, ts // CHUNK, n_p // CHUNK,
                                 f"ssd_s_{l}")

    att_rows = _tile(tp, (ATT_LEFT,))
    bias_p = _band_bias(p["att_table"], min(att_rows, ATT_SUB))
    att_p = _att_prompt_call(qkv, bias_p, bp, tp // att_rows, att_rows, f"att_p_{l}")
    bias_s = bias_p[:, :ts, :ts + ATT_LEFT]
    att_s = _att_sample_call(qkv, st["cache_att_kt"], st["cache_att_vt"], bias_s, bs, ts, n_p // ts, l,
                             f"att_s_{l}")

    lru_rows = _tile(tp, (256, 128, 64))
    lru_p, plc, plh = _lru_call(proj, zeros(bp, CONV_WIDTH - 1, D_MODEL), zeros(bp, 1, D_MODEL), p, bp,
                                tp // lru_rows, lru_rows, 0, f"lru_p_{l}")
    lru_s, slc, slh = _lru_call(proj, st["cache_lru_conv"], st["state_lru"].reshape(bs, 1, D_MODEL), p, bs,
                                1, ts, n_p // ts, f"lru_s_{l}")

    x1, x1b = _merge_call(((ssd_p, ssd_s), (att_p, att_s), (lru_p, lru_s)), proj, x, p, f"merge_{l}")
    gates_t = _router_call(x1, p, f"router_{l}")
    gates = jnp.pad(gates_t.T, ((0, 0), (0, LANES - N_EXPERTS)))
    x2, x2b = _moe_call(x1b, x1, gates, p, f"moe_{l}", n_p if l == DEPTH - 1 else None)

    def kv(col, rows0, nb, t):
        return kv_keep[rows0:rows0 + nb * t, col * D_MODEL:(col + 1) * D_MODEL].reshape(
            nb, t, ATT_HEADS, ATT_HEAD_DIM)

    states = dict(
        p_ssd_conv=pconv, s_ssd_conv=sconv, p_ssd_state=_heads_state(ph, bp), s_ssd_state=_heads_state(sh, bs),
        p_att_k=kv(0, 0, bp, keep), s_att_k=kv(0, bp * keep, bs, ts),
        p_att_v=kv(1, 0, bp, keep), s_att_v=kv(1, bp * keep, bs, ts),
        p_lru_conv=plc, s_lru_conv=slc, p_lru_state=plh.reshape(bp, D_MODEL), s_lru_state=slh.reshape(bs, D_MODEL))
    return x2, x2b, states


def kernel(x_prompt, x_sample, cache_ssd_conv, state_ssd, cache_att_k, cache_att_v, cache_lru_conv, state_lru, w_in, ssd_conv_w, ssd_conv_b, ssd_dt_bias, ssd_a_log, ssd_d, ssd_norm_w, att_rel_bias, lru_conv_w, lru_conv_b, lru_wa, lru_ba, lru_wx, lru_bx, lru_lambda, gate_bias, w_ssd_proj, w_att_proj, w_lru_proj, w_out, ln1_g, ln1_b, router_w, router_bias, exp_w_gate, exp_w_up, exp_w_down, sh_w_gate, sh_w_up, sh_w_down, ln2_g, ln2_b):
    w = dict(w_in=w_in, ssd_conv_w=ssd_conv_w, ssd_conv_b=ssd_conv_b, ssd_dt_bias=ssd_dt_bias, ssd_a_log=ssd_a_log,
             ssd_d=ssd_d, ssd_norm_w=ssd_norm_w, att_rel_bias=att_rel_bias, lru_conv_w=lru_conv_w,
             lru_conv_b=lru_conv_b, lru_wa=lru_wa, lru_ba=lru_ba, lru_wx=lru_wx, lru_bx=lru_bx,
             lru_lambda=lru_lambda, gate_bias=gate_bias, w_ssd_proj=w_ssd_proj, w_att_proj=w_att_proj,
             w_lru_proj=w_lru_proj, w_out=w_out, ln1_g=ln1_g, ln1_b=ln1_b, router_w=router_w,
             router_bias=router_bias, exp_w_gate=exp_w_gate, exp_w_up=exp_w_up, exp_w_down=exp_w_down,
             sh_w_gate=sh_w_gate, sh_w_up=sh_w_up, sh_w_down=sh_w_down, ln2_g=ln2_g, ln2_b=ln2_b)
    bp, tp, _ = x_prompt.shape
    bs, ts, _ = x_sample.shape
    n_p = bp * tp
    x = jnp.concatenate([x_prompt.reshape(n_p, D_MODEL), x_sample.reshape(bs * ts, D_MODEL)], axis=0)
    xb = x.astype(BF16)
    w.update(_prep_stacks(w))
    cache_att_kt = jnp.transpose(cache_att_k, (0, 1, 3, 4, 2))
    cache_att_vt = jnp.transpose(cache_att_v, (0, 1, 3, 4, 2))
    per_layer = []
    for l in range(DEPTH):
        st = dict(cache_ssd_conv=cache_ssd_conv[l], state_ssd=state_ssd[l], cache_att_kt=cache_att_kt,
                  cache_att_vt=cache_att_vt, cache_lru_conv=cache_lru_conv[l], state_lru=state_lru[l])
        x, xb, states = _layer(x, xb, st, _prep_layer(w, l), (bp, tp, bs, ts), l)
        per_layer.append(states)
    y_prompt, y_sample = x, xb
    stack = lambda k: jnp.stack([s[k] for s in per_layer], axis=0)
    return (y_prompt.reshape(bp, tp, D_MODEL), y_sample.reshape(bs, ts, D_MODEL),
            stack("p_ssd_conv"), stack("s_ssd_conv"), stack("p_ssd_state"), stack("s_ssd_state"),
            stack("p_att_k"), stack("s_att_k"), stack("p_att_v"), stack("s_att_v"),
            stack("p_lru_conv"), stack("s_lru_conv"), stack("p_lru_state"), stack("s_lru_state"))
```

```python
import functools
import math

import jax
import jax.numpy as jnp
import numpy as np
from jax import lax
from jax.experimental import pallas as pl
from jax.experimental.pallas import tpu as pltpu

F32 = jnp.float32
BF16 = jnp.bfloat16
HIGHEST = lax.Precision.HIGHEST

D_MODEL = 1024
DEPTH = 2
CHUNK = 64
CONV_WIDTH = 4
SSD_HEADS = 16
SSD_HEAD_DIM = 64
SSD_GROUPS = 4
SSD_STATE = 128
SSD_CONV_DIM = D_MODEL + 2 * SSD_GROUPS * SSD_STATE
ATT_HEADS = 16
ATT_HEAD_DIM = 64
ATT_LEFT = 8 * CHUNK
ATT_BAND = ATT_LEFT + CHUNK
REL_CLIP = 128
ATT_SCALE = ATT_HEAD_DIM ** -0.5
LRU_BLOCKS = 16
LRU_C = 8.0
N_BRANCH = 3
N_EXPERTS = 64
N_GROUPS = 8
GROUP_SIZE = N_EXPERTS // N_GROUPS
TOPK_GROUPS = 4
TOP_K = 8
EXPERT_HIDDEN = D_MODEL // 4
ROUTED_SCALE = 2.5
DN_ALPHA = (2.0 * DEPTH) ** 0.25
LN_EPS = 1e-5
RMS_EPS = 1e-5

LANES = 128
SUBLANES = 8
CONV_PAD = 8

COL_G, COL_Z, COL_XBC, COL_XL, COL_YL = 0, 3, 2, 6, 7
COL_Q, COL_K, COL_V = 0, 1, 2
ATT_SUB = 256
MOE_GROUP = 8
SSD_CHUNKS_PER_STEP = 4
NEG_BIG = -1e30


def _params(semantics, vmem_mb):
    return pltpu.CompilerParams(dimension_semantics=semantics, vmem_limit_bytes=vmem_mb * 2 ** 20)


def _tile(n, candidates):
    for c in candidates:
        if n % c == 0:
            return c
    raise ValueError(f"no tile for {n}")


def _nn(a, b, precision=None):
    return jnp.dot(a, b, preferred_element_type=F32, precision=precision)


def _nt(a, b, precision=None):
    return lax.dot_general(a, b, (((1,), (1,)), ((), ())), preferred_element_type=F32, precision=precision)


def _tn(a, b):
    return lax.dot_general(a, b, (((0,), (0,)), ((), ())), preferred_element_type=F32)


def _sigmoid(x):
    return 0.5 * jnp.tanh(0.5 * x) + 0.5


def _silu(x):
    return x * _sigmoid(x)


def _softplus(x):
    return jnp.maximum(x, 0.0) + jnp.log1p(jnp.exp(-jnp.abs(x)))


def _layer_norm(y, g, b):
    mu = jnp.mean(y, axis=-1, keepdims=True)
    d = y - mu
    var = jnp.mean(d * d, axis=-1, keepdims=True)
    return d * lax.rsqrt(var + LN_EPS) * g + b


def _const_spec(shape):
    nd = len(shape)
    return pl.BlockSpec(shape, lambda *_: (0,) * nd)


def _mm_body(a_ref, b_ref, o_ref):
    o_ref[...] = _nn(a_ref[...], b_ref[...]).astype(o_ref.dtype)


def _matmul(a, b, tn, name, out_dtype=F32):
    m, k = a.shape
    n = b.shape[1]
    tm = _tile(m, (1024, 512, 256, 128))
    return pl.pallas_call(
        _mm_body,
        grid=(m // tm, n // tn),
        in_specs=[pl.BlockSpec((tm, k), lambda i, j: (i, 0)), pl.BlockSpec((k, tn), lambda i, j: (0, j))],
        out_specs=pl.BlockSpec((tm, tn), lambda i, j: (i, j)),
        out_shape=jax.ShapeDtypeStruct((m, n), out_dtype),
        compiler_params=_params(("parallel", "arbitrary"), 40),
        name=name,
    )(a, b)


def _conv_block(xp_scr, x_ref, cw_ref, cb_ref, rows):
    lo = CONV_PAD - (CONV_WIDTH - 1)
    x = x_ref[...]
    carried = xp_scr[lo:CONV_PAD, :]
    first = lax.broadcasted_iota(jnp.int32, (SUBLANES, 1), 0) == 0
    delayed = [x]
    for k in range(1, CONV_WIDTH):
        rolled = pltpu.roll(delayed[-1], 1, 0)
        patch = jnp.where(first, carried[CONV_WIDTH - 1 - k:CONV_WIDTH - k, :], rolled[0:SUBLANES, :])
        delayed.append(jnp.concatenate([patch, rolled[SUBLANES:, :]], axis=0))
    y = cb_ref[...] + delayed[CONV_WIDTH - 1] * cw_ref[0:1, :]
    for k in range(1, CONV_WIDTH):
        y = y + delayed[CONV_WIDTH - 1 - k] * cw_ref[k:k + 1, :]
    tail = x[rows - (CONV_WIDTH - 1):rows, :]
    xp_scr[lo:CONV_PAD, :] = tail
    return y, tail


def _split3(x):
    hi = x.astype(BF16)
    r1 = x - hi.astype(F32)
    mid = r1.astype(BF16)
    return hi, mid, (r1 - mid.astype(F32)).astype(BF16)


def _ssd_body(z_ref, xbc_ref, dt_ref, buf0_ref, h0_ref, cw_ref, cb_ref, dtb_ref, alog_ref, dch_ref, nw_ref,
              expand_ref, y_ref, bufo_ref, ho_ref, xp_scr, h_scr, *, nc, cps):
    c = pl.program_id(1)
    lo = CONV_PAD - (CONV_WIDTH - 1)

    @pl.when(c == 0)
    def _():
        xp_scr[lo:CONV_PAD, :] = buf0_ref[0]
        h_scr[...] = h0_ref[0]

    conv, tail = _conv_block(xp_scr, xbc_ref, cw_ref, cb_ref, cps * CHUNK)
    bufo_ref[0] = tail
    xbc_all = _silu(conv)
    dt_all = _softplus(dt_ref[...] + dtb_ref[...])
    da_all = dt_all * (-jnp.exp(alog_ref[...]))
    row = lax.broadcasted_iota(jnp.int32, (CHUNK, CHUNK), 0)
    col = lax.broadcasted_iota(jnp.int32, (CHUNK, CHUNK), 1)
    causal = row >= col
    tri = causal.astype(BF16)
    eye = (lax.broadcasted_iota(jnp.int32, (LANES, LANES), 0)
           == lax.broadcasted_iota(jnp.int32, (LANES, LANES), 1)).astype(BF16)
    for ci in range(cps):
        rs = slice(ci * CHUNK, (ci + 1) * CHUNK)
        _ssd_chunk(xbc_all[rs], dt_all[rs], da_all[rs], z_ref[rs, :], tri, eye, causal, expand_ref, dch_ref,
                   nw_ref, h_scr, y_ref, rs)

    @pl.when(c == nc - 1)
    def _():
        ho_ref[0] = h_scr[...]


def _ssd_chunk(xbc, dt, da, z, tri, eye, causal, expand_ref, dch_ref, nw_ref, h_scr, y_ref, rs):
    xs = xbc[:, :D_MODEL]
    bm = xbc[:, D_MODEL:D_MODEL + SSD_GROUPS * SSD_STATE].astype(BF16)
    cm = xbc[:, D_MODEL + SSD_GROUPS * SSD_STATE:].astype(BF16)
    cs = sum(_nn(tri, piece) for piece in _split3(da))
    cs_last = cs[CHUNK - 1:CHUNK, :]
    to_end = jnp.exp(cs_last - cs)
    ecs = jnp.exp(cs)
    expand = expand_ref[...]
    wide = sum(_nn(piece, expand) for piece in _split3(jnp.concatenate([dt, to_end, ecs], axis=0)))
    dt_ch = wide[0:CHUNK]
    to_end_ch = wide[CHUNK:2 * CHUNK]
    ecs_ch = wide[2 * CHUNK:3 * CHUNK]
    dec_ch = ecs_ch[CHUNK - 1:CHUNK, :]
    cs_t = sum(_nt(eye, piece) for piece in _split3(cs))

    xdt = xs * dt_ch
    xdt_b = xdt.astype(BF16)
    xw_b = (xdt * to_end_ch).astype(BF16)
    gw = SSD_HEADS // SSD_GROUPS * SSD_HEAD_DIM
    y_parts = []
    for g in range(SSD_GROUPS):
        bg = bm[:, g * SSD_STATE:(g + 1) * SSD_STATE]
        cg = cm[:, g * SSD_STATE:(g + 1) * SSD_STATE]
        gsl = slice(g * gw, (g + 1) * gw)
        cb = _nt(cg, bg)
        h_prev = h_scr[:, gsl]
        y_off = _nn(cg, h_prev.astype(BF16)) * ecs_ch[:, gsl]
        st = _tn(bg, xw_b[:, gsl])
        h_scr[:, gsl] = h_prev * dec_ch[:, gsl] + st
        diag = []
        for r in range(SSD_HEADS // SSD_GROUPS):
            h = g * (SSD_HEADS // SSD_GROUPS) + r
            seg = cs[:, h:h + 1] - cs_t[h:h + 1, :CHUNK]
            decay = jnp.where(causal, jnp.exp(jnp.where(causal, seg, 0.0)), 0.0)
            sc = (cb * decay).astype(BF16)
            diag.append(_nn(sc, xdt_b[:, h * SSD_HEAD_DIM:(h + 1) * SSD_HEAD_DIM]))
        y_parts.append(jnp.concatenate(diag, axis=-1) + y_off)
    y = jnp.concatenate(y_parts, axis=-1) + xs * dch_ref[...]
    v = y * _silu(z)
    outs = []
    for g in range(SSD_GROUPS):
        vg = v[:, g * gw:(g + 1) * gw]
        ms = jnp.mean(vg * vg, axis=-1, keepdims=True)
        outs.append(vg * lax.rsqrt(ms + RMS_EPS))
    y_ref[rs, :] = (jnp.concatenate(outs, axis=-1) * nw_ref[...]).astype(y_ref.dtype)


def _ssd_call(proj, proj_dt, buf0, h0_t, p, nb, nchunks, blk0, name):
    cps = _tile(nchunks, (SSD_CHUNKS_PER_STEP, 1))
    nc = nchunks // cps
    rows = cps * CHUNK
    row = lambda b, c: blk0 // cps + b * nc + c
    in_specs = [
        pl.BlockSpec((rows, D_MODEL), lambda b, c: (row(b, c), COL_Z)),
        pl.BlockSpec((rows, SSD_CONV_DIM), lambda b, c: (row(b, c), COL_XBC)),
        pl.BlockSpec((rows, LANES), lambda b, c: (row(b, c), 0)),
        pl.BlockSpec((1, CONV_WIDTH - 1, SSD_CONV_DIM), lambda b, c: (b, 0, 0)),
        pl.BlockSpec((1, SSD_STATE, D_MODEL), lambda b, c: (b, 0, 0)),
        _const_spec((CONV_WIDTH, SSD_CONV_DIM)),
        _const_spec((1, SSD_CONV_DIM)),
        _const_spec((1, LANES)),
        _const_spec((1, LANES)),
        _const_spec((1, D_MODEL)),
        _const_spec((1, D_MODEL)),
        _const_spec((LANES, D_MODEL)),
    ]
    out_specs = [
        pl.BlockSpec((rows, D_MODEL), lambda b, c: (b * nc + c, 0)),
        pl.BlockSpec((1, CONV_WIDTH - 1, SSD_CONV_DIM), lambda b, c: (b, 0, 0)),
        pl.BlockSpec((1, SSD_STATE, D_MODEL), lambda b, c: (b, 0, 0)),
    ]
    out_shape = [
        jax.ShapeDtypeStruct((nb * nchunks * CHUNK, D_MODEL), BF16),
        jax.ShapeDtypeStruct((nb, CONV_WIDTH - 1, SSD_CONV_DIM), F32),
        jax.ShapeDtypeStruct((nb, SSD_STATE, D_MODEL), F32),
    ]
    return pl.pallas_call(
        functools.partial(_ssd_body, nc=nc, cps=cps),
        grid=(nb, nc),
        in_specs=in_specs,
        out_specs=out_specs,
        out_shape=out_shape,
        scratch_shapes=[pltpu.VMEM((CONV_PAD, SSD_CONV_DIM), F32), pltpu.VMEM((SSD_STATE, D_MODEL), F32)],
        compiler_params=_params(("parallel", "arbitrary"), 40),
        name=name,
    )(proj, proj, proj_dt, buf0, h0_t, p["ssd_conv_w"], p["ssd_conv_b"], p["ssd_dt_bias"], p["ssd_a_log"],
      p["ssd_d_ch"], p["ssd_norm_w"], p["ssd_expand"])


def _head_pair_queries(q2):
    first = lax.broadcasted_iota(jnp.int32, (1, LANES), 1) < ATT_HEAD_DIM
    zero = jnp.zeros_like(q2)
    return first, (jnp.where(first, q2, zero), jnp.where(first, zero, q2))


def _att_prompt_body(q_ref, kp_ref, kc_ref, vp_ref, vc_ref, bias_ref, o_ref, k_scr, v_scr, *, rows, sub):
    t = pl.program_id(1)
    k_scr[0:ATT_LEFT, :] = kp_ref[...]
    v_scr[0:ATT_LEFT, :] = vp_ref[...]
    k_scr[ATT_LEFT:ATT_LEFT + rows, :] = kc_ref[...]
    v_scr[ATT_LEFT:ATT_LEFT + rows, :] = vc_ref[...]
    win = sub + ATT_LEFT

    def tile(first_tile):
        for s in range(rows // sub):
            r0 = s * sub
            dead = lax.broadcasted_iota(jnp.int32, (1, win), 1) + r0 < ATT_LEFT
            for hp in range(ATT_HEADS // 2):
                ls = slice(hp * LANES, (hp + 1) * LANES)
                first, queries = _head_pair_queries(q_ref[r0:r0 + sub, ls])
                k2 = k_scr[r0:r0 + win, ls]
                v2 = v_scr[r0:r0 + win, ls]
                outs = []
                for h, qh in zip((2 * hp, 2 * hp + 1), queries):
                    sc = _nt(qh, k2) + bias_ref[h]
                    if first_tile:
                        sc = jnp.where(dead, NEG_BIG, sc)
                    m = jnp.max(sc, axis=-1, keepdims=True)
                    e = jnp.exp(sc - m)
                    den = jnp.sum(e, axis=-1, keepdims=True)
                    outs.append(_nn(e.astype(BF16), v2) / den)
                o_ref[r0:r0 + sub, ls] = jnp.where(first, outs[0], outs[1]).astype(o_ref.dtype)

    pl.when(t == 0)(functools.partial(tile, True))
    pl.when(t > 0)(functools.partial(tile, False))


def _att_prompt_call(qkv, bias, nb, nt, rows, name):
    sub = min(rows, ATT_SUB)
    cur = lambda col: pl.BlockSpec((rows, D_MODEL), lambda b, t: (b * nt + t, col))
    prev = lambda col: pl.BlockSpec((ATT_LEFT, D_MODEL), lambda b, t: (b * nt + jnp.maximum(t - 1, 0), col))
    return pl.pallas_call(
        functools.partial(_att_prompt_body, rows=rows, sub=sub),
        grid=(nb, nt),
        in_specs=[cur(COL_Q), prev(COL_K), cur(COL_K), prev(COL_V), cur(COL_V),
                  _const_spec((ATT_HEADS, sub, sub + ATT_LEFT))],
        out_specs=pl.BlockSpec((rows, D_MODEL), lambda b, t: (b * nt + t, 0)),
        out_shape=jax.ShapeDtypeStruct((nb * nt * rows, D_MODEL), BF16),
        scratch_shapes=[pltpu.VMEM((ATT_LEFT + rows, D_MODEL), BF16), pltpu.VMEM((ATT_LEFT + rows, D_MODEL), BF16)],
        compiler_params=_params(("parallel", "arbitrary"), 56),
        name=name,
    )(qkv, qkv, qkv, qkv, qkv, bias)


def _att_sample_body(q_ref, kn_ref, vn_ref, kt_ref, vt_ref, bias_ref, o_ref):
    for hp in range(ATT_HEADS // 2):
        ls = slice(hp * LANES, (hp + 1) * LANES)
        first, queries = _head_pair_queries(q_ref[:, ls])
        kn2 = kn_ref[:, ls]
        vn2 = vn_ref[:, ls]
        kt2 = jnp.concatenate([kt_ref[2 * hp], kt_ref[2 * hp + 1]], axis=0).astype(BF16)
        vt2 = jnp.concatenate([vt_ref[2 * hp], vt_ref[2 * hp + 1]], axis=0).astype(BF16)
        outs = []
        for h, qh in zip((2 * hp, 2 * hp + 1), queries):
            bias = bias_ref[h]
            s_old = _nn(qh, kt2) + bias[:, :ATT_LEFT]
            s_new = _nt(qh, kn2) + bias[:, ATT_LEFT:]
            m = jnp.maximum(jnp.max(s_old, axis=-1, keepdims=True), jnp.max(s_new, axis=-1, keepdims=True))
            e_old = jnp.exp(s_old - m)
            e_new = jnp.exp(s_new - m)
            den = jnp.sum(e_old, axis=-1, keepdims=True) + jnp.sum(e_new, axis=-1, keepdims=True)
            outs.append((_nt(e_old.astype(BF16), vt2) + _nn(e_new.astype(BF16), vn2)) / den)
        o_ref[:, ls] = jnp.where(first, outs[0], outs[1]).astype(o_ref.dtype)


def _att_sample_call(qkv, k_cache_t, v_cache_t, bias, nb, rows, blk0, layer, name):
    cur = lambda col: pl.BlockSpec((rows, D_MODEL), lambda b: (blk0 + b, col))
    cache = pl.BlockSpec((None, None, ATT_HEADS, ATT_HEAD_DIM, ATT_LEFT), lambda b: (layer, b, 0, 0, 0))
    return pl.pallas_call(
        _att_sample_body,
        grid=(nb,),
        in_specs=[cur(COL_Q), cur(COL_K), cur(COL_V), cache, cache, _const_spec((ATT_HEADS, rows, rows + ATT_LEFT))],
        out_specs=pl.BlockSpec((rows, D_MODEL), lambda b: (b, 0)),
        out_shape=jax.ShapeDtypeStruct((nb * rows, D_MODEL), BF16),
        compiler_params=_params(("parallel",), 40),
        name=name,
    )(qkv, qkv, qkv, k_cache_t, v_cache_t, bias)


def _band_bias(table, sub):
    win = sub + ATT_LEFT
    span = -(-(sub - 1 + win) // LANES) * LANES
    idx = [min(max(win - 1 - m, -REL_CLIP), REL_CLIP) + REL_CLIP for m in range(span)]
    vec = table[:, np.array(idx, np.int32)].reshape(ATT_HEADS, 1, span)
    return pl.pallas_call(
        functools.partial(_band_bias_body, sub=sub, win=win),
        grid=(ATT_HEADS,),
        in_specs=[pl.BlockSpec((1, 1, span), lambda h: (h, 0, 0))],
        out_specs=pl.BlockSpec((1, sub, win), lambda h: (h, 0, 0)),
        out_shape=jax.ShapeDtypeStruct((ATT_HEADS, sub, win), F32),
        compiler_params=_params(("parallel",), 32),
        name="band_bias",
    )(vec)


def _band_bias_body(vec_ref, o_ref, *, sub, win):
    span = vec_ref.shape[-1]
    toep = pltpu.roll(jnp.broadcast_to(vec_ref[0], (sub, span)), span - (sub - 1), 1, stride=1, stride_axis=0)
    r = lax.broadcasted_iota(jnp.int32, (sub, win), 0)
    j = lax.broadcasted_iota(jnp.int32, (sub, win), 1) - (r - jnp.bitwise_and(r, CHUNK - 1))
    o_ref[0] = jnp.where(jnp.logical_and(j >= 0, j < ATT_BAND), toep[:, :win], NEG_BIG)


def _lru_scan(a, u, carry):
    rows, width = a.shape
    a3 = a.reshape(rows // SUBLANES, SUBLANES, width)
    u3 = u.reshape(rows // SUBLANES, SUBLANES, width)
    pos = lax.broadcasted_iota(jnp.int32, (1, SUBLANES, 1), 1)
    d = 1
    while d < SUBLANES:
        a_sh = pltpu.roll(a3, d, 1)
        u_sh = pltpu.roll(u3, d, 1)
        m = pos >= d
        u3 = jnp.where(m, a3 * u_sh + u3, u3)
        a3 = jnp.where(m, a3 * a_sh, a3)
        d *= 2
    out = []
    for g in range(rows // SUBLANES):
        h = a3[g] * carry + u3[g]
        carry = h[SUBLANES - 1:SUBLANES, :]
        out.append(h)
    return jnp.concatenate(out, axis=0), carry


def _gelu_tanh(x):
    return 0.5 * x * (1.0 + jnp.tanh(math.sqrt(2.0 / math.pi) * (x + 0.044715 * (x * x * x))))


def _lru_body(xl_ref, yl_ref, buf0_ref, h0_ref, cw_ref, cb_ref, wa_ref, ba_ref, wx_ref, bx_ref, lam_ref,
              o_ref, bufo_ref, ho_ref, xp_scr, carry_scr, *, rows):
    c = pl.program_id(1)
    lo = CONV_PAD - (CONV_WIDTH - 1)

    @pl.when(c == 0)
    def _():
        xp_scr[lo:CONV_PAD, :] = buf0_ref[0]
        carry_scr[...] = h0_ref[0]

    xc, tail = _conv_block(xp_scr, xl_ref, cw_ref, cb_ref, rows)
    bufo_ref[0] = tail
    xcb = xc.astype(BF16)
    gate_r = _sigmoid(_nn(xcb, wa_ref[...]) + ba_ref[...])
    gate_i = _sigmoid(_nn(xcb, wx_ref[...]) + bx_ref[...])
    log_a = LRU_C * gate_r * (-_softplus(-lam_ref[...]))
    a = jnp.exp(log_a)
    u = jnp.sqrt(1.0 - a * a) * (gate_i * xc)
    h, carry = _lru_scan(a, u, carry_scr[...])
    o_ref[...] = (h * _gelu_tanh(yl_ref[...])).astype(o_ref.dtype)
    carry_scr[...] = carry
    ho_ref[0] = carry


def _lru_call(proj, buf0, h0, p, nb, nc, rows, blk0, name):
    row = lambda b, c: blk0 + b * nc + c
    in_specs = [
        pl.BlockSpec((rows, D_MODEL), lambda b, c: (row(b, c), COL_XL)),
        pl.BlockSpec((rows, D_MODEL), lambda b, c: (row(b, c), COL_YL)),
        pl.BlockSpec((1, CONV_WIDTH - 1, D_MODEL), lambda b, c: (b, 0, 0)),
        pl.BlockSpec((1, 1, D_MODEL), lambda b, c: (b, 0, 0)),
        _const_spec((CONV_WIDTH, D_MODEL)),
        _const_spec((1, D_MODEL)),
        _const_spec((D_MODEL, D_MODEL)),
        _const_spec((1, D_MODEL)),
        _const_spec((D_MODEL, D_MODEL)),
        _const_spec((1, D_MODEL)),
        _const_spec((1, D_MODEL)),
    ]
    out_specs = [
        pl.BlockSpec((rows, D_MODEL), lambda b, c: (b * nc + c, 0)),
        pl.BlockSpec((1, CONV_WIDTH - 1, D_MODEL), lambda b, c: (b, 0, 0)),
        pl.BlockSpec((1, 1, D_MODEL), lambda b, c: (b, 0, 0)),
    ]
    out_shape = [
        jax.ShapeDtypeStruct((nb * nc * rows, D_MODEL), BF16),
        jax.ShapeDtypeStruct((nb, CONV_WIDTH - 1, D_MODEL), F32),
        jax.ShapeDtypeStruct((nb, 1, D_MODEL), F32),
    ]
    return pl.pallas_call(
        functools.partial(_lru_body, rows=rows),
        grid=(nb, nc),
        in_specs=in_specs,
        out_specs=out_specs,
        out_shape=out_shape,
        scratch_shapes=[pltpu.VMEM((CONV_PAD, D_MODEL), F32), pltpu.VMEM((1, D_MODEL), F32)],
        compiler_params=_params(("parallel", "arbitrary"), 48),
        name=name,
    )(proj, proj, buf0, h0, p["lru_conv_w"], p["lru_conv_b"], p["lru_wa_d"], p["lru_ba"], p["lru_wx_d"],
      p["lru_bx"], p["lru_lambda"])


def _merge_body(sp_ref, ss_ref, ap_ref, as_ref, lp_ref, ls_ref, g_ref, x_ref, ws_ref, wa_ref, wl_ref, wo_ref,
                gb_ref, lg_ref, lb_ref, o_ref, ob_ref, *, prompt_tiles):
    in_prompt = pl.program_id(0) < prompt_tiles
    branch = lambda p_ref, s_ref: jnp.where(in_prompt, p_ref[...], s_ref[...])
    g = _sigmoid(g_ref[...] + gb_ref[...])
    m = (g[:, 0:D_MODEL] * _nn(branch(sp_ref, ss_ref), ws_ref[...])
         + g[:, D_MODEL:2 * D_MODEL] * _nn(branch(ap_ref, as_ref), wa_ref[...])
         + g[:, 2 * D_MODEL:3 * D_MODEL] * _nn(branch(lp_ref, ls_ref), wl_ref[...]))
    y = DN_ALPHA * x_ref[...] + _nn(m.astype(BF16), wo_ref[...])
    x1 = _layer_norm(y, lg_ref[...], lb_ref[...])
    o_ref[...] = x1
    ob_ref[...] = x1.astype(BF16)


def _merge_call(branches, proj, x, p, name):
    n = x.shape[0]
    n_p = branches[0][0].shape[0]
    tm = _tile(math.gcd(n_p, n - n_p), (512, 256, 128))
    pt = n_p // tm
    rowblk = lambda w: pl.BlockSpec((tm, w), lambda i: (i, 0))
    pblk = pl.BlockSpec((tm, D_MODEL), lambda i: (jnp.minimum(i, pt - 1), 0))
    sblk = pl.BlockSpec((tm, D_MODEL), lambda i: (jnp.maximum(i - pt, 0), 0))
    wspec = pl.BlockSpec((D_MODEL, D_MODEL), lambda i: (0, 0), pipeline_mode=pl.Buffered(1))
    return pl.pallas_call(
        functools.partial(_merge_body, prompt_tiles=pt),
        grid=(n // tm,),
        in_specs=[pblk, sblk, pblk, sblk, pblk, sblk,
                  pl.BlockSpec((tm, N_BRANCH * D_MODEL), lambda i: (i, COL_G)), rowblk(D_MODEL),
                  wspec, wspec, wspec, wspec,
                  _const_spec((1, N_BRANCH * D_MODEL)), _const_spec((1, D_MODEL)), _const_spec((1, D_MODEL))],
        out_specs=[rowblk(D_MODEL), rowblk(D_MODEL)],
        out_shape=[jax.ShapeDtypeStruct((n, D_MODEL), F32), jax.ShapeDtypeStruct((n, D_MODEL), BF16)],
        compiler_params=_params(("parallel",), 58),
        name=name,
    )(*branches[0], *branches[1], *branches[2], proj, x, p["w_ssd_proj"], p["w_att_proj"], p["w_lru_proj"],
      p["w_out"], p["gate_bias"], p["ln1_g"], p["ln1_b"])


def _first_max(x, idx, big):
    m = jnp.max(x, axis=0, keepdims=True)
    first = jnp.min(jnp.where(x == m, idx, big), axis=0, keepdims=True)
    return m, idx == first


def _router_body(x_ref, rwt_ref, rb_ref, o_ref):
    tm = x_ref.shape[0]
    logits = _nt(rwt_ref[...], x_ref[...], HIGHEST)
    scores = _sigmoid(logits)
    sel = scores + rb_ref[...]
    sub = lax.broadcasted_iota(jnp.int32, (GROUP_SIZE, tm), 0)
    gscore = jnp.zeros((N_GROUPS, tm), F32)
    for g in range(N_GROUPS):
        sg = sel[g * GROUP_SIZE:(g + 1) * GROUP_SIZE, :]
        m1, hit = _first_max(sg, sub, GROUP_SIZE)
        m2 = jnp.max(jnp.where(hit, -jnp.inf, sg), axis=0, keepdims=True)
        gscore = jnp.where(sub == g, m1 + m2, gscore)
    gsel = jnp.zeros((N_GROUPS, tm), F32)
    for _ in range(TOPK_GROUPS):
        _, hit = _first_max(gscore, sub, N_GROUPS)
        gsel = jnp.where(hit, 1.0, gsel)
        gscore = jnp.where(hit, -jnp.inf, gscore)
    masked = jnp.concatenate(
        [jnp.where(gsel[g:g + 1, :] > 0.0, sel[g * GROUP_SIZE:(g + 1) * GROUP_SIZE, :], -jnp.inf)
         for g in range(N_GROUPS)], axis=0)
    eidx = lax.broadcasted_iota(jnp.int32, (N_EXPERTS, tm), 0)
    chosen = jnp.zeros((N_EXPERTS, tm), F32)
    for _ in range(TOP_K):
        _, hit = _first_max(masked, eidx, N_EXPERTS)
        chosen = jnp.where(hit, 1.0, chosen)
        masked = jnp.where(hit, -jnp.inf, masked)
    w = chosen * scores
    o_ref[...] = w / jnp.sum(w, axis=0, keepdims=True) * ROUTED_SCALE


def _router_call(x1, p, name):
    n = x1.shape[0]
    tm = _tile(n, (512, 256, 128))
    return pl.pallas_call(
        _router_body,
        grid=(n // tm,),
        in_specs=[pl.BlockSpec((tm, D_MODEL), lambda i: (i, 0)), _const_spec((N_EXPERTS, D_MODEL)),
                  _const_spec((N_EXPERTS, 1))],
        out_specs=pl.BlockSpec((N_EXPERTS, tm), lambda i: (0, i)),
        out_shape=jax.ShapeDtypeStruct((N_EXPERTS, n), F32),
        compiler_params=_params(("parallel",), 32),
        name=name,
    )(x1, p["router_w_t"], p["router_bias"])


def _swiglu(xb, wgu, width):
    gu = _nn(xb, wgu)
    return _silu(gu[:, :width]) * gu[:, width:]


def _moe_body(xb_ref, x_ref, g_ref, wg_ref, wu_ref, wd_ref, sgu_ref, sd_ref, lg_ref, lb_ref, o_ref, o2_ref, acc,
              *, prompt_tiles):
    j = pl.program_id(1)
    xb = xb_ref[...]
    g = pltpu.roll(g_ref[...], (LANES - j * MOE_GROUP) % LANES, 1)
    scaled = [_silu(_nn(xb, wg_ref[c])) * _nn(xb, wu_ref[c]) * g[:, c:c + 1] for c in range(MOE_GROUP)]
    part = _nn(jnp.concatenate(scaled, axis=-1).astype(BF16), wd_ref[0])

    @pl.when(j == 0)
    def _():
        acc[...] = part

    @pl.when(j > 0)
    def _():
        acc[...] += part

    @pl.when(j == pl.num_programs(1) - 1)
    def _():
        shared = _nn(_swiglu(xb, sgu_ref[...], EXPERT_HIDDEN).astype(BF16), sd_ref[...])
        x2 = _layer_norm(DN_ALPHA * x_ref[...] + (acc[...] + shared), lg_ref[...], lb_ref[...])
        if prompt_tiles is None:
            o_ref[...] = x2
            o2_ref[...] = x2.astype(BF16)
        else:
            @pl.when(pl.program_id(0) < prompt_tiles)
            def _():
                o_ref[...] = x2

            @pl.when(pl.program_id(0) >= prompt_tiles)
            def _():
                o2_ref[...] = x2


def _moe_call(x1b, x1, gates, p, name, n_prompt=None):
    n = x1.shape[0]
    layer = p["layer"]
    tm = _tile(n if n_prompt is None else math.gcd(n_prompt, n - n_prompt), (512, 256, 128))
    width = MOE_GROUP * EXPERT_HIDDEN
    rowblk = lambda w: pl.BlockSpec((tm, w), lambda i, j: (i, 0))
    if n_prompt is None:
        pt = None
        out_specs = [rowblk(D_MODEL), rowblk(D_MODEL)]
        out_shape = [jax.ShapeDtypeStruct((n, D_MODEL), F32), jax.ShapeDtypeStruct((n, D_MODEL), BF16)]
        semantics = ("parallel", "arbitrary")
    else:
        pt = n_prompt // tm
        out_specs = [pl.BlockSpec((tm, D_MODEL), lambda i, j: (jnp.minimum(i, pt - 1), 0)),
                     pl.BlockSpec((tm, D_MODEL), lambda i, j: (jnp.maximum(i - pt, 0), 0))]
        out_shape = [jax.ShapeDtypeStruct((n_prompt, D_MODEL), F32),
                     jax.ShapeDtypeStruct((n - n_prompt, D_MODEL), F32)]
        semantics = ("arbitrary", "arbitrary")
    return pl.pallas_call(
        functools.partial(_moe_body, prompt_tiles=pt),
        grid=(n // tm, N_EXPERTS // MOE_GROUP),
        in_specs=[rowblk(D_MODEL), rowblk(D_MODEL), rowblk(LANES),
                  pl.BlockSpec((None, MOE_GROUP, D_MODEL, EXPERT_HIDDEN), lambda i, j: (layer, j, 0, 0)),
                  pl.BlockSpec((None, MOE_GROUP, D_MODEL, EXPERT_HIDDEN), lambda i, j: (layer, j, 0, 0)),
                  pl.BlockSpec((None, 1, width, D_MODEL), lambda i, j: (layer, j, 0, 0)),
                  _const_spec((D_MODEL, 2 * EXPERT_HIDDEN)), _const_spec((EXPERT_HIDDEN, D_MODEL)),
                  _const_spec((1, D_MODEL)), _const_spec((1, D_MODEL))],
        out_specs=out_specs,
        out_shape=out_shape,
        scratch_shapes=[pltpu.VMEM((tm, D_MODEL), F32)],
        compiler_params=_params(semantics, 58),
        name=name,
    )(x1b, x1, gates, p["exp_wg"], p["exp_wu"], p["exp_wd"], p["sh_wgu"], p["sh_wd"], p["ln2_g"], p["ln2_b"])


def _block_diag(w):
    k, d, _ = w.shape
    eye = jnp.eye(k, dtype=w.dtype)
    return (eye[:, None, :, None] * w[:, :, None, :]).reshape(k * d, k * d)


def _prep_stacks(w):
    depth = w["exp_w_down"].shape[0]
    return dict(
        w_in_b=w["w_in"].astype(BF16),
        exp_wg_all=w["exp_w_gate"].astype(BF16), exp_wu_all=w["exp_w_up"].astype(BF16),
        exp_wd_all=w["exp_w_down"].reshape(depth, N_EXPERTS // MOE_GROUP, MOE_GROUP * EXPERT_HIDDEN,
                                           D_MODEL).astype(BF16))


def _prep_layer(w, l):
    w_in = w["w_in_b"][l]
    o = 0
    seg = {}
    for nm, width in (("z", D_MODEL), ("xbc", SSD_CONV_DIM), ("dt", SSD_HEADS), ("q", D_MODEL), ("k", D_MODEL),
                      ("v", D_MODEL), ("xl", D_MODEL), ("yl", D_MODEL), ("g", N_BRANCH * D_MODEL)):
        seg[nm] = w_in[:, o:o + width]
        o += width
    row = lambda a: a.reshape(1, -1)
    lane_pad = lambda a: jnp.pad(a.reshape(1, -1), ((0, 0), (0, LANES - a.shape[-1])))
    head_of_channel = jnp.arange(D_MODEL) // SSD_HEAD_DIM
    return dict(
        w_main=jnp.concatenate([seg[k] for k in ("g", "z", "xbc", "xl", "yl")], axis=1).astype(BF16),
        w_qkv=jnp.concatenate([seg["q"] * ATT_SCALE, seg["k"], seg["v"]], axis=1).astype(BF16),
        w_kv=jnp.concatenate([seg["k"], seg["v"]], axis=1).astype(BF16),
        w_dt=jnp.pad(seg["dt"], ((0, 0), (0, LANES - SSD_HEADS))).astype(BF16),
        ssd_conv_w=w["ssd_conv_w"][l], ssd_conv_b=row(w["ssd_conv_b"][l]),
        ssd_dt_bias=lane_pad(w["ssd_dt_bias"][l]), ssd_a_log=lane_pad(w["ssd_a_log"][l]),
        ssd_d_ch=row(w["ssd_d"][l][head_of_channel]), ssd_norm_w=row(w["ssd_norm_w"][l]),
        ssd_expand=(jnp.arange(LANES)[:, None] == head_of_channel[None, :]).astype(F32),
        att_table=w["att_rel_bias"][l],
        lru_conv_w=w["lru_conv_w"][l], lru_conv_b=row(w["lru_conv_b"][l]),
        lru_wa_d=_block_diag(w["lru_wa"][l]).astype(BF16), lru_ba=row(w["lru_ba"][l]),
        lru_wx_d=_block_diag(w["lru_wx"][l]).astype(BF16), lru_bx=row(w["lru_bx"][l]),
        lru_lambda=row(w["lru_lambda"][l]),
        gate_bias=row(w["gate_bias"][l]),
        w_ssd_proj=w["w_ssd_proj"][l].astype(BF16), w_att_proj=w["w_att_proj"][l].astype(BF16),
        w_lru_proj=w["w_lru_proj"][l].astype(BF16), w_out=w["w_out"][l].astype(BF16),
        ln1_g=row(w["ln1_g"][l]), ln1_b=row(w["ln1_b"][l]),
        router_w_t=w["router_w"][l].T, router_bias=w["router_bias"][l].reshape(N_EXPERTS, 1),
        layer=l, exp_wg=w["exp_wg_all"], exp_wu=w["exp_wu_all"], exp_wd=w["exp_wd_all"],
        sh_wgu=jnp.concatenate([w["sh_w_gate"][l], w["sh_w_up"][l]], axis=-1).astype(BF16),
        sh_wd=w["sh_w_down"][l].astype(BF16),
        ln2_g=row(w["ln2_g"][l]), ln2_b=row(w["ln2_b"][l]),
    )


def _heads_state(h_t, nb):
    return jnp.swapaxes(h_t, 1, 2).reshape(nb, SSD_HEADS, SSD_HEAD_DIM, SSD_STATE)


def _layer(x, xb, st, p, geom, l):
    bp, tp, bs, ts = geom
    n_p = bp * tp
    ncp = tp // CHUNK
    keep = min(ATT_LEFT, tp)
    proj = _matmul(xb, p["w_main"], 2 * D_MODEL, f"in_proj_{l}")
    qkv = _matmul(xb, p["w_qkv"], D_MODEL, f"qkv_proj_{l}", BF16)
    proj_dt = _matmul(xb, p["w_dt"], LANES, f"dt_proj_{l}")
    xb_keep = jnp.concatenate([xb[(b + 1) * tp - keep:(b + 1) * tp] for b in range(bp)] + [xb[n_p:]], axis=0)
    kv_keep = _matmul(xb_keep, p["w_kv"], D_MODEL, f"kv_keep_{l}")

    zeros = lambda *s: jnp.zeros(s, F32)
    ssd_p, pconv, ph = _ssd_call(proj, proj_dt, zeros(bp, CONV_WIDTH - 1, SSD_CONV_DIM),
                                 zeros(bp, SSD_STATE, D_MODEL), p, bp, ncp, 0, f"ssd_p_{l}")
    h0_t = jnp.swapaxes(st["state_ssd"].reshape(bs, D_MODEL, SSD_STATE), 1, 2)
    ssd_s, sconv, sh = _ssd_call(proj, proj_dt, st["cache_ssd_conv"], h0_t, p, bs, ts // CHUNK, n_p // CHUNK,
                                 f"ssd_s_{l}")

    att_rows = _tile(tp, (ATT_LEFT,))
    bias_p = _band_bias(p["att_table"], min(att_rows, ATT_SUB))
    att_p = _att_prompt_call(qkv, bias_p, bp, tp // att_rows, att_rows, f"att_p_{l}")
    bias_s = bias_p[:, :ts, :ts + ATT_LEFT]
    att_s = _att_sample_call(qkv, st["cache_att_kt"], st["cache_att_vt"], bias_s, bs, ts, n_p // ts, l,
                             f"att_s_{l}")

    lru_rows = _tile(tp, (256, 128, 64))
    lru_p, plc, plh = _lru_call(proj, zeros(bp, CONV_WIDTH - 1, D_MODEL), zeros(bp, 1, D_MODEL), p, bp,
                                tp // lru_rows, lru_rows, 0, f"lru_p_{l}")
    lru_s, slc, slh = _lru_call(proj, st["cache_lru_conv"], st["state_lru"].reshape(bs, 1, D_MODEL), p, bs,
                                1, ts, n_p // ts, f"lru_s_{l}")

    x1, x1b = _merge_call(((ssd_p, ssd_s), (att_p, att_s), (lru_p, lru_s)), proj, x, p, f"merge_{l}")
    gates_t = _router_call(x1, p, f"router_{l}")
    gates = jnp.pad(gates_t.T, ((0, 0), (0, LANES - N_EXPERTS)))
    x2, x2b = _moe_call(x1b, x1, gates, p, f"moe_{l}", n_p if l == DEPTH - 1 else None)

    def kv(col, rows0, nb, t):
        return kv_keep[rows0:rows0 + nb * t, col * D_MODEL:(col + 1) * D_MODEL].reshape(
            nb, t, ATT_HEADS, ATT_HEAD_DIM)

    states = dict(
        p_ssd_conv=pconv, s_ssd_conv=sconv, p_ssd_state=_heads_state(ph, bp), s_ssd_state=_heads_state(sh, bs),
        p_att_k=kv(0, 0, bp, keep), s_att_k=kv(0, bp * keep, bs, ts),
        p_att_v=kv(1, 0, bp, keep), s_att_v=kv(1, bp * keep, bs, ts),
        p_lru_conv=plc, s_lru_conv=slc, p_lru_state=plh.reshape(bp, D_MODEL), s_lru_state=slh.reshape(bs, D_MODEL))
    return x2, x2b, states


def kernel(x_prompt, x_sample, cache_ssd_conv, state_ssd, cache_att_k, cache_att_v, cache_lru_conv, state_lru, w_in, ssd_conv_w, ssd_conv_b, ssd_dt_bias, ssd_a_log, ssd_d, ssd_norm_w, att_rel_bias, lru_conv_w, lru_conv_b, lru_wa, lru_ba, lru_wx, lru_bx, lru_lambda, gate_bias, w_ssd_proj, w_att_proj, w_lru_proj, w_out, ln1_g, ln1_b, router_w, router_bias, exp_w_gate, exp_w_up, exp_w_down, sh_w_gate, sh_w_up, sh_w_down, ln2_g, ln2_b):
    w = dict(w_in=w_in, ssd_conv_w=ssd_conv_w, ssd_conv_b=ssd_conv_b, ssd_dt_bias=ssd_dt_bias, ssd_a_log=ssd_a_log,
             ssd_d=ssd_d, ssd_norm_w=ssd_norm_w, att_rel_bias=att_rel_bias, lru_conv_w=lru_conv_w,
             lru_conv_b=lru_conv_b, lru_wa=lru_wa, lru_ba=lru_ba, lru_wx=lru_wx, lru_bx=lru_bx,
             lru_lambda=lru_lambda, gate_bias=gate_bias, w_ssd_proj=w_ssd_proj, w_att_proj=w_att_proj,
             w_lru_proj=w_lru_proj, w_out=w_out, ln1_g=ln1_g, ln1_b=ln1_b, router_w=router_w,
             router_bias=router_bias, exp_w_gate=exp_w_gate, exp_w_up=exp_w_up, exp_w_down=exp_w_down,
             sh_w_gate=sh_w_gate, sh_w_up=sh_w_up, sh_w_down=sh_w_down, ln2_g=ln2_g, ln2_b=ln2_b)
    bp, tp, _ = x_prompt.shape
    bs, ts, _ = x_sample.shape
    n_p = bp * tp
    x = jnp.concatenate([x_prompt.reshape(n_p, D_MODEL), x_sample.reshape(bs * ts, D_MODEL)], axis=0)
    xb = x.astype(BF16)
    w.update(_prep_stacks(w))
    cache_att_kt = jnp.transpose(cache_att_k, (0, 1, 3, 4, 2))
    cache_att_vt = jnp.transpose(cache_att_v, (0, 1, 3, 4, 2))
    per_layer = []
    for l in range(DEPTH):
        st = dict(cache_ssd_conv=cache_ssd_conv[l], state_ssd=state_ssd[l], cache_att_kt=cache_att_kt,
                  cache_att_vt=cache_att_vt, cache_lru_conv=cache_lru_conv[l], state_lru=state_lru[l])
        x, xb, states = _layer(x, xb, st, _prep_layer(w, l), (bp, tp, bs, ts), l)
        per_layer.append(states)
    y_prompt, y_sample = x, xb
    stack = lambda k: jnp.stack([s[k] for s in per_layer], axis=0)
    return (y_prompt.reshape(bp, tp, D_MODEL), y_sample.reshape(bs, ts, D_MODEL),
            stack("p_ssd_conv"), stack("s_ssd_conv"), stack("p_ssd_state"), stack("s_ssd_state"),
            stack("p_att_k"), stack("s_att_k"), stack("p_att_v"), stack("s_att_v"),
            stack("p_lru_conv"), stack("s_lru_conv"), stack("p_lru_state"), stack("s_lru_state"))
```

```python
import functools
import math

import jax
import jax.numpy as jnp
import numpy as np
from jax import lax
from jax.experimental import pallas as pl
from jax.experimental.pallas import tpu as pltpu

F32 = jnp.float32
BF16 = jnp.bfloat16
HIGHEST = lax.Precision.HIGHEST

D_MODEL = 1024
DEPTH = 2
CHUNK = 64
CONV_WIDTH = 4
SSD_HEADS = 16
SSD_HEAD_DIM = 64
SSD_GROUPS = 4
SSD_STATE = 128
SSD_CONV_DIM = D_MODEL + 2 * SSD_GROUPS * SSD_STATE
ATT_HEADS = 16
ATT_HEAD_DIM = 64
ATT_LEFT = 8 * CHUNK
ATT_BAND = ATT_LEFT + CHUNK
REL_CLIP = 128
ATT_SCALE = ATT_HEAD_DIM ** -0.5
LRU_BLOCKS = 16
LRU_C = 8.0
N_BRANCH = 3
N_EXPERTS = 64
N_GROUPS = 8
GROUP_SIZE = N_EXPERTS // N_GROUPS
TOPK_GROUPS = 4
TOP_K = 8
EXPERT_HIDDEN = D_MODEL // 4
ROUTED_SCALE = 2.5
DN_ALPHA = (2.0 * DEPTH) ** 0.25
LN_EPS = 1e-5
RMS_EPS = 1e-5

LANES = 128
SUBLANES = 8
CONV_PAD = 8

COL_G, COL_Z, COL_XBC, COL_XL, COL_YL = 0, 3, 2, 6, 7
COL_Q, COL_K, COL_V = 0, 1, 2
ATT_SUB = 256
MOE_GROUP = 8
SSD_CHUNKS_PER_STEP = 8
NEG_BIG = -1e30


def _params(semantics, vmem_mb):
    return pltpu.CompilerParams(dimension_semantics=semantics, vmem_limit_bytes=vmem_mb * 2 ** 20)


def _tile(n, candidates):
    for c in candidates:
        if n % c == 0:
            return c
    raise ValueError(f"no tile for {n}")


def _nn(a, b, precision=None):
    return jnp.dot(a, b, preferred_element_type=F32, precision=precision)


def _nt(a, b, precision=None):
    return lax.dot_general(a, b, (((1,), (1,)), ((), ())), preferred_element_type=F32, precision=precision)


def _tn(a, b):
    return lax.dot_general(a, b, (((0,), (0,)), ((), ())), preferred_element_type=F32)


def _sigmoid(x):
    return 0.5 * jnp.tanh(0.5 * x) + 0.5


def _silu(x):
    return x * _sigmoid(x)


def _softplus(x):
    return jnp.maximum(x, 0.0) + jnp.log1p(jnp.exp(-jnp.abs(x)))


def _layer_norm(y, g, b):
    mu = jnp.mean(y, axis=-1, keepdims=True)
    d = y - mu
    var = jnp.mean(d * d, axis=-1, keepdims=True)
    return d * lax.rsqrt(var + LN_EPS) * g + b


def _const_spec(shape):
    nd = len(shape)
    return pl.BlockSpec(shape, lambda *_: (0,) * nd)


def _mm_body(a_ref, b_ref, o_ref):
    o_ref[...] = _nn(a_ref[...], b_ref[...]).astype(o_ref.dtype)


def _matmul(a, b, tn, name, out_dtype=F32):
    m, k = a.shape
    n = b.shape[1]
    tm = _tile(m, (1024, 512, 256, 128))
    return pl.pallas_call(
        _mm_body,
        grid=(m // tm, n // tn),
        in_specs=[pl.BlockSpec((tm, k), lambda i, j: (i, 0)), pl.BlockSpec((k, tn), lambda i, j: (0, j))],
        out_specs=pl.BlockSpec((tm, tn), lambda i, j: (i, j)),
        out_shape=jax.ShapeDtypeStruct((m, n), out_dtype),
        compiler_params=_params(("parallel", "arbitrary"), 40),
        name=name,
    )(a, b)


def _conv_block(xp_scr, x_ref, cw_ref, cb_ref, rows):
    lo = CONV_PAD - (CONV_WIDTH - 1)
    x = x_ref[...]
    carried = xp_scr[lo:CONV_PAD, :]
    first = lax.broadcasted_iota(jnp.int32, (SUBLANES, 1), 0) == 0
    delayed = [x]
    for k in range(1, CONV_WIDTH):
        rolled = pltpu.roll(delayed[-1], 1, 0)
        patch = jnp.where(first, carried[CONV_WIDTH - 1 - k:CONV_WIDTH - k, :], rolled[0:SUBLANES, :])
        delayed.append(jnp.concatenate([patch, rolled[SUBLANES:, :]], axis=0))
    y = cb_ref[...] + delayed[CONV_WIDTH - 1] * cw_ref[0:1, :]
    for k in range(1, CONV_WIDTH):
        y = y + delayed[CONV_WIDTH - 1 - k] * cw_ref[k:k + 1, :]
    tail = x[rows - (CONV_WIDTH - 1):rows, :]
    xp_scr[lo:CONV_PAD, :] = tail
    return y, tail


def _split3(x):
    hi = x.astype(BF16)
    r1 = x - hi.astype(F32)
    mid = r1.astype(BF16)
    return hi, mid, (r1 - mid.astype(F32)).astype(BF16)


def _ssd_body(z_ref, xbc_ref, dt_ref, buf0_ref, h0_ref, cw_ref, cb_ref, dtb_ref, alog_ref, dch_ref, nw_ref,
              expand_ref, y_ref, bufo_ref, ho_ref, xp_scr, h_scr, *, nc, cps):
    c = pl.program_id(1)
    lo = CONV_PAD - (CONV_WIDTH - 1)

    @pl.when(c == 0)
    def _():
        xp_scr[lo:CONV_PAD, :] = buf0_ref[0]
        h_scr[...] = h0_ref[0]

    conv, tail = _conv_block(xp_scr, xbc_ref, cw_ref, cb_ref, cps * CHUNK)
    bufo_ref[0] = tail
    xbc_all = _silu(conv)
    dt_all = _softplus(dt_ref[...] + dtb_ref[...])
    da_all = dt_all * (-jnp.exp(alog_ref[...]))
    row = lax.broadcasted_iota(jnp.int32, (CHUNK, CHUNK), 0)
    col = lax.broadcasted_iota(jnp.int32, (CHUNK, CHUNK), 1)
    causal = row >= col
    tri = causal.astype(BF16)
    eye = (lax.broadcasted_iota(jnp.int32, (LANES, LANES), 0)
           == lax.broadcasted_iota(jnp.int32, (LANES, LANES), 1)).astype(BF16)
    for ci in range(cps):
        rs = slice(ci * CHUNK, (ci + 1) * CHUNK)
        _ssd_chunk(xbc_all[rs], dt_all[rs], da_all[rs], z_ref[rs, :], tri, eye, causal, expand_ref, dch_ref,
                   nw_ref, h_scr, y_ref, rs)

    @pl.when(c == nc - 1)
    def _():
        ho_ref[0] = h_scr[...]


def _ssd_chunk(xbc, dt, da, z, tri, eye, causal, expand_ref, dch_ref, nw_ref, h_scr, y_ref, rs):
    xs = xbc[:, :D_MODEL]
    bm = xbc[:, D_MODEL:D_MODEL + SSD_GROUPS * SSD_STATE].astype(BF16)
    cm = xbc[:, D_MODEL + SSD_GROUPS * SSD_STATE:].astype(BF16)
    cs = sum(_nn(tri, piece) for piece in _split3(da))
    cs_last = cs[CHUNK - 1:CHUNK, :]
    to_end = jnp.exp(cs_last - cs)
    ecs = jnp.exp(cs)
    expand = expand_ref[...]
    wide = sum(_nn(piece, expand) for piece in _split3(jnp.concatenate([dt, to_end, ecs], axis=0)))
    dt_ch = wide[0:CHUNK]
    to_end_ch = wide[CHUNK:2 * CHUNK]
    ecs_ch = wide[2 * CHUNK:3 * CHUNK]
    dec_ch = ecs_ch[CHUNK - 1:CHUNK, :]
    cs_t = sum(_nt(eye, piece) for piece in _split3(cs))

    xdt = xs * dt_ch
    xdt_b = xdt.astype(BF16)
    xw_b = (xdt * to_end_ch).astype(BF16)
    gw = SSD_HEADS // SSD_GROUPS * SSD_HEAD_DIM
    y_parts = []
    for g in range(SSD_GROUPS):
        bg = bm[:, g * SSD_STATE:(g + 1) * SSD_STATE]
        cg = cm[:, g * SSD_STATE:(g + 1) * SSD_STATE]
        gsl = slice(g * gw, (g + 1) * gw)
        cb = _nt(cg, bg)
        h_prev = h_scr[:, gsl]
        y_off = _nn(cg, h_prev.astype(BF16)) * ecs_ch[:, gsl]
        st = _tn(bg, xw_b[:, gsl])
        h_scr[:, gsl] = h_prev * dec_ch[:, gsl] + st
        diag = []
        for r in range(SSD_HEADS // SSD_GROUPS):
            h = g * (SSD_HEADS // SSD_GROUPS) + r
            seg = cs[:, h:h + 1] - cs_t[h:h + 1, :CHUNK]
            decay = jnp.where(causal, jnp.exp(jnp.where(causal, seg, 0.0)), 0.0)
            sc = (cb * decay).astype(BF16)
            diag.append(_nn(sc, xdt_b[:, h * SSD_HEAD_DIM:(h + 1) * SSD_HEAD_DIM]))
        y_parts.append(jnp.concatenate(diag, axis=-1) + y_off)
    y = jnp.concatenate(y_parts, axis=-1) + xs * dch_ref[...]
    v = y * _silu(z)
    outs = []
    for g in range(SSD_GROUPS):
        vg = v[:, g * gw:(g + 1) * gw]
        ms = jnp.mean(vg * vg, axis=-1, keepdims=True)
        outs.append(vg * lax.rsqrt(ms + RMS_EPS))
    y_ref[rs, :] = (jnp.concatenate(outs, axis=-1) * nw_ref[...]).astype(y_ref.dtype)


def _ssd_call(proj, proj_dt, buf0, h0_t, p, nb, nchunks, blk0, name):
    cps = _tile(nchunks, (SSD_CHUNKS_PER_STEP, 1))
    nc = nchunks // cps
    rows = cps * CHUNK
    row = lambda b, c: blk0 // cps + b * nc + c
    in_specs = [
        pl.BlockSpec((rows, D_MODEL), lambda b, c: (row(b, c), COL_Z)),
        pl.BlockSpec((rows, SSD_CONV_DIM), lambda b, c: (row(b, c), COL_XBC)),
        pl.BlockSpec((rows, LANES), lambda b, c: (row(b, c), 0)),
        pl.BlockSpec((1, CONV_WIDTH - 1, SSD_CONV_DIM), lambda b, c: (b, 0, 0)),
        pl.BlockSpec((1, SSD_STATE, D_MODEL), lambda b, c: (b, 0, 0)),
        _const_spec((CONV_WIDTH, SSD_CONV_DIM)),
        _const_spec((1, SSD_CONV_DIM)),
        _const_spec((1, LANES)),
        _const_spec((1, LANES)),
        _const_spec((1, D_MODEL)),
        _const_spec((1, D_MODEL)),
        _const_spec((LANES, D_MODEL)),
    ]
    out_specs = [
        pl.BlockSpec((rows, D_MODEL), lambda b, c: (b * nc + c, 0)),
        pl.BlockSpec((1, CONV_WIDTH - 1, SSD_CONV_DIM), lambda b, c: (b, 0, 0)),
        pl.BlockSpec((1, SSD_STATE, D_MODEL), lambda b, c: (b, 0, 0)),
    ]
    out_shape = [
        jax.ShapeDtypeStruct((nb * nchunks * CHUNK, D_MODEL), BF16),
        jax.ShapeDtypeStruct((nb, CONV_WIDTH - 1, SSD_CONV_DIM), F32),
        jax.ShapeDtypeStruct((nb, SSD_STATE, D_MODEL), F32),
    ]
    return pl.pallas_call(
        functools.partial(_ssd_body, nc=nc, cps=cps),
        grid=(nb, nc),
        in_specs=in_specs,
        out_specs=out_specs,
        out_shape=out_shape,
        scratch_shapes=[pltpu.VMEM((CONV_PAD, SSD_CONV_DIM), F32), pltpu.VMEM((SSD_STATE, D_MODEL), F32)],
        compiler_params=_params(("parallel", "arbitrary"), 40),
        name=name,
    )(proj, proj, proj_dt, buf0, h0_t, p["ssd_conv_w"], p["ssd_conv_b"], p["ssd_dt_bias"], p["ssd_a_log"],
      p["ssd_d_ch"], p["ssd_norm_w"], p["ssd_expand"])


def _head_pair_queries(q2):
    first = lax.broadcasted_iota(jnp.int32, (1, LANES), 1) < ATT_HEAD_DIM
    zero = jnp.zeros_like(q2)
    return first, (jnp.where(first, q2, zero), jnp.where(first, zero, q2))


def _att_prompt_body(q_ref, kp_ref, kc_ref, vp_ref, vc_ref, bias_ref, o_ref, k_scr, v_scr, *, rows, sub):
    t = pl.program_id(1)
    k_scr[0:ATT_LEFT, :] = kp_ref[...]
    v_scr[0:ATT_LEFT, :] = vp_ref[...]
    k_scr[ATT_LEFT:ATT_LEFT + rows, :] = kc_ref[...]
    v_scr[ATT_LEFT:ATT_LEFT + rows, :] = vc_ref[...]
    win = sub + ATT_LEFT

    for s in range(rows // sub):
        r0 = s * sub
        pos = lax.broadcasted_iota(jnp.int32, (1, win), 1) + r0
        dead = jnp.logical_and(pos < ATT_LEFT, t == 0)
        for hp in range(ATT_HEADS // 2):
            ls = slice(hp * LANES, (hp + 1) * LANES)
            first, queries = _head_pair_queries(q_ref[r0:r0 + sub, ls])
            k2 = k_scr[r0:r0 + win, ls]
            v2 = v_scr[r0:r0 + win, ls]
            outs = []
            for h, qh in zip((2 * hp, 2 * hp + 1), queries):
                sc = jnp.where(dead, NEG_BIG, _nt(qh, k2) + bias_ref[h])
                m = jnp.max(sc, axis=-1, keepdims=True)
                e = jnp.exp(sc - m)
                den = jnp.sum(e, axis=-1, keepdims=True)
                outs.append(_nn(e.astype(BF16), v2) / den)
            o_ref[r0:r0 + sub, ls] = jnp.where(first, outs[0], outs[1]).astype(o_ref.dtype)


def _att_prompt_call(qkv, bias, nb, nt, rows, name):
    sub = min(rows, ATT_SUB)
    cur = lambda col: pl.BlockSpec((rows, D_MODEL), lambda b, t: (b * nt + t, col))
    prev = lambda col: pl.BlockSpec((ATT_LEFT, D_MODEL), lambda b, t: (b * nt + jnp.maximum(t - 1, 0), col))
    return pl.pallas_call(
        functools.partial(_att_prompt_body, rows=rows, sub=sub),
        grid=(nb, nt),
        in_specs=[cur(COL_Q), prev(COL_K), cur(COL_K), prev(COL_V), cur(COL_V),
                  _const_spec((ATT_HEADS, sub, sub + ATT_LEFT))],
        out_specs=pl.BlockSpec((rows, D_MODEL), lambda b, t: (b * nt + t, 0)),
        out_shape=jax.ShapeDtypeStruct((nb * nt * rows, D_MODEL), BF16),
        scratch_shapes=[pltpu.VMEM((ATT_LEFT + rows, D_MODEL), BF16), pltpu.VMEM((ATT_LEFT + rows, D_MODEL), BF16)],
        compiler_params=_params(("parallel", "arbitrary"), 56),
        name=name,
    )(qkv, qkv, qkv, qkv, qkv, bias)


def _att_sample_body(q_ref, kn_ref, vn_ref, kt_ref, vt_ref, bias_ref, o_ref):
    for hp in range(ATT_HEADS // 2):
        ls = slice(hp * LANES, (hp + 1) * LANES)
        first, queries = _head_pair_queries(q_ref[:, ls])
        kn2 = kn_ref[:, ls]
        vn2 = vn_ref[:, ls]
        kt2 = jnp.concatenate([kt_ref[2 * hp], kt_ref[2 * hp + 1]], axis=0).astype(BF16)
        vt2 = jnp.concatenate([vt_ref[2 * hp], vt_ref[2 * hp + 1]], axis=0).astype(BF16)
        outs = []
        for h, qh in zip((2 * hp, 2 * hp + 1), queries):
            bias = bias_ref[h]
            s_old = _nn(qh, kt2) + bias[:, :ATT_LEFT]
            s_new = _nt(qh, kn2) + bias[:, ATT_LEFT:]
            m = jnp.maximum(jnp.max(s_old, axis=-1, keepdims=True), jnp.max(s_new, axis=-1, keepdims=True))
            e_old = jnp.exp(s_old - m)
            e_new = jnp.exp(s_new - m)
            den = jnp.sum(e_old, axis=-1, keepdims=True) + jnp.sum(e_new, axis=-1, keepdims=True)
            outs.append((_nt(e_old.astype(BF16), vt2) + _nn(e_new.astype(BF16), vn2)) / den)
        o_ref[:, ls] = jnp.where(first, outs[0], outs[1]).astype(o_ref.dtype)


def _att_sample_call(qkv, k_cache_t, v_cache_t, bias, nb, rows, blk0, layer, name):
    cur = lambda col: pl.BlockSpec((rows, D_MODEL), lambda b: (blk0 + b, col))
    cache = pl.BlockSpec((None, None, ATT_HEADS, ATT_HEAD_DIM, ATT_LEFT), lambda b: (layer, b, 0, 0, 0))
    return pl.pallas_call(
        _att_sample_body,
        grid=(nb,),
        in_specs=[cur(COL_Q), cur(COL_K), cur(COL_V), cache, cache, _const_spec((ATT_HEADS, rows, rows + ATT_LEFT))],
        out_specs=pl.BlockSpec((rows, D_MODEL), lambda b: (b, 0)),
        out_shape=jax.ShapeDtypeStruct((nb * rows, D_MODEL), BF16),
        compiler_params=_params(("parallel",), 40),
        name=name,
    )(qkv, qkv, qkv, k_cache_t, v_cache_t, bias)


def _band_bias(table, sub):
    win = sub + ATT_LEFT
    span = -(-(sub - 1 + win) // LANES) * LANES
    idx = [min(max(win - 1 - m, -REL_CLIP), REL_CLIP) + REL_CLIP for m in range(span)]
    vec = table[:, np.array(idx, np.int32)].reshape(ATT_HEADS, 1, span)
    return pl.pallas_call(
        functools.partial(_band_bias_body, sub=sub, win=win),
        grid=(ATT_HEADS,),
        in_specs=[pl.BlockSpec((1, 1, span), lambda h: (h, 0, 0))],
        out_specs=pl.BlockSpec((1, sub, win), lambda h: (h, 0, 0)),
        out_shape=jax.ShapeDtypeStruct((ATT_HEADS, sub, win), F32),
        compiler_params=_params(("parallel",), 32),
        name="band_bias",
    )(vec)


def _band_bias_body(vec_ref, o_ref, *, sub, win):
    span = vec_ref.shape[-1]
    toep = pltpu.roll(jnp.broadcast_to(vec_ref[0], (sub, span)), span - (sub - 1), 1, stride=1, stride_axis=0)
    r = lax.broadcasted_iota(jnp.int32, (sub, win), 0)
    j = lax.broadcasted_iota(jnp.int32, (sub, win), 1) - (r - jnp.bitwise_and(r, CHUNK - 1))
    o_ref[0] = jnp.where(jnp.logical_and(j >= 0, j < ATT_BAND), toep[:, :win], NEG_BIG)


def _lru_scan(a, u, carry):
    rows, width = a.shape
    a3 = a.reshape(rows // SUBLANES, SUBLANES, width)
    u3 = u.reshape(rows // SUBLANES, SUBLANES, width)
    pos = lax.broadcasted_iota(jnp.int32, (1, SUBLANES, 1), 1)
    d = 1
    while d < SUBLANES:
        a_sh = pltpu.roll(a3, d, 1)
        u_sh = pltpu.roll(u3, d, 1)
        m = pos >= d
        u3 = jnp.where(m, a3 * u_sh + u3, u3)
        a3 = jnp.where(m, a3 * a_sh, a3)
        d *= 2
    out = []
    for g in range(rows // SUBLANES):
        h = a3[g] * carry + u3[g]
        carry = h[SUBLANES - 1:SUBLANES, :]
        out.append(h)
    return jnp.concatenate(out, axis=0), carry


def _gelu_tanh(x):
    return 0.5 * x * (1.0 + jnp.tanh(math.sqrt(2.0 / math.pi) * (x + 0.044715 * (x * x * x))))


def _lru_body(xl_ref, yl_ref, buf0_ref, h0_ref, cw_ref, cb_ref, wa_ref, ba_ref, wx_ref, bx_ref, lam_ref,
              o_ref, bufo_ref, ho_ref, xp_scr, carry_scr, *, rows):
    c = pl.program_id(1)
    lo = CONV_PAD - (CONV_WIDTH - 1)

    @pl.when(c == 0)
    def _():
        xp_scr[lo:CONV_PAD, :] = buf0_ref[0]
        carry_scr[...] = h0_ref[0]

    xc, tail = _conv_block(xp_scr, xl_ref, cw_ref, cb_ref, rows)
    bufo_ref[0] = tail
    xcb = xc.astype(BF16)
    gate_r = _sigmoid(_nn(xcb, wa_ref[...]) + ba_ref[...])
    gate_i = _sigmoid(_nn(xcb, wx_ref[...]) + bx_ref[...])
    log_a = LRU_C * gate_r * (-_softplus(-lam_ref[...]))
    a = jnp.exp(log_a)
    u = jnp.sqrt(1.0 - a * a) * (gate_i * xc)
    h, carry = _lru_scan(a, u, carry_scr[...])
    o_ref[...] = (h * _gelu_tanh(yl_ref[...])).astype(o_ref.dtype)
    carry_scr[...] = carry
    ho_ref[0] = carry


def _lru_call(proj, buf0, h0, p, nb, nc, rows, blk0, name):
    row = lambda b, c: blk0 + b * nc + c
    in_specs = [
        pl.BlockSpec((rows, D_MODEL), lambda b, c: (row(b, c), COL_XL)),
        pl.BlockSpec((rows, D_MODEL), lambda b, c: (row(b, c), COL_YL)),
        pl.BlockSpec((1, CONV_WIDTH - 1, D_MODEL), lambda b, c: (b, 0, 0)),
        pl.BlockSpec((1, 1, D_MODEL), lambda b, c: (b, 0, 0)),
        _const_spec((CONV_WIDTH, D_MODEL)),
        _const_spec((1, D_MODEL)),
        _const_spec((D_MODEL, D_MODEL)),
        _const_spec((1, D_MODEL)),
        _const_spec((D_MODEL, D_MODEL)),
        _const_spec((1, D_MODEL)),
        _const_spec((1, D_MODEL)),
    ]
    out_specs = [
        pl.BlockSpec((rows, D_MODEL), lambda b, c: (b * nc + c, 0)),
        pl.BlockSpec((1, CONV_WIDTH - 1, D_MODEL), lambda b, c: (b, 0, 0)),
        pl.BlockSpec((1, 1, D_MODEL), lambda b, c: (b, 0, 0)),
    ]
    out_shape = [
        jax.ShapeDtypeStruct((nb * nc * rows, D_MODEL), BF16),
        jax.ShapeDtypeStruct((nb, CONV_WIDTH - 1, D_MODEL), F32),
        jax.ShapeDtypeStruct((nb, 1, D_MODEL), F32),
    ]
    return pl.pallas_call(
        functools.partial(_lru_body, rows=rows),
        grid=(nb, nc),
        in_specs=in_specs,
        out_specs=out_specs,
        out_shape=out_shape,
        scratch_shapes=[pltpu.VMEM((CONV_PAD, D_MODEL), F32), pltpu.VMEM((1, D_MODEL), F32)],
        compiler_params=_params(("parallel", "arbitrary"), 48),
        name=name,
    )(proj, proj, buf0, h0, p["lru_conv_w"], p["lru_conv_b"], p["lru_wa_d"], p["lru_ba"], p["lru_wx_d"],
      p["lru_bx"], p["lru_lambda"])


def _merge_body(sp_ref, ss_ref, ap_ref, as_ref, lp_ref, ls_ref, g_ref, x_ref, ws_ref, wa_ref, wl_ref, wo_ref,
                gb_ref, lg_ref, lb_ref, o_ref, ob_ref, *, prompt_tiles):
    in_prompt = pl.program_id(0) < prompt_tiles
    branch = lambda p_ref, s_ref: jnp.where(in_prompt, p_ref[...], s_ref[...])
    g = _sigmoid(g_ref[...] + gb_ref[...])
    m = (g[:, 0:D_MODEL] * _nn(branch(sp_ref, ss_ref), ws_ref[...])
         + g[:, D_MODEL:2 * D_MODEL] * _nn(branch(ap_ref, as_ref), wa_ref[...])
         + g[:, 2 * D_MODEL:3 * D_MODEL] * _nn(branch(lp_ref, ls_ref), wl_ref[...]))
    y = DN_ALPHA * x_ref[...] + _nn(m.astype(BF16), wo_ref[...])
    x1 = _layer_norm(y, lg_ref[...], lb_ref[...])
    o_ref[...] = x1
    ob_ref[...] = x1.astype(BF16)


def _merge_call(branches, proj, x, p, name):
    n = x.shape[0]
    n_p = branches[0][0].shape[0]
    tm = _tile(math.gcd(n_p, n - n_p), (512, 256, 128))
    pt = n_p // tm
    rowblk = lambda w: pl.BlockSpec((tm, w), lambda i: (i, 0))
    pblk = pl.BlockSpec((tm, D_MODEL), lambda i: (jnp.minimum(i, pt - 1), 0))
    sblk = pl.BlockSpec((tm, D_MODEL), lambda i: (jnp.maximum(i - pt, 0), 0))
    wspec = pl.BlockSpec((D_MODEL, D_MODEL), lambda i: (0, 0), pipeline_mode=pl.Buffered(1))
    return pl.pallas_call(
        functools.partial(_merge_body, prompt_tiles=pt),
        grid=(n // tm,),
        in_specs=[pblk, sblk, pblk, sblk, pblk, sblk,
                  pl.BlockSpec((tm, N_BRANCH * D_MODEL), lambda i: (i, COL_G)), rowblk(D_MODEL),
                  wspec, wspec, wspec, wspec,
                  _const_spec((1, N_BRANCH * D_MODEL)), _const_spec((1, D_MODEL)), _const_spec((1, D_MODEL))],
        out_specs=[rowblk(D_MODEL), rowblk(D_MODEL)],
        out_shape=[jax.ShapeDtypeStruct((n, D_MODEL), F32), jax.ShapeDtypeStruct((n, D_MODEL), BF16)],
        compiler_params=_params(("parallel",), 58),
        name=name,
    )(*branches[0], *branches[1], *branches[2], proj, x, p["w_ssd_proj"], p["w_att_proj"], p["w_lru_proj"],
      p["w_out"], p["gate_bias"], p["ln1_g"], p["ln1_b"])


def _first_max(x, idx, big):
    m = jnp.max(x, axis=0, keepdims=True)
    first = jnp.min(jnp.where(x == m, idx, big), axis=0, keepdims=True)
    return m, idx == first


def _router_body(x_ref, rwt_ref, rb_ref, o_ref):
    tm = x_ref.shape[0]
    logits = _nt(rwt_ref[...], x_ref[...], HIGHEST)
    scores = _sigmoid(logits)
    sel = scores + rb_ref[...]
    sub = lax.broadcasted_iota(jnp.int32, (GROUP_SIZE, tm), 0)
    gscore = jnp.zeros((N_GROUPS, tm), F32)
    for g in range(N_GROUPS):
        sg = sel[g * GROUP_SIZE:(g + 1) * GROUP_SIZE, :]
        m1, hit = _first_max(sg, sub, GROUP_SIZE)
        m2 = jnp.max(jnp.where(hit, -jnp.inf, sg), axis=0, keepdims=True)
        gscore = jnp.where(sub == g, m1 + m2, gscore)
    gsel = jnp.zeros((N_GROUPS, tm), F32)
    for _ in range(TOPK_GROUPS):
        _, hit = _first_max(gscore, sub, N_GROUPS)
        gsel = jnp.where(hit, 1.0, gsel)
        gscore = jnp.where(hit, -jnp.inf, gscore)
    masked = jnp.concatenate(
        [jnp.where(gsel[g:g + 1, :] > 0.0, sel[g * GROUP_SIZE:(g + 1) * GROUP_SIZE, :], -jnp.inf)
         for g in range(N_GROUPS)], axis=0)
    eidx = lax.broadcasted_iota(jnp.int32, (N_EXPERTS, tm), 0)
    chosen = jnp.zeros((N_EXPERTS, tm), F32)
    for _ in range(TOP_K):
        _, hit = _first_max(masked, eidx, N_EXPERTS)
        chosen = jnp.where(hit, 1.0, chosen)
        masked = jnp.where(hit, -jnp.inf, masked)
    w = chosen * scores
    o_ref[...] = w / jnp.sum(w, axis=0, keepdims=True) * ROUTED_SCALE


def _router_call(x1, p, name):
    n = x1.shape[0]
    tm = _tile(n, (512, 256, 128))
    return pl.pallas_call(
        _router_body,
        grid=(n // tm,),
        in_specs=[pl.BlockSpec((tm, D_MODEL), lambda i: (i, 0)), _const_spec((N_EXPERTS, D_MODEL)),
                  _const_spec((N_EXPERTS, 1))],
        out_specs=pl.BlockSpec((N_EXPERTS, tm), lambda i: (0, i)),
        out_shape=jax.ShapeDtypeStruct((N_EXPERTS, n), F32),
        compiler_params=_params(("parallel",), 32),
        name=name,
    )(x1, p["router_w_t"], p["router_bias"])


def _swiglu(xb, wgu, width):
    gu = _nn(xb, wgu)
    return _silu(gu[:, :width]) * gu[:, width:]


def _moe_body(xb_ref, x_ref, g_ref, wg_ref, wu_ref, wd_ref, sgu_ref, sd_ref, lg_ref, lb_ref, o_ref, o2_ref, acc,
              *, prompt_tiles):
    j = pl.program_id(1)
    xb = xb_ref[...]
    g = pltpu.roll(g_ref[...], (LANES - j * MOE_GROUP) % LANES, 1)
    scaled = [_silu(_nn(xb, wg_ref[c])) * _nn(xb, wu_ref[c]) * g[:, c:c + 1] for c in range(MOE_GROUP)]
    part = _nn(jnp.concatenate(scaled, axis=-1).astype(BF16), wd_ref[0])

    @pl.when(j == 0)
    def _():
        acc[...] = part

    @pl.when(j > 0)
    def _():
        acc[...] += part

    @pl.when(j == pl.num_programs(1) - 1)
    def _():
        shared = _nn(_swiglu(xb, sgu_ref[...], EXPERT_HIDDEN).astype(BF16), sd_ref[...])
        x2 = _layer_norm(DN_ALPHA * x_ref[...] + (acc[...] + shared), lg_ref[...], lb_ref[...])
        if prompt_tiles is None:
            o_ref[...] = x2
            o2_ref[...] = x2.astype(BF16)
        else:
            @pl.when(pl.program_id(0) < prompt_tiles)
            def _():
                o_ref[...] = x2

            @pl.when(pl.program_id(0) >= prompt_tiles)
            def _():
                o2_ref[...] = x2


def _moe_call(x1b, x1, gates, p, name, n_prompt=None):
    n = x1.shape[0]
    layer = p["layer"]
    tm = _tile(n if n_prompt is None else math.gcd(n_prompt, n - n_prompt), (512, 256, 128))
    width = MOE_GROUP * EXPERT_HIDDEN
    rowblk = lambda w: pl.BlockSpec((tm, w), lambda i, j: (i, 0))
    if n_prompt is None:
        pt = None
        out_specs = [rowblk(D_MODEL), rowblk(D_MODEL)]
        out_shape = [jax.ShapeDtypeStruct((n, D_MODEL), F32), jax.ShapeDtypeStruct((n, D_MODEL), BF16)]
        semantics = ("parallel", "arbitrary")
    else:
        pt = n_prompt // tm
        out_specs = [pl.BlockSpec((tm, D_MODEL), lambda i, j: (jnp.minimum(i, pt - 1), 0)),
                     pl.BlockSpec((tm, D_MODEL), lambda i, j: (jnp.maximum(i - pt, 0), 0))]
        out_shape = [jax.ShapeDtypeStruct((n_prompt, D_MODEL), F32),
                     jax.ShapeDtypeStruct((n - n_prompt, D_MODEL), F32)]
        semantics = ("arbitrary", "arbitrary")
    return pl.pallas_call(
        functools.partial(_moe_body, prompt_tiles=pt),
        grid=(n // tm, N_EXPERTS // MOE_GROUP),
        in_specs=[rowblk(D_MODEL), rowblk(D_MODEL), rowblk(LANES),
                  pl.BlockSpec((None, MOE_GROUP, D_MODEL, EXPERT_HIDDEN), lambda i, j: (layer, j, 0, 0)),
                  pl.BlockSpec((None, MOE_GROUP, D_MODEL, EXPERT_HIDDEN), lambda i, j: (layer, j, 0, 0)),
                  pl.BlockSpec((None, 1, width, D_MODEL), lambda i, j: (layer, j, 0, 0)),
                  _const_spec((D_MODEL, 2 * EXPERT_HIDDEN)), _const_spec((EXPERT_HIDDEN, D_MODEL)),
                  _const_spec((1, D_MODEL)), _const_spec((1, D_MODEL))],
        out_specs=out_specs,
        out_shape=out_shape,
        scratch_shapes=[pltpu.VMEM((tm, D_MODEL), F32)],
        compiler_params=_params(semantics, 58),
        name=name,
    )(x1b, x1, gates, p["exp_wg"], p["exp_wu"], p["exp_wd"], p["sh_wgu"], p["sh_wd"], p["ln2_g"], p["ln2_b"])


def _block_diag(w):
    k, d, _ = w.shape
    eye = jnp.eye(k, dtype=w.dtype)
    return (eye[:, None, :, None] * w[:, :, None, :]).reshape(k * d, k * d)


def _prep_stacks(w):
    depth = w["exp_w_down"].shape[0]
    return dict(
        w_in_b=w["w_in"].astype(BF16),
        exp_wg_all=w["exp_w_gate"].astype(BF16), exp_wu_all=w["exp_w_up"].astype(BF16),
        exp_wd_all=w["exp_w_down"].reshape(depth, N_EXPERTS // MOE_GROUP, MOE_GROUP * EXPERT_HIDDEN,
                                           D_MODEL).astype(BF16))


def _prep_layer(w, l):
    w_in = w["w_in_b"][l]
    o = 0
    seg = {}
    for nm, width in (("z", D_MODEL), ("xbc", SSD_CONV_DIM), ("dt", SSD_HEADS), ("q", D_MODEL), ("k", D_MODEL),
                      ("v", D_MODEL), ("xl", D_MODEL), ("yl", D_MODEL), ("g", N_BRANCH * D_MODEL)):
        seg[nm] = w_in[:, o:o + width]
        o += width
    row = lambda a: a.reshape(1, -1)
    lane_pad = lambda a: jnp.pad(a.reshape(1, -1), ((0, 0), (0, LANES - a.shape[-1])))
    head_of_channel = jnp.arange(D_MODEL) // SSD_HEAD_DIM
    return dict(
        w_main=jnp.concatenate([seg[k] for k in ("g", "z", "xbc", "xl", "yl")], axis=1).astype(BF16),
        w_qkv=jnp.concatenate([seg["q"] * ATT_SCALE, seg["k"], seg["v"]], axis=1).astype(BF16),
        w_kv=jnp.concatenate([seg["k"], seg["v"]], axis=1).astype(BF16),
        w_dt=jnp.pad(seg["dt"], ((0, 0), (0, LANES - SSD_HEADS))).astype(BF16),
        ssd_conv_w=w["ssd_conv_w"][l], ssd_conv_b=row(w["ssd_conv_b"][l]),
        ssd_dt_bias=lane_pad(w["ssd_dt_bias"][l]), ssd_a_log=lane_pad(w["ssd_a_log"][l]),
        ssd_d_ch=row(w["ssd_d"][l][head_of_channel]), ssd_norm_w=row(w["ssd_norm_w"][l]),
        ssd_expand=(jnp.arange(LANES)[:, None] == head_of_channel[None, :]).astype(F32),
        att_table=w["att_rel_bias"][l],
        lru_conv_w=w["lru_conv_w"][l], lru_conv_b=row(w["lru_conv_b"][l]),
        lru_wa_d=_block_diag(w["lru_wa"][l]).astype(BF16), lru_ba=row(w["lru_ba"][l]),
        lru_wx_d=_block_diag(w["lru_wx"][l]).astype(BF16), lru_bx=row(w["lru_bx"][l]),
        lru_lambda=row(w["lru_lambda"][l]),
        gate_bias=row(w["gate_bias"][l]),
        w_ssd_proj=w["w_ssd_proj"][l].astype(BF16), w_att_proj=w["w_att_proj"][l].astype(BF16),
        w_lru_proj=w["w_lru_proj"][l].astype(BF16), w_out=w["w_out"][l].astype(BF16),
        ln1_g=row(w["ln1_g"][l]), ln1_b=row(w["ln1_b"][l]),
        router_w_t=w["router_w"][l].T, router_bias=w["router_bias"][l].reshape(N_EXPERTS, 1),
        layer=l, exp_wg=w["exp_wg_all"], exp_wu=w["exp_wu_all"], exp_wd=w["exp_wd_all"],
        sh_wgu=jnp.concatenate([w["sh_w_gate"][l], w["sh_w_up"][l]], axis=-1).astype(BF16),
        sh_wd=w["sh_w_down"][l].astype(BF16),
        ln2_g=row(w["ln2_g"][l]), ln2_b=row(w["ln2_b"][l]),
    )


def _heads_state(h_t, nb):
    return jnp.swapaxes(h_t, 1, 2).reshape(nb, SSD_HEADS, SSD_HEAD_DIM, SSD_STATE)


def _layer(x, xb, st, p, geom, l):
    bp, tp, bs, ts = geom
    n_p = bp * tp
    ncp = tp // CHUNK
    keep = min(ATT_LEFT, tp)
    proj = _matmul(xb, p["w_main"], 2 * D_MODEL, f"in_proj_{l}")
    qkv = _matmul(xb, p["w_qkv"], 3 * D_MODEL, f"qkv_proj_{l}", BF16)
    proj_dt = _matmul(xb, p["w_dt"], LANES, f"dt_proj_{l}")
    xb_keep = jnp.concatenate([xb[(b + 1) * tp - keep:(b + 1) * tp] for b in range(bp)] + [xb[n_p:]], axis=0)
    kv_keep = _matmul(xb_keep, p["w_kv"], D_MODEL, f"kv_keep_{l}")

    zeros = lambda *s: jnp.zeros(s, F32)
    ssd_p, pconv, ph = _ssd_call(proj, proj_dt, zeros(bp, CONV_WIDTH - 1, SSD_CONV_DIM),
                                 zeros(bp, SSD_STATE, D_MODEL), p, bp, ncp, 0, f"ssd_p_{l}")
    h0_t = jnp.swapaxes(st["state_ssd"].reshape(bs, D_MODEL, SSD_STATE), 1, 2)
    ssd_s, sconv, sh = _ssd_call(proj, proj_dt, st["cache_ssd_conv"], h0_t, p, bs, ts // CHUNK, n_p // CHUNK,
                                 f"ssd_s_{l}")

    att_rows = _tile(tp, (ATT_LEFT,))
    bias_p = _band_bias(p["att_table"], min(att_rows, ATT_SUB))
    att_p = _att_prompt_call(qkv, bias_p, bp, tp // att_rows, att_rows, f"att_p_{l}")
    bias_s = bias_p[:, :ts, :ts + ATT_LEFT]
    att_s = _att_sample_call(qkv, st["cache_att_kt"], st["cache_att_vt"], bias_s, bs, ts, n_p // ts, l,
                             f"att_s_{l}")

    lru_rows = _tile(tp, (256, 128, 64))
    lru_p, plc, plh = _lru_call(proj, zeros(bp, CONV_WIDTH - 1, D_MODEL), zeros(bp, 1, D_MODEL), p, bp,
                                tp // lru_rows, lru_rows, 0, f"lru_p_{l}")
    lru_s, slc, slh = _lru_call(proj, st["cache_lru_conv"], st["state_lru"].reshape(bs, 1, D_MODEL), p, bs,
                                1, ts, n_p // ts, f"lru_s_{l}")

    x1, x1b = _merge_call(((ssd_p, ssd_s), (att_p, att_s), (lru_p, lru_s)), proj, x, p, f"merge_{l}")
    gates_t = _router_call(x1, p, f"router_{l}")
    gates = jnp.pad(gates_t.T, ((0, 0), (0, LANES - N_EXPERTS)))
    x2, x2b = _moe_call(x1b, x1, gates, p, f"moe_{l}", n_p if l == DEPTH - 1 else None)

    def kv(col, rows0, nb, t):
        return kv_keep[rows0:rows0 + nb * t, col * D_MODEL:(col + 1) * D_MODEL].reshape(
            nb, t, ATT_HEADS, ATT_HEAD_DIM)

    states = dict(
        p_ssd_conv=pconv, s_ssd_conv=sconv, p_ssd_state=_heads_state(ph, bp), s_ssd_state=_heads_state(sh, bs),
        p_att_k=kv(0, 0, bp, keep), s_att_k=kv(0, bp * keep, bs, ts),
        p_att_v=kv(1, 0, bp, keep), s_att_v=kv(1, bp * keep, bs, ts),
        p_lru_conv=plc, s_lru_conv=slc, p_lru_state=plh.reshape(bp, D_MODEL), s_lru_state=slh.reshape(bs, D_MODEL))
    return x2, x2b, states


def kernel(x_prompt, x_sample, cache_ssd_conv, state_ssd, cache_att_k, cache_att_v, cache_lru_conv, state_lru, w_in, ssd_conv_w, ssd_conv_b, ssd_dt_bias, ssd_a_log, ssd_d, ssd_norm_w, att_rel_bias, lru_conv_w, lru_conv_b, lru_wa, lru_ba, lru_wx, lru_bx, lru_lambda, gate_bias, w_ssd_proj, w_att_proj, w_lru_proj, w_out, ln1_g, ln1_b, router_w, router_bias, exp_w_gate, exp_w_up, exp_w_down, sh_w_gate, sh_w_up, sh_w_down, ln2_g, ln2_b):
    w = dict(w_in=w_in, ssd_conv_w=ssd_conv_w, ssd_conv_b=ssd_conv_b, ssd_dt_bias=ssd_dt_bias, ssd_a_log=ssd_a_log,
             ssd_d=ssd_d, ssd_norm_w=ssd_norm_w, att_rel_bias=att_rel_bias, lru_conv_w=lru_conv_w,
             lru_conv_b=lru_conv_b, lru_wa=lru_wa, lru_ba=lru_ba, lru_wx=lru_wx, lru_bx=lru_bx,
             lru_lambda=lru_lambda, gate_bias=gate_bias, w_ssd_proj=w_ssd_proj, w_att_proj=w_att_proj,
             w_lru_proj=w_lru_proj, w_out=w_out, ln1_g=ln1_g, ln1_b=ln1_b, router_w=router_w,
             router_bias=router_bias, exp_w_gate=exp_w_gate, exp_w_up=exp_w_up, exp_w_down=exp_w_down,
             sh_w_gate=sh_w_gate, sh_w_up=sh_w_up, sh_w_down=sh_w_down, ln2_g=ln2_g, ln2_b=ln2_b)
    bp, tp, _ = x_prompt.shape
    bs, ts, _ = x_sample.shape
    n_p = bp * tp
    x = jnp.concatenate([x_prompt.reshape(n_p, D_MODEL), x_sample.reshape(bs * ts, D_MODEL)], axis=0)
    xb = jnp.concatenate([x_prompt.reshape(n_p, D_MODEL).astype(BF16),
                          x_sample.reshape(bs * ts, D_MODEL).astype(BF16)], axis=0)
    w.update(_prep_stacks(w))
    cache_att_kt = jnp.transpose(cache_att_k, (0, 1, 3, 4, 2))
    cache_att_vt = jnp.transpose(cache_att_v, (0, 1, 3, 4, 2))
    per_layer = []
    for l in range(DEPTH):
        st = dict(cache_ssd_conv=cache_ssd_conv[l], state_ssd=state_ssd[l], cache_att_kt=cache_att_kt,
                  cache_att_vt=cache_att_vt, cache_lru_conv=cache_lru_conv[l], state_lru=state_lru[l])
        x, xb, states = _layer(x, xb, st, _prep_layer(w, l), (bp, tp, bs, ts), l)
        per_layer.append(states)
    y_prompt, y_sample = x, xb
    stack = lambda k: jnp.stack([s[k] for s in per_layer], axis=0)
    return (y_prompt.reshape(bp, tp, D_MODEL), y_sample.reshape(bs, ts, D_MODEL),
            stack("p_ssd_conv"), stack("s_ssd_conv"), stack("p_ssd_state"), stack("s_ssd_state"),
            stack("p_att_k"), stack("s_att_k"), stack("p_att_v"), stack("s_att_v"),
            stack("p_lru_conv"), stack("s_lru_conv"), stack("p_lru_state"), stack("s_lru_state"))
```
